```python
import math
import jax, jax.numpy as jnp
from jax import lax
import numpy as np

D_MODEL = 1024
BATCH = 8
SEQ = 2048
DEPTH = 4
DEC_BATCH = 128
DEC_SEQ = 1
PAST_LEN = 8192
PAGE_SIZE = 128

N_EVEN = (DEPTH + 1) // 2
N_ODD = DEPTH // 2

S5_WIDTH = D_MODEL // 2
S5_GROUP = 16
S5_GROUPS = S5_WIDTH // S5_GROUP
S5_STATE = 64
SWA_HEADS = 8
SWA_KV_HEADS = 2
SWA_HEAD_DIM = 64
SWA_WIDTH = SWA_HEADS * SWA_HEAD_DIM
WINDOW = 128
SWA_BLOCK = WINDOW
D_IN_EVEN = S5_WIDTH + SWA_WIDTH + 2 * SWA_KV_HEADS * SWA_HEAD_DIM
D_MIX_EVEN = S5_WIDTH + SWA_WIDTH
SSD_INNER = 2 * D_MODEL
SSD_HEAD_DIM = 64
SSD_HEADS = SSD_INNER // SSD_HEAD_DIM
SSD_GROUPS = 4
SSD_STATE = 128
SSD_CONV = 4
SSD_CHUNK = 128
SSD_CONV_DIM = SSD_INNER + 2 * SSD_GROUPS * SSD_STATE
D_IN_ODD = SSD_INNER + SSD_CONV_DIM + SSD_HEADS
PEER_HEADS = 8
PEER_KEYS = 128
PEER_EXPERTS = PEER_KEYS * PEER_KEYS
PEER_DK = 256
PEER_TOPK = 16
PEER_BLOCK = 256
DN_ALPHA = (2 * DEPTH) ** 0.25
DN_BETA = (8 * DEPTH) ** -0.25
LN_EPS = 1e-5
RMS_EPS = 1e-5
NEG_INF = -1e30

kernel_name = 'hybrid_s5_swa_ssd_peer_step'


def _layernorm(x, g, b):
    xf = x.astype(jnp.float32)
    mu = jnp.mean(xf, axis=-1, keepdims=True)
    var = jnp.mean(jnp.square(xf - mu), axis=-1, keepdims=True)
    y = (xf - mu) * lax.rsqrt(var + LN_EPS) * g.astype(jnp.float32) + b.astype(jnp.float32)
    return y.astype(x.dtype)


def _cmul(ar, ai, br, bi):
    return ar * br - ai * bi, ar * bi + ai * br


def _s5(u, h0_re, h0_im, lam_re, lam_im, log_dt, b_re, b_im, c_re, c_im, d_skip, w_glu):
    f32 = jnp.float32
    bsz, L, _ = u.shape
    uf = u.astype(f32)
    ug = uf.reshape(bsz, L, S5_GROUPS, S5_GROUP)
    lr, li = lam_re.astype(f32), lam_im.astype(f32)
    dt = jnp.exp(log_dt.astype(f32))[:, None]
    mag = jnp.exp(lr * dt)
    ab_re, ab_im = mag * jnp.cos(li * dt), mag * jnp.sin(li * dt)
    den = lr * lr + li * li
    nr, ni = ab_re - 1.0, ab_im
    coef_re = (nr * lr + ni * li) / den
    coef_im = (ni * lr - nr * li) / den
    bb_re, bb_im = _cmul(coef_re[..., None], coef_im[..., None], b_re.astype(f32), b_im.astype(f32))
    bu_re = jnp.einsum('blgc,gpc->blgp', ug, bb_re)
    bu_im = jnp.einsum('blgc,gpc->blgp', ug, bb_im)
    h0r, h0i = _cmul(ab_re, ab_im, h0_re.astype(f32), h0_im.astype(f32))
    bu_re = bu_re.at[:, 0].add(h0r)
    bu_im = bu_im.at[:, 0].add(h0i)
    a_re = jnp.broadcast_to(ab_re, bu_re.shape)
    a_im = jnp.broadcast_to(ab_im, bu_im.shape)

    def combine(e1, e2):
        a1r, a1i, b1r, b1i = e1
        a2r, a2i, b2r, b2i = e2
        ar, ai = _cmul(a2r, a2i, a1r, a1i)
        tr, ti = _cmul(a2r, a2i, b1r, b1i)
        return ar, ai, tr + b2r, ti + b2i

    _, _, h_re, h_im = lax.associative_scan(combine, (a_re, a_im, bu_re, bu_im), axis=1)
    y = (jnp.einsum('blgp,gcp->blgc', h_re, c_re.astype(f32))
         - jnp.einsum('blgp,gcp->blgc', h_im, c_im.astype(f32)))
    y = y.reshape(bsz, L, S5_WIDTH) + d_skip.astype(f32) * uf
    g = jax.nn.gelu(y)
    out = g * jax.nn.sigmoid(g @ w_glu.astype(f32))
    return out.astype(u.dtype), h_re[:, -1].astype(h0_re.dtype), h_im[:, -1].astype(h0_im.dtype)


def _alibi_slopes():
    return 2.0 ** (-8.0 * jnp.arange(1, SWA_HEADS + 1, dtype=jnp.float32) / SWA_HEADS)


def _swa(q, k, v, k_buf, v_buf, start, sinks):
    f32 = jnp.float32
    bsz, L = q.shape[0], q.shape[1]
    w = k_buf.shape[1]
    rep = SWA_HEADS // SWA_KV_HEADS
    k_cat = jnp.concatenate([k_buf.astype(k.dtype), k], axis=1)
    v_cat = jnp.concatenate([v_buf.astype(v.dtype), v], axis=1)
    new_k, new_v = k_cat[:, -w:], v_cat[:, -w:]
    nb = -(-L // SWA_BLOCK)
    lp = nb * SWA_BLOCK
    pad = lp - L
    qp = jnp.pad(q, ((0, 0), (0, pad), (0, 0), (0, 0)))
    kp = jnp.pad(k_cat, ((0, 0), (0, pad), (0, 0), (0, 0)))
    vp = jnp.pad(v_cat, ((0, 0), (0, pad), (0, 0), (0, 0)))
    qb = qp.reshape(bsz, nb, SWA_BLOCK, SWA_KV_HEADS, rep, SWA_HEAD_DIM).astype(f32)
    kb = kp.reshape(bsz, nb + 1, SWA_BLOCK, SWA_KV_HEADS, SWA_HEAD_DIM)
    vb = vp.reshape(bsz, nb + 1, SWA_BLOCK, SWA_KV_HEADS, SWA_HEAD_DIM)
    kb = jnp.concatenate([kb[:, :-1], kb[:, 1:]], axis=2).astype(f32)
    vb = jnp.concatenate([vb[:, :-1], vb[:, 1:]], axis=2).astype(f32)
    qpos = start + jnp.arange(lp, dtype=jnp.int32).reshape(nb, SWA_BLOCK)
    kpos_all = start - w + jnp.arange(w + lp, dtype=jnp.int32).reshape(nb + 1, SWA_BLOCK)
    kpos = jnp.concatenate([kpos_all[:-1], kpos_all[1:]], axis=1)
    dist = qpos[:, :, None] - kpos[:, None, :]
    valid = (dist >= 0) & (dist < WINDOW) & (kpos[:, None, :] >= 0)
    s = jnp.einsum('bnqgrd,bnkgd->bngrqk', qb, kb) * (SWA_HEAD_DIM ** -0.5)
    slopes = _alibi_slopes().reshape(SWA_KV_HEADS, rep)[None, None, :, :, None, None]
    s = s - slopes * dist[None, :, None, None].astype(f32)
    s = jnp.where(valid[None, :, None, None], s, NEG_INF)
    sink = sinks.astype(f32).reshape(SWA_KV_HEADS, rep)[None, None, :, :, None, None]
    m = jnp.maximum(jnp.max(s, axis=-1, keepdims=True), sink)
    p = jnp.exp(s - m)
    p = p / (jnp.sum(p, axis=-1, keepdims=True) + jnp.exp(sink - m))
    o = jnp.einsum('bngrqk,bnkgd->bnqgrd', p, vb)
    o = o.reshape(bsz, lp, SWA_WIDTH)[:, :L]
    return o.astype(q.dtype), new_k.astype(k_buf.dtype), new_v.astype(v_buf.dtype)


def _even_mixer(x, h_re, h_im, k_buf, v_buf, start, w_in, lam_re, lam_im, log_dt,
                b_re, b_im, c_re, c_im, s5_d, w_glu, sinks, w_out):
    bsz, L, _ = x.shape
    kv_w = SWA_KV_HEADS * SWA_HEAD_DIM
    proj = x @ w_in
    u_a, q, k, v = jnp.split(proj, [S5_WIDTH, S5_WIDTH + SWA_WIDTH, S5_WIDTH + SWA_WIDTH + kv_w], axis=-1)
    y_a, h_re, h_im = _s5(u_a, h_re, h_im, lam_re, lam_im, log_dt, b_re, b_im, c_re, c_im, s5_d, w_glu)
    q = q.reshape(bsz, L, SWA_HEADS, SWA_HEAD_DIM)
    k = k.reshape(bsz, L, SWA_KV_HEADS, SWA_HEAD_DIM)
    v = v.reshape(bsz, L, SWA_KV_HEADS, SWA_HEAD_DIM)
    y_b, k_buf, v_buf = _swa(q, k, v, k_buf, v_buf, start, sinks)
    out = jnp.concatenate([y_a, y_b], axis=-1) @ w_out
    return out, h_re, h_im, k_buf, v_buf


def _causal_conv(xbc, buf, w, b):
    L = xbc.shape[1]
    xp = jnp.concatenate([buf.astype(xbc.dtype), xbc], axis=1)
    y = xp[:, 0:L] * w[0]
    for tap in range(1, SSD_CONV):
        y = y + xp[:, tap:tap + L] * w[tap]
    return jax.nn.silu(y + b), xp[:, -(SSD_CONV - 1):].astype(buf.dtype)


def _ssd(x, dt, a, bm, cm, s0):
    bsz, L = x.shape[0], x.shape[1]
    r = SSD_HEADS // SSD_GROUPS
    q = min(SSD_CHUNK, L)
    nc = -(-L // q)
    pad = nc * q - L
    x = jnp.pad(x, ((0, 0), (0, pad), (0, 0), (0, 0))).reshape(bsz, nc, q, SSD_GROUPS, r, SSD_HEAD_DIM)
    dt = jnp.pad(dt, ((0, 0), (0, pad), (0, 0))).reshape(bsz, nc, q, SSD_GROUPS, r)
    bm = jnp.pad(bm, ((0, 0), (0, pad), (0, 0), (0, 0))).reshape(bsz, nc, q, SSD_GROUPS, SSD_STATE)
    cm = jnp.pad(cm, ((0, 0), (0, pad), (0, 0), (0, 0))).reshape(bsz, nc, q, SSD_GROUPS, SSD_STATE)
    acs = jnp.cumsum(dt * a.reshape(SSD_GROUPS, r), axis=2)
    seg = acs[:, :, :, None] - acs[:, :, None, :]
    causal = (jnp.arange(q)[:, None] >= jnp.arange(q)[None, :])[None, None, :, :, None, None]
    decay = jnp.where(causal, jnp.exp(jnp.where(causal, seg, 0.0)), 0.0)
    cb = jnp.einsum('bctgn,bcsgn->bctsg', cm, bm)
    dx = dt[..., None] * x
    y_diag = jnp.einsum('bctsgr,bcsgrp->bctgrp', cb[..., None] * decay, dx)
    dstate = jnp.exp(acs[:, :, -1:] - acs)
    states = jnp.einsum('bcsgn,bcsgrp->bcgrpn', bm, dstate[..., None] * dx)
    chunk_decay = jnp.exp(acs[:, :, -1])

    def step(carry, inp):
        dec, st = inp
        return dec[..., None, None] * carry + st, carry

    init = s0.reshape(bsz, SSD_GROUPS, r, SSD_HEAD_DIM, SSD_STATE)
    final, s_in = lax.scan(step, init, (jnp.moveaxis(chunk_decay, 1, 0), jnp.moveaxis(states, 1, 0)))
    s_in = jnp.moveaxis(s_in, 0, 1)
    y_off = jnp.einsum('bctgn,bcgrpn->bctgrp', cm, s_in) * jnp.exp(acs)[..., None]
    y = (y_diag + y_off).reshape(bsz, nc * q, SSD_HEADS, SSD_HEAD_DIM)[:, :L]
    return y, final.reshape(bsz, SSD_HEADS, SSD_HEAD_DIM, SSD_STATE)


def _odd_mixer(x, ssm_state, conv_buf, w_in, conv_w, conv_b, dt_bias, a_log, d_skip, norm_w, w_out):
    f32 = jnp.float32
    bsz, L, _ = x.shape
    proj = x @ w_in
    z, xbc, dt_raw = jnp.split(proj, [SSD_INNER, SSD_INNER + SSD_CONV_DIM], axis=-1)
    xbc, conv_buf = _causal_conv(xbc, conv_buf, conv_w, conv_b)
    xs, bm, cm = jnp.split(xbc.astype(f32), [SSD_INNER, SSD_INNER + SSD_GROUPS * SSD_STATE], axis=-1)
    xs = xs.reshape(bsz, L, SSD_HEADS, SSD_HEAD_DIM)
    bm = bm.reshape(bsz, L, SSD_GROUPS, SSD_STATE)
    cm = cm.reshape(bsz, L, SSD_GROUPS, SSD_STATE)
    dt = jax.nn.softplus(dt_raw.astype(f32) + dt_bias.astype(f32))
    a = -jnp.exp(a_log.astype(f32))
    y, new_state = _ssd(xs, dt, a, bm, cm, ssm_state.astype(f32))
    y = y + d_skip.astype(f32)[:, None] * xs
    y = y.reshape(bsz, L, SSD_INNER) * jax.nn.silu(z.astype(f32))
    y = y * lax.rsqrt(jnp.mean(jnp.square(y), axis=-1, keepdims=True) + RMS_EPS) * norm_w.astype(f32)
    out = y.astype(x.dtype) @ w_out
    return out, new_state.astype(ssm_state.dtype), conv_buf


def _peer(x, w_q, k1, k2, u_tab, v_tab):
    shape = x.shape
    xt = x.reshape(-1, D_MODEL)
    t = xt.shape[0]
    blk = min(PEER_BLOCK, t)
    nb = -(-t // blk)
    xb = jnp.pad(xt, ((0, nb * blk - t), (0, 0))).reshape(nb, blk, D_MODEL)
    half = PEER_DK // 2

    def block(xc):
        q = (xc @ w_q).reshape(blk, PEER_HEADS, PEER_DK)
        s1 = jnp.einsum('thd,hkd->thk', q[..., :half], k1).astype(jnp.float32)
        s2 = jnp.einsum('thd,hkd->thk', q[..., half:], k2).astype(jnp.float32)
        v1, i1 = lax.top_k(s1, PEER_TOPK)
        v2, i2 = lax.top_k(s2, PEER_TOPK)
        cand = (v1[..., :, None] + v2[..., None, :]).reshape(blk, PEER_HEADS, PEER_TOPK * PEER_TOPK)
        cidx = (i1[..., :, None] * PEER_KEYS + i2[..., None, :]).reshape(blk, PEER_HEADS, PEER_TOPK * PEER_TOPK)
        sv, pos = lax.top_k(cand, PEER_TOPK)
        idx = jnp.take_along_axis(cidx, pos, axis=-1)
        gates = jax.nn.softmax(sv, axis=-1)
        u_sel = jnp.take(u_tab, idx, axis=0)
        h = jnp.einsum('thkd,td->thk', u_sel, xc).astype(jnp.float32)
        act = (jax.nn.gelu(h) * gates).astype(xc.dtype)
        v_sel = jnp.take(v_tab, idx, axis=0)
        return jnp.einsum('thk,thkd->td', act, v_sel)

    y = lax.map(block, xb).reshape(nb * blk, D_MODEL)[:t]
    return y.reshape(shape).astype(x.dtype)


def _trunk(x, s5_re, s5_im, k_buf, v_buf, ssm, conv, start, even_w, odd_w, chan_w):
    (w_in_e, lam_re, lam_im, log_dt, b_re, b_im, c_re, c_im, s5_d, w_glu, sinks, w_out_e) = even_w
    (w_in_o, conv_w, conv_b, dt_bias, a_log, ssd_d, norm_w, w_out_o) = odd_w
    (ln1_g, ln1_b, ln2_g, ln2_b, w_q, k1, k2, u_tab, v_tab) = chan_w
    n_re, n_im, n_k, n_v, n_ssm, n_conv = [], [], [], [], [], []
    for layer in range(DEPTH):
        i = layer // 2
        if layer % 2 == 0:
            mix, hr, hi, kb, vb = _even_mixer(x, s5_re[i], s5_im[i], k_buf[i], v_buf[i], start, w_in_e[i],
                                              lam_re[i], lam_im[i], log_dt[i], b_re[i], b_im[i], c_re[i],
                                              c_im[i], s5_d[i], w_glu[i], sinks[i], w_out_e[i])
            n_re.append(hr)
            n_im.append(hi)
            n_k.append(kb)
            n_v.append(vb)
        else:
            mix, st, cb = _odd_mixer(x, ssm[i], conv[i], w_in_o[i], conv_w[i], conv_b[i], dt_bias[i],
                                     a_log[i], ssd_d[i], norm_w[i], w_out_o[i])
            n_ssm.append(st)
            n_conv.append(cb)
        x = _layernorm(DN_ALPHA * x + mix, ln1_g[layer], ln1_b[layer])
        x = _layernorm(DN_ALPHA * x + _peer(x, w_q[layer], k1[layer], k2[layer], u_tab[layer], v_tab[layer]),
                       ln2_g[layer], ln2_b[layer])
    return (x, jnp.stack(n_re), jnp.stack(n_im), jnp.stack(n_k), jnp.stack(n_v),
            jnp.stack(n_ssm), jnp.stack(n_conv))


def setup_inputs(seed: int = 0) -> dict:
    key = jax.random.key(seed)
    ks = iter(jax.random.split(key, 48))
    f32 = jnp.float32

    def nrm(shape, scale):
        return jax.random.normal(next(ks), shape, f32) * scale

    w_buf = min(WINDOW, PAST_LEN)
    inp = {}
    inp['x_prompt'] = nrm((BATCH, SEQ, D_MODEL), 1.0)
    inp['x_sample'] = nrm((DEC_BATCH, DEC_SEQ, D_MODEL), 1.0)
    inp['state_s5_re'] = nrm((N_EVEN, DEC_BATCH, S5_GROUPS, S5_STATE), 0.3)
    inp['state_s5_im'] = nrm((N_EVEN, DEC_BATCH, S5_GROUPS, S5_STATE), 0.3)
    inp['cache_swa_k'] = nrm((N_EVEN, DEC_BATCH, w_buf, SWA_KV_HEADS, SWA_HEAD_DIM), 1.0)
    inp['cache_swa_v'] = nrm((N_EVEN, DEC_BATCH, w_buf, SWA_KV_HEADS, SWA_HEAD_DIM), 1.0)
    inp['state_ssd'] = nrm((N_ODD, DEC_BATCH, SSD_HEADS, SSD_HEAD_DIM, SSD_STATE), 0.1)
    inp['state_conv'] = nrm((N_ODD, DEC_BATCH, SSD_CONV - 1, SSD_CONV_DIM), 1.0)
    inp['w_in_even'] = nrm((N_EVEN, D_MODEL, D_IN_EVEN), D_MODEL ** -0.5)
    inp['s5_lambda_re'] = -0.5 + nrm((N_EVEN, S5_GROUPS, S5_STATE), 0.01)
    inp['s5_lambda_im'] = (math.pi * jnp.arange(S5_STATE, dtype=f32))[None, None, :] + nrm((N_EVEN, S5_GROUPS, S5_STATE), 0.01)
    inp['s5_log_dt'] = jax.random.uniform(next(ks), (N_EVEN, S5_GROUPS), f32, math.log(1e-3), math.log(1e-1))
    inp['s5_b_re'] = nrm((N_EVEN, S5_GROUPS, S5_STATE, S5_GROUP), (2 * S5_GROUP) ** -0.5)
    inp['s5_b_im'] = nrm((N_EVEN, S5_GROUPS, S5_STATE, S5_GROUP), (2 * S5_GROUP) ** -0.5)
    inp['s5_c_re'] = nrm((N_EVEN, S5_GROUPS, S5_GROUP, S5_STATE), (2 * S5_STATE) ** -0.5)
    inp['s5_c_im'] = nrm((N_EVEN, S5_GROUPS, S5_GROUP, S5_STATE), (2 * S5_STATE) ** -0.5)
    inp['s5_d'] = nrm((N_EVEN, S5_WIDTH), 1.0)
    inp['s5_w_glu'] = nrm((N_EVEN, S5_WIDTH, S5_WIDTH), S5_WIDTH ** -0.5)
    inp['swa_sinks'] = nrm((N_EVEN, SWA_HEADS), 0.5)
    inp['w_out_even'] = nrm((N_EVEN, D_MIX_EVEN, D_MODEL), DN_BETA * D_MIX_EVEN ** -0.5)
    inp['w_in_odd'] = nrm((N_ODD, D_MODEL, D_IN_ODD), D_MODEL ** -0.5)
    inp['ssd_conv_w'] = nrm((N_ODD, SSD_CONV, SSD_CONV_DIM), SSD_CONV ** -0.5)
    inp['ssd_conv_b'] = nrm((N_ODD, SSD_CONV_DIM), 0.01)
    dt0 = jnp.exp(jax.random.uniform(next(ks), (N_ODD, SSD_HEADS), f32, math.log(1e-3), math.log(1e-1)))
    inp['ssd_dt_bias'] = dt0 + jnp.log(-jnp.expm1(-dt0))
    inp['ssd_a_log'] = jnp.log(jax.random.uniform(next(ks), (N_ODD, SSD_HEADS), f32, 1.0, 16.0))
    inp['ssd_d'] = 1.0 + nrm((N_ODD, SSD_HEADS), 0.1)
    inp['ssd_norm_w'] = 1.0 + nrm((N_ODD, SSD_INNER), 0.01)
    inp['w_out_odd'] = nrm((N_ODD, SSD_INNER, D_MODEL), DN_BETA * SSD_INNER ** -0.5)
    inp['ln1_g'] = 1.0 + nrm((DEPTH, D_MODEL), 0.01)
    inp['ln1_b'] = nrm((DEPTH, D_MODEL), 0.01)
    inp['ln2_g'] = 1.0 + nrm((DEPTH, D_MODEL), 0.01)
    inp['ln2_b'] = nrm((DEPTH, D_MODEL), 0.01)
    inp['peer_w_q'] = nrm((DEPTH, D_MODEL, PEER_HEADS * PEER_DK), D_MODEL ** -0.5)
    inp['peer_k1'] = nrm((DEPTH, PEER_HEADS, PEER_KEYS, PEER_DK // 2), (PEER_DK // 2) ** -0.5)
    inp['peer_k2'] = nrm((DEPTH, PEER_HEADS, PEER_KEYS, PEER_DK // 2), (PEER_DK // 2) ** -0.5)
    inp['peer_u'] = nrm((DEPTH, PEER_EXPERTS, D_MODEL), D_MODEL ** -0.5)
    inp['peer_v'] = nrm((DEPTH, PEER_EXPERTS, D_MODEL), DN_BETA * PEER_HEADS ** -0.5)
    return inp


def reference(x_prompt, x_sample, state_s5_re, state_s5_im, cache_swa_k, cache_swa_v, state_ssd, state_conv,
              w_in_even, s5_lambda_re, s5_lambda_im, s5_log_dt, s5_b_re, s5_b_im, s5_c_re, s5_c_im, s5_d,
              s5_w_glu, swa_sinks, w_out_even,
              w_in_odd, ssd_conv_w, ssd_conv_b, ssd_dt_bias, ssd_a_log, ssd_d, ssd_norm_w, w_out_odd,
              ln1_g, ln1_b, ln2_g, ln2_b, peer_w_q, peer_k1, peer_k2, peer_u, peer_v):
    even_w = (w_in_even, s5_lambda_re, s5_lambda_im, s5_log_dt, s5_b_re, s5_b_im, s5_c_re, s5_c_im,
              s5_d, s5_w_glu, swa_sinks, w_out_even)
    odd_w = (w_in_odd, ssd_conv_w, ssd_conv_b, ssd_dt_bias, ssd_a_log, ssd_d, ssd_norm_w, w_out_odd)
    chan_w = (ln1_g, ln1_b, ln2_g, ln2_b, peer_w_q, peer_k1, peer_k2, peer_u, peer_v)
    bp = x_prompt.shape[0]
    dty = x_prompt.dtype
    z_s5 = jnp.zeros((N_EVEN, bp, S5_GROUPS, S5_STATE), dty)
    z_kv = jnp.zeros((N_EVEN, bp, WINDOW, SWA_KV_HEADS, SWA_HEAD_DIM), dty)
    z_ssm = jnp.zeros((N_ODD, bp, SSD_HEADS, SSD_HEAD_DIM, SSD_STATE), dty)
    z_conv = jnp.zeros((N_ODD, bp, SSD_CONV - 1, SSD_CONV_DIM), dty)
    y_prompt, p_s5_re, p_s5_im, p_swa_k, p_swa_v, p_ssd, p_conv = _trunk(
        x_prompt, z_s5, z_s5, z_kv, z_kv, z_ssm, z_conv, 0, even_w, odd_w, chan_w)
    y_sample, s_s5_re, s_s5_im, s_swa_k, s_swa_v, s_ssd, s_conv = _trunk(
        x_sample, state_s5_re, state_s5_im, cache_swa_k, cache_swa_v, state_ssd, state_conv, PAST_LEN,
        even_w, odd_w, chan_w)
    return (y_prompt, y_sample, p_s5_re, p_s5_im, p_swa_k, p_swa_v, p_ssd, p_conv,
            s_s5_re, s_s5_im, s_swa_k, s_swa_v, s_ssd, s_conv)
```

```python
import functools
import math

import jax
import jax.numpy as jnp
from jax import lax
from jax.experimental import pallas as pl
from jax.experimental.pallas import tpu as pltpu

F32 = jnp.float32
BF16 = jnp.bfloat16

D_MODEL = 1024
DEPTH = 4
PAST_LEN = 8192
S5_WIDTH = 512
S5_GROUP = 16
S5_GROUPS = 32
S5_STATE = 64
S5_FLAT = S5_GROUPS * S5_STATE
SWA_HEADS = 8
SWA_KV_HEADS = 2
SWA_REP = SWA_HEADS // SWA_KV_HEADS
SWA_HEAD_DIM = 64
SWA_WIDTH = SWA_HEADS * SWA_HEAD_DIM
SWA_KV_WIDTH = SWA_KV_HEADS * SWA_HEAD_DIM
WINDOW = 128
SSD_INNER = 2048
SSD_HEAD_DIM = 64
SSD_HEADS = 32
SSD_GROUPS = 4
SSD_HPG = SSD_HEADS // SSD_GROUPS
SSD_GW = SSD_HPG * SSD_HEAD_DIM
SSD_STATE = 128
SSD_CONV = 4
SSD_CHUNK = 128
SSD_CONV_DIM = SSD_INNER + 2 * SSD_GROUPS * SSD_STATE
PEER_HEADS = 8
PEER_KEYS = 128
PEER_EXPERTS = PEER_KEYS * PEER_KEYS
PEER_DK = 256
PEER_TOPK = 16
DN_ALPHA = (2 * DEPTH) ** 0.25
LN_EPS = 1e-5
RMS_EPS = 1e-5
NEG_INF = -1e30

LANES = 128
SUBLANES = 8
VMEM_LIMIT = 56 * 1024 * 1024


def _cp(*sem):
    return pltpu.CompilerParams(dimension_semantics=sem, vmem_limit_bytes=VMEM_LIMIT)


def _ln(x, g, b):
    mu = jnp.mean(x, axis=-1, keepdims=True)
    xc = x - mu
    var = jnp.mean(xc * xc, axis=-1, keepdims=True)
    return xc * lax.rsqrt(var + LN_EPS) * g + b


def _softplus(x):
    return jnp.maximum(x, 0.0) + jnp.log(1.0 + jnp.exp(-jnp.abs(x)))


def _silu(x):
    return x * jax.nn.sigmoid(x)


def _dot(a, b):
    return jnp.dot(a, b, preferred_element_type=F32)


def _dot_nt(a, b):
    return lax.dot_general(a, b, (((1,), (1,)), ((), ())), preferred_element_type=F32)


def _dot_tn(a, b):
    return lax.dot_general(a, b, (((0,), (0,)), ((), ())), preferred_element_type=F32)


def _split3(x):
    x1 = x.astype(BF16)
    r1 = x - x1.astype(F32)
    x2 = r1.astype(BF16)
    r2 = r1 - x2.astype(F32)
    return x1, x2, r2.astype(BF16)


def _dot_exact01(x, m01):
    x1, x2, x3 = _split3(x)
    return _dot(x1, m01) + _dot(x2, m01) + _dot(x3, m01)


def _mm_kernel(x_ref, w_ref, o_ref):
    o_ref[...] = _dot(x_ref[...].astype(BF16), w_ref[...]).astype(o_ref.dtype)


def _mm(x, w, tn=512, out_dtype=F32):
    m, k = x.shape
    n = w.shape[1]
    tm = min(m, 512)
    tn = min(n, tn)
    return pl.pallas_call(
        _mm_kernel,
        grid=(m // tm, n // tn),
        in_specs=[pl.BlockSpec((tm, k), lambda i, j: (i, 0)),
                  pl.BlockSpec((k, tn), lambda i, j: (0, j))],
        out_specs=pl.BlockSpec((tm, tn), lambda i, j: (i, j)),
        out_shape=jax.ShapeDtypeStruct((m, n), out_dtype),
        compiler_params=_cp("parallel", "arbitrary"),
        name="mm",
    )(x, w)


def _mm_ln_kernel(x_ref, w_ref, r_ref, g_ref, b_ref, o_ref):
    y = _dot(x_ref[...].astype(BF16), w_ref[...])
    o_ref[...] = _ln(DN_ALPHA * r_ref[...] + y, g_ref[...], b_ref[...])


def _mm_ln(x, w, res, g, b):
    m, k = x.shape
    n = w.shape[1]
    tm = min(m, 512)
    return pl.pallas_call(
        _mm_ln_kernel,
        grid=(m // tm,),
        in_specs=[pl.BlockSpec((tm, k), lambda i: (i, 0)),
                  pl.BlockSpec((k, n), lambda i: (0, 0)),
                  pl.BlockSpec((tm, n), lambda i: (i, 0)),
                  pl.BlockSpec((1, n), lambda i: (0, 0)),
                  pl.BlockSpec((1, n), lambda i: (0, 0))],
        out_specs=pl.BlockSpec((tm, n), lambda i: (i, 0)),
        out_shape=jax.ShapeDtypeStruct((m, n), F32),
        compiler_params=_cp("parallel"),
        name="mm_ln",
    )(x, w, res, g.reshape(1, n), b.reshape(1, n))


def _s5_kernel(u_ref, h0_ref, bd_ref, ar_ref, ai_ref, cbd_ref, d_ref, wg_ref,
               y_ref, hT_ref, hbuf, state, *, rows, steps):
    c = pl.program_id(0)

    @pl.when(c == 0)
    def _():
        state[...] = h0_ref[...]

    u = u_ref[...]
    hbuf[...] = _dot(u.astype(BF16), bd_ref[...])
    ar = ar_ref[...]
    ai = ai_ref[...]

    def step(t, carry):
        hr, hi = carry
        r0 = pl.multiple_of(t * rows, rows)
        br = hbuf[pl.ds(r0, rows), 0:S5_FLAT]
        bi = hbuf[pl.ds(r0, rows), S5_FLAT:2 * S5_FLAT]
        nr = ar * hr - ai * hi + br
        ni = ar * hi + ai * hr + bi
        hbuf[pl.ds(r0, rows), 0:S5_FLAT] = nr
        hbuf[pl.ds(r0, rows), S5_FLAT:2 * S5_FLAT] = ni
        return nr, ni

    hr, hi = lax.fori_loop(0, steps, step, (state[:, 0:S5_FLAT], state[:, S5_FLAT:2 * S5_FLAT]))
    state[:, 0:S5_FLAT] = hr
    state[:, S5_FLAT:2 * S5_FLAT] = hi

    y = _dot(hbuf[...].astype(BF16), cbd_ref[...]) + d_ref[...] * u
    g = jax.nn.gelu(y)
    y_ref[...] = g * jax.nn.sigmoid(_dot(g.astype(BF16), wg_ref[...]))

    @pl.when(c == pl.num_programs(0) - 1)
    def _():
        hT_ref[...] = state[...]


def _s5(u_tm, h0, prm, rows, steps):
    n = u_tm.shape[0]
    blk = rows * steps
    bd, ar, ai, cbd, d, wg = prm
    const = lambda shape: pl.BlockSpec(shape, lambda c: (0, 0))
    return pl.pallas_call(
        functools.partial(_s5_kernel, rows=rows, steps=steps),
        grid=(n // blk,),
        in_specs=[pl.BlockSpec((blk, S5_WIDTH), lambda c: (c, 0)),
                  const((rows, 2 * S5_FLAT)),
                  const((S5_WIDTH, 2 * S5_FLAT)),
                  const((1, S5_FLAT)), const((1, S5_FLAT)),
                  const((2 * S5_FLAT, S5_WIDTH)),
                  const((1, S5_WIDTH)),
                  const((S5_WIDTH, S5_WIDTH))],
        out_specs=[pl.BlockSpec((blk, S5_WIDTH), lambda c: (c, 0)),
                   const((rows, 2 * S5_FLAT))],
        out_shape=[jax.ShapeDtypeStruct((n, S5_WIDTH), F32),
                   jax.ShapeDtypeStruct((rows, 2 * S5_FLAT), F32)],
        scratch_shapes=[pltpu.VMEM((blk, 2 * S5_FLAT), F32),
                        pltpu.VMEM((rows, 2 * S5_FLAT), F32)],
        compiler_params=_cp("arbitrary"),
        name="s5",
    )(u_tm, h0, bd, ar, ai, cbd, d, wg)


def _s5_params(lam_re, lam_im, log_dt, b_re, b_im, c_re, c_im, d_skip, w_glu):
    dt = jnp.exp(log_dt)[:, None]
    mag = jnp.exp(lam_re * dt)
    ab_re, ab_im = mag * jnp.cos(lam_im * dt), mag * jnp.sin(lam_im * dt)
    den = lam_re * lam_re + lam_im * lam_im
    nr, ni = ab_re - 1.0, ab_im
    coef_re = ((nr * lam_re + ni * lam_im) / den)[..., None]
    coef_im = ((ni * lam_re - nr * lam_im) / den)[..., None]
    bb_re = coef_re * b_re - coef_im * b_im
    bb_im = coef_re * b_im + coef_im * b_re
    eye = jnp.eye(S5_GROUPS, dtype=F32)
    pack_in = lambda bb: jnp.einsum('gpc,gh->gchp', bb, eye).reshape(S5_WIDTH, S5_FLAT)
    bd = jnp.concatenate([pack_in(bb_re), pack_in(bb_im)], axis=1).astype(BF16)
    pack_out = lambda cc: jnp.einsum('gcp,gh->gphc', cc, eye).reshape(S5_FLAT, S5_WIDTH)
    cbd = jnp.concatenate([pack_out(c_re), -pack_out(c_im)], axis=0).astype(BF16)
    return (bd, ab_re.reshape(1, S5_FLAT), ab_im.reshape(1, S5_FLAT), cbd,
            d_skip.reshape(1, S5_WIDTH), w_glu.astype(BF16))


def _swa_kernel(sink_ref, q_ref, kp_ref, kc_ref, vp_ref, vc_ref, o_ref):
    n = pl.program_id(1)
    q = q_ref[0].astype(BF16)
    k = jnp.concatenate([kp_ref[0], kc_ref[0]], axis=0).astype(BF16)
    v = jnp.concatenate([vp_ref[0], vc_ref[0]], axis=0).astype(BF16)
    qi = lax.broadcasted_iota(jnp.int32, (WINDOW, 2 * WINDOW), 0)
    kj = lax.broadcasted_iota(jnp.int32, (WINDOW, 2 * WINDOW), 1)
    dist = WINDOW + qi - kj
    valid = (dist >= 0) & (dist < WINDOW) & ((n > 0) | (kj >= WINDOW))
    distf = dist.astype(F32)
    outs = []
    for h in range(SWA_HEADS):
        g = h // SWA_REP
        qh = q[:, h * SWA_HEAD_DIM:(h + 1) * SWA_HEAD_DIM]
        kg = k[:, g * SWA_HEAD_DIM:(g + 1) * SWA_HEAD_DIM]
        vg = v[:, g * SWA_HEAD_DIM:(g + 1) * SWA_HEAD_DIM]
        s = _dot_nt(qh, kg) * (SWA_HEAD_DIM ** -0.5)
        s = s - (2.0 ** -(h + 1)) * distf
        s = jnp.where(valid, s, NEG_INF)
        sink = sink_ref[h]
        m = jnp.maximum(jnp.max(s, axis=-1, keepdims=True), sink)
        p = jnp.exp(s - m)
        p = p / (jnp.sum(p, axis=-1, keepdims=True) + jnp.exp(sink - m))
        outs.append(_dot(p.astype(BF16), vg))
    o_ref[0] = jnp.concatenate(outs, axis=-1)


def _swa_prompt(proj, sinks):
    bsz, seq, _ = proj.shape
    kcol = (S5_WIDTH + SWA_WIDTH) // SWA_KV_WIDTH
    prev = lambda b, n: (b, jnp.maximum(n - 1, 0), kcol)
    prev_v = lambda b, n: (b, jnp.maximum(n - 1, 0), kcol + 1)
    blk = (1, WINDOW, SWA_KV_WIDTH)
    return pl.pallas_call(
        _swa_kernel,
        grid=(bsz, seq // WINDOW),
        in_specs=[pl.BlockSpec(memory_space=pltpu.SMEM),
                  pl.BlockSpec((1, WINDOW, SWA_WIDTH), lambda b, n: (b, n, 1)),
                  pl.BlockSpec(blk, prev),
                  pl.BlockSpec(blk, lambda b, n: (b, n, kcol)),
                  pl.BlockSpec(blk, prev_v),
                  pl.BlockSpec(blk, lambda b, n: (b, n, kcol + 1))],
        out_specs=pl.BlockSpec((1, WINDOW, SWA_WIDTH), lambda b, n: (b, n, 0)),
        out_shape=jax.ShapeDtypeStruct((bsz, seq, SWA_WIDTH), F32),
        compiler_params=_cp("parallel", "arbitrary"),
        name="swa_prompt",
    )(sinks, proj, proj, proj, proj, proj)


def _per_head_column(values):
    r = lax.broadcasted_iota(jnp.int32, (SWA_REP, 1), 0)
    col = jnp.full((SWA_REP, 1), values[-1], F32)
    for i in range(SWA_REP - 2, -1, -1):
        col = jnp.where(r == i, values[i], col)
    return col


def _swa_step_kernel(sink_ref, p_ref, kc_ref, vc_ref, o_ref, *, nseq, start):
    kj = lax.broadcasted_iota(jnp.int32, (SWA_REP, WINDOW), 1)
    dist = WINDOW - kj
    valid = (dist < WINDOW) & (start - WINDOW + kj >= 0)
    distf = dist.astype(F32)
    for i in range(nseq):
        row = p_ref[i]
        kn = row[:, S5_WIDTH + SWA_WIDTH:S5_WIDTH + SWA_WIDTH + SWA_KV_WIDTH].astype(BF16)
        vn = row[:, S5_WIDTH + SWA_WIDTH + SWA_KV_WIDTH:].astype(BF16)
        kc = kc_ref[i].astype(BF16)
        vc = vc_ref[i].astype(BF16)
        outs = []
        for g in range(SWA_KV_HEADS):
            q4 = jnp.concatenate(
                [row[:, S5_WIDTH + (g * SWA_REP + r) * SWA_HEAD_DIM:S5_WIDTH + (g * SWA_REP + r + 1) * SWA_HEAD_DIM]
                 for r in range(SWA_REP)], axis=0).astype(BF16)
            sl = slice(g * SWA_HEAD_DIM, (g + 1) * SWA_HEAD_DIM)
            slope = _per_head_column([2.0 ** -(g * SWA_REP + r + 1) for r in range(SWA_REP)])
            sink = _per_head_column([sink_ref[g * SWA_REP + r] for r in range(SWA_REP)])
            scale = SWA_HEAD_DIM ** -0.5
            sc = _dot_nt(q4, kc[:, sl]) * scale - slope * distf
            sc = jnp.where(valid, sc, NEG_INF)
            sn = jnp.sum(q4.astype(F32) * kn[:, sl].astype(F32), axis=-1, keepdims=True) * scale
            m = jnp.maximum(jnp.maximum(jnp.max(sc, axis=-1, keepdims=True), sn), sink)
            pc = jnp.exp(sc - m)
            pn = jnp.exp(sn - m)
            den = jnp.sum(pc, axis=-1, keepdims=True) + pn + jnp.exp(sink - m)
            o4 = _dot((pc / den).astype(BF16), vc[:, sl])
            o4 = o4 + (pn / den).astype(BF16).astype(F32) * vn[:, sl].astype(F32)
            outs.extend([o4[r:r + 1, :] for r in range(SWA_REP)])
        o_ref[i] = jnp.concatenate(outs, axis=-1)


def _swa_step(proj, k_cache, v_cache, sinks, start):
    n = proj.shape[0]
    nseq = min(n, SUBLANES)
    out = pl.pallas_call(
        functools.partial(_swa_step_kernel, nseq=nseq, start=start),
        grid=(n // nseq,),
        in_specs=[pl.BlockSpec(memory_space=pltpu.SMEM),
                  pl.BlockSpec((nseq, 1, proj.shape[1]), lambda i: (i, 0, 0)),
                  pl.BlockSpec((nseq, WINDOW, SWA_KV_WIDTH), lambda i: (i, 0, 0)),
                  pl.BlockSpec((nseq, WINDOW, SWA_KV_WIDTH), lambda i: (i, 0, 0))],
        out_specs=pl.BlockSpec((nseq, 1, SWA_WIDTH), lambda i: (i, 0, 0)),
        out_shape=jax.ShapeDtypeStruct((n, 1, SWA_WIDTH), F32),
        compiler_params=_cp("parallel"),
        name="swa_step",
    )(sinks, proj.reshape(n, 1, -1), k_cache, v_cache)
    return out.reshape(n, SWA_WIDTH)


def _head_expand(width):
    r = lax.broadcasted_iota(jnp.int32, (LANES, SSD_HEADS * width), 0)
    c = lax.broadcasted_iota(jnp.int32, (LANES, SSD_HEADS * width), 1)
    return jnp.where((c >= r * width) & (c < (r + 1) * width), 1.0, 0.0).astype(BF16)


def _ssd_kernel(z_ref, xbc_ref, dt_ref, cw_ref, cb_ref, dtb_ref, a_ref, d_ref, nw_ref,
                y_ref, st_ref, tail, state, ybuf):
    c = pl.program_id(1)
    q = SSD_CHUNK

    @pl.when(c == 0)
    def _():
        tail[...] = jnp.zeros_like(tail)
        state[...] = jnp.zeros_like(state)

    cur = xbc_ref[0]
    tl = tail[...]
    row8 = lax.broadcasted_iota(jnp.int32, (SUBLANES, SSD_CONV_DIM), 0)
    acc = None
    for tap in range(SSD_CONV):
        sh = SSD_CONV - 1 - tap
        if sh == 0:
            term = cur
        else:
            rolled = pltpu.roll(cur, sh, 0)
            head = jnp.where(row8 < sh, pltpu.roll(tl, sh, 0), rolled[0:SUBLANES])
            term = jnp.concatenate([head, rolled[SUBLANES:]], axis=0)
        term = term * cw_ref[tap:tap + 1, :]
        acc = term if acc is None else acc + term
    tail[...] = cur[q - SUBLANES:q]
    xbc = _silu(acc + cb_ref[...])
    xs = xbc[:, 0:SSD_INNER]
    bm = xbc[:, SSD_INNER:SSD_INNER + SSD_GROUPS * SSD_STATE].astype(BF16)
    cm = xbc[:, SSD_INNER + SSD_GROUPS * SSD_STATE:].astype(BF16)

    dt = _softplus(dt_ref[0] + dtb_ref[...])
    da = dt * a_ref[...]
    ri = lax.broadcasted_iota(jnp.int32, (q, q), 0)
    ci = lax.broadcasted_iota(jnp.int32, (q, q), 1)
    causal = ri >= ci
    tri = jnp.where(causal, 1.0, 0.0).astype(BF16)
    da1, da2, da3 = _split3(da)
    acs = _dot(tri, da1) + _dot(tri, da2) + _dot(tri, da3)
    acs_t = acs.T
    e64 = _head_expand(SSD_HEAD_DIM)
    dt_x = _dot_exact01(dt, e64)
    acs_x = _dot_exact01(acs, e64)
    last_x = acs_x[q - 1:q, :]
    dx = dt_x * xs
    dsx = (jnp.exp(last_x - acs_x) * dx).astype(BF16)
    eacs_x = jnp.exp(acs_x)
    dxb = dx.astype(BF16)

    for g in range(SSD_GROUPS):
        ns = slice(g * SSD_STATE, (g + 1) * SSD_STATE)
        gs = slice(g * SSD_GW, (g + 1) * SSD_GW)
        cb = _dot_nt(cm[:, ns], bm[:, ns])
        for r in range(SSD_HPG):
            h = g * SSD_HPG + r
            seg = acs[:, h:h + 1] - acs_t[h:h + 1, :]
            decay = jnp.where(causal, jnp.exp(jnp.where(causal, seg, 0.0)), 0.0)
            hs = slice(h * SSD_HEAD_DIM, (h + 1) * SSD_HEAD_DIM)
            ybuf[:, hs] = _dot((cb * decay).astype(BF16), dxb[:, hs])
        s_in = state[g]
        y_off = _dot(cm[:, ns], s_in.astype(BF16)) * eacs_x[:, gs]
        ybuf[:, gs] = ybuf[:, gs] + y_off
        state[g] = jnp.exp(last_x[:, gs]) * s_in + _dot_tn(bm[:, ns], dsx[:, gs])

    y = ybuf[...] + d_ref[...] * xs
    y = y * _silu(z_ref[0])
    y = y * lax.rsqrt(jnp.mean(y * y, axis=-1, keepdims=True) + RMS_EPS) * nw_ref[...]
    y_ref[0] = y.astype(y_ref.dtype)

    @pl.when(c == pl.num_programs(1) - 1)
    def _():
        for g in range(SSD_GROUPS):
            st_ref[0, g * SSD_GW:(g + 1) * SSD_GW, :] = state[g].T


def _ssd_prompt(z, xbc, dtraw, prm):
    bsz, seq, _ = z.shape
    cw, cb, dtb, a, d_x, nw = prm
    const = lambda shape: pl.BlockSpec(shape, lambda b, c: (0, 0))
    seqblk = lambda w: pl.BlockSpec((1, SSD_CHUNK, w), lambda b, c: (b, c, 0))
    return pl.pallas_call(
        _ssd_kernel,
        grid=(bsz, seq // SSD_CHUNK),
        in_specs=[seqblk(SSD_INNER), seqblk(SSD_CONV_DIM), seqblk(LANES),
                  const((SSD_CONV, SSD_CONV_DIM)), const((1, SSD_CONV_DIM)),
                  const((1, LANES)), const((1, LANES)),
                  const((1, SSD_INNER)), const((1, SSD_INNER))],
        out_specs=[seqblk(SSD_INNER),
                   pl.BlockSpec((1, SSD_INNER, SSD_STATE), lambda b, c: (b, 0, 0))],
        out_shape=[jax.ShapeDtypeStruct((bsz, seq, SSD_INNER), BF16),
                   jax.ShapeDtypeStruct((bsz, SSD_INNER, SSD_STATE), F32)],
        scratch_shapes=[pltpu.VMEM((SUBLANES, SSD_CONV_DIM), F32),
                        pltpu.VMEM((SSD_GROUPS, SSD_STATE, SSD_GW), F32),
                        pltpu.VMEM((SSD_CHUNK, SSD_INNER), F32)],
        compiler_params=_cp("parallel", "arbitrary"),
        name="ssd_prompt",
    )(z, xbc, dtraw, cw, cb, dtb, a, d_x, nw)


def _pad_rows(row):
    return jnp.concatenate([row, jnp.zeros((SUBLANES - 1, row.shape[1]), row.dtype)], axis=0)


def _ssd_step_kernel(z_ref, x_ref, dt_ref, buf_ref, s0_ref, cw_ref, cb_ref, dtb_ref, a_ref, d_ref, nw_ref,
                     y_ref, s1_ref, *, nseq):
    e64 = _head_expand(SSD_HEAD_DIM)
    ones3 = jnp.where(lax.broadcasted_iota(jnp.int32, (SUBLANES, SSD_STATE), 0) < 3, 1.0, 0.0).astype(BF16)
    for i in range(nseq):
        buf = buf_ref[i]
        acc = buf[0:1] * cw_ref[0:1, :]
        acc = acc + buf[1:2] * cw_ref[1:2, :]
        acc = acc + buf[2:3] * cw_ref[2:3, :]
        acc = acc + x_ref[i] * cw_ref[3:4, :]
        xbc = _silu(acc + cb_ref[...])
        xs = xbc[:, 0:SSD_INNER]
        bm = xbc[:, SSD_INNER:SSD_INNER + SSD_GROUPS * SSD_STATE].astype(BF16)
        cm = xbc[:, SSD_INNER + SSD_GROUPS * SSD_STATE:].astype(BF16)
        dt = _softplus(dt_ref[i] + dtb_ref[...])
        ed = jnp.exp(dt * a_ref[...])
        dt_x = _dot_exact01(_pad_rows(dt), e64)[0:1]
        ed_x = _dot_exact01(_pad_rows(ed), e64)[0:1]
        dx = dt_x * xs
        dxb = dx.astype(BF16)
        ys = []
        for g in range(SSD_GROUPS):
            ns = slice(g * SSD_STATE, (g + 1) * SSD_STATE)
            gs = slice(g * SSD_GW, (g + 1) * SSD_GW)
            s0 = s0_ref[i, gs, :]
            cmg = _pad_rows(cm[:, ns])
            cbv = jnp.sum(cm[:, ns].astype(F32) * bm[:, ns].astype(F32), axis=-1, keepdims=True)
            y_diag = cbv.astype(BF16).astype(F32) * dxb[:, gs].astype(F32)
            y_off = _dot_nt(cmg, s0.astype(BF16))[0:1] * ed_x[:, gs]
            ys.append(y_diag + y_off)
            e1, e2, e3 = _split3(ed_x[:, gs])
            ed_rows = jnp.concatenate([e1, e2, e3, jnp.zeros((SUBLANES - 3, SSD_GW), BF16)], axis=0)
            ed_full = _dot_tn(ed_rows, ones3)
            outer = _dot_tn(_pad_rows(dxb[:, gs]), _pad_rows(bm[:, ns]))
            s1_ref[i, gs, :] = ed_full * s0 + outer
        y = jnp.concatenate(ys, axis=-1) + d_ref[...] * xs
        y = y * _silu(z_ref[i])
        y = y * lax.rsqrt(jnp.mean(y * y, axis=-1, keepdims=True) + RMS_EPS) * nw_ref[...]
        y_ref[i] = y.astype(y_ref.dtype)


def _ssd_step(z, xbc, dtraw, conv_buf, s0, prm):
    n = z.shape[0]
    nseq = min(n, 4)
    cw, cb, dtb, a, d_x, nw = prm
    const = lambda shape: pl.BlockSpec(shape, lambda i: (0, 0))
    tok = lambda w: pl.BlockSpec((nseq, 1, w), lambda i: (i, 0, 0))
    y, s1 = pl.pallas_call(
        functools.partial(_ssd_step_kernel, nseq=nseq),
        grid=(n // nseq,),
        in_specs=[tok(SSD_INNER), tok(SSD_CONV_DIM), tok(LANES),
                  pl.BlockSpec((nseq, SSD_CONV - 1, SSD_CONV_DIM), lambda i: (i, 0, 0)),
                  pl.BlockSpec((nseq, SSD_INNER, SSD_STATE), lambda i: (i, 0, 0)),
                  const((SSD_CONV, SSD_CONV_DIM)), const((1, SSD_CONV_DIM)),
                  const((1, LANES)), const((1, LANES)),
                  const((1, SSD_INNER)), const((1, SSD_INNER))],
        out_specs=[tok(SSD_INNER),
                   pl.BlockSpec((nseq, SSD_INNER, SSD_STATE), lambda i: (i, 0, 0))],
        out_shape=[jax.ShapeDtypeStruct((n, 1, SSD_INNER), BF16),
                   jax.ShapeDtypeStruct((n, SSD_INNER, SSD_STATE), F32)],
        compiler_params=_cp("parallel"),
        name="ssd_step",
    )(z.reshape(n, 1, -1), xbc.reshape(n, 1, -1), dtraw.reshape(n, 1, -1), conv_buf, s0,
      cw, cb, dtb, a, d_x, nw)
    return y.reshape(n, SSD_INNER), s1


def _ssd_params(conv_w, conv_b, dt_bias, a_log, d_skip, norm_w):
    pad = lambda v: jnp.pad(v, (0, LANES - SSD_HEADS)).reshape(1, LANES)
    return (conv_w, conv_b.reshape(1, -1), pad(dt_bias), pad(-jnp.exp(a_log)),
            jnp.repeat(d_skip, SSD_HEAD_DIM).reshape(1, SSD_INNER), norm_w.reshape(1, SSD_INNER))


PEER_TE = 1024
PEER_ROWS = PEER_TE // PEER_KEYS


def _topk_desc(s, k):
    vals = []
    cur = s
    for _ in range(k):
        v = jnp.max(cur, axis=0, keepdims=True)
        vals.append(v)
        cur = jnp.where(cur == v, -jnp.inf, cur)
    return vals


def _peer_route(xb, wq_ref, k1_ref, k2_ref, s1_ref, s2_ref, tau_ref):
    tm = xb.shape[0]
    q = _dot(xb, wq_ref[...]).astype(BF16)
    half = PEER_DK // 2
    for h in range(PEER_HEADS):
        s1 = _dot_nt(k1_ref[h], q[:, h * PEER_DK:h * PEER_DK + half])
        s2 = _dot_nt(k2_ref[h], q[:, h * PEER_DK + half:(h + 1) * PEER_DK])
        a = _topk_desc(s1, PEER_TOPK)
        b = _topk_desc(s2, PEER_TOPK)
        s1 = s1 - a[0]
        s2 = s2 - b[0]
        a = [v - a[0] for v in a]
        b = [v - b[0] for v in b]
        b_all = jnp.concatenate(b, axis=0)
        cands = [a[0] + b_all]
        for i in range(1, PEER_TOPK // 2):
            cands.append(a[i] + b_all[0:PEER_TOPK // 2])
        cands.append(jnp.concatenate(a[PEER_TOPK // 2:], axis=0) + b[0])
        cand = jnp.concatenate(cands, axis=0)
        cur = cand
        zsum = jnp.zeros((1, tm), F32)
        v = None
        for _ in range(PEER_TOPK):
            v = jnp.max(cur, axis=0, keepdims=True)
            zsum = zsum + jnp.exp(v)
            cur = jnp.where(cur == v, -jnp.inf, cur)
        sel = cand >= v
        s1 = s1 - jnp.log(zsum)
        a_sh = [x - jnp.log(zsum) for x in a]
        cands2 = [a_sh[0] + b_all]
        for i in range(1, PEER_TOPK // 2):
            cands2.append(a_sh[i] + b_all[0:PEER_TOPK // 2])
        cands2.append(jnp.concatenate(a_sh[PEER_TOPK // 2:], axis=0) + b[0])
        cand2 = jnp.concatenate(cands2, axis=0)
        tau = jnp.min(jnp.where(sel, cand2, jnp.inf), axis=0, keepdims=True)
        s1_ref[h] = s1.reshape(PEER_KEYS // PEER_ROWS, PEER_ROWS, tm)
        s2_ref[h] = s2
        tau_ref[h] = jnp.broadcast_to(tau, (SUBLANES, tm))


def _peer_kernel(x_ref, wq_ref, k1_ref, k2_ref, u_ref, vt_ref, g_ref, b_ref, o_ref,
                 acc, s1_ref, s2_ref, tau_ref):
    j = pl.program_id(1)
    xb = x_ref[...].astype(BF16)

    @pl.when(j == 0)
    def _():
        acc[...] = jnp.zeros_like(acc)
        _peer_route(xb, wq_ref, k1_ref, k2_ref, s1_ref, s2_ref, tau_ref)

    h_t = _dot_nt(u_ref[...], xb)
    acts = []
    for r in range(PEER_ROWS):
        gate = None
        for h in range(PEER_HEADS):
            a = s1_ref[h, j, r:r + 1, :] + s2_ref[h]
            w = jnp.where(a >= tau_ref[h, 0:1, :], jnp.exp(a), 0.0)
            gate = w if gate is None else gate + w
        hr = h_t[r * PEER_KEYS:(r + 1) * PEER_KEYS, :]
        acts.append((jax.nn.gelu(hr) * gate).astype(BF16))
    act = jnp.concatenate(acts, axis=0)
    acc[...] += _dot(vt_ref[...], act)

    @pl.when(j == pl.num_programs(1) - 1)
    def _():
        o_ref[...] = _ln(DN_ALPHA * x_ref[...] + acc[...].T, g_ref[...], b_ref[...])


def _peer_ln(x, wq, k1, k2, u, vt, g, b):
    t = x.shape[0]
    tm = min(t, 512)
    const2 = lambda shape: pl.BlockSpec(shape, lambda i, j: (0, 0))
    const3 = lambda shape: pl.BlockSpec(shape, lambda i, j: (0, 0, 0))
    return pl.pallas_call(
        _peer_kernel,
        grid=(t // tm, PEER_EXPERTS // PEER_TE),
        in_specs=[pl.BlockSpec((tm, D_MODEL), lambda i, j: (i, 0)),
                  const2((D_MODEL, PEER_HEADS * PEER_DK)),
                  const3((PEER_HEADS, PEER_KEYS, PEER_DK // 2)),
                  const3((PEER_HEADS, PEER_KEYS, PEER_DK // 2)),
                  pl.BlockSpec((PEER_TE, D_MODEL), lambda i, j: (j, 0)),
                  pl.BlockSpec((D_MODEL, PEER_TE), lambda i, j: (0, j)),
                  const2((1, D_MODEL)), const2((1, D_MODEL))],
        out_specs=pl.BlockSpec((tm, D_MODEL), lambda i, j: (i, 0)),
        out_shape=jax.ShapeDtypeStruct((t, D_MODEL), F32),
        scratch_shapes=[pltpu.VMEM((D_MODEL, tm), F32),
                        pltpu.VMEM((PEER_HEADS, PEER_KEYS // PEER_ROWS, PEER_ROWS, tm), F32),
                        pltpu.VMEM((PEER_HEADS, PEER_KEYS, tm), F32),
                        pltpu.VMEM((PEER_HEADS, SUBLANES, tm), F32)],
        compiler_params=_cp("parallel", "arbitrary"),
        name="peer",
    )(x, wq, k1, k2, u, vt, g.reshape(1, -1), b.reshape(1, -1))


def _even_prompt(x, bsz, seq, w_in, s5p, sinks, w_out, g, b):
    proj = _mm(x, w_in, tn=640)
    p3 = proj.reshape(bsz, seq, -1)
    u_tm = p3[:, :, :S5_WIDTH].transpose(1, 0, 2).reshape(seq * bsz, S5_WIDTH)
    steps = min(seq, 64)
    y_a, h_t = _s5(u_tm, jnp.zeros((bsz, 2 * S5_FLAT), F32), s5p, bsz, steps)
    y_a = y_a.reshape(seq, bsz, S5_WIDTH).transpose(1, 0, 2)
    y_b = _swa_prompt(p3, sinks)
    mix = jnp.concatenate([y_a, y_b], axis=-1).reshape(bsz * seq, -1)
    x = _mm_ln(mix, w_out, x, g, b)
    kv = p3[:, seq - WINDOW:, S5_WIDTH + SWA_WIDTH:]
    new_k = kv[..., :SWA_KV_WIDTH].reshape(bsz, WINDOW, SWA_KV_HEADS, SWA_HEAD_DIM)
    new_v = kv[..., SWA_KV_WIDTH:].reshape(bsz, WINDOW, SWA_KV_HEADS, SWA_HEAD_DIM)
    h_re = h_t[:, :S5_FLAT].reshape(bsz, S5_GROUPS, S5_STATE)
    h_im = h_t[:, S5_FLAT:].reshape(bsz, S5_GROUPS, S5_STATE)
    return x, h_re, h_im, new_k, new_v


def _even_sample(x, h_re, h_im, k_buf, v_buf, start, w_in, s5p, sinks, w_out, g, b):
    n = x.shape[0]
    proj = _mm(x, w_in, tn=640)
    h0 = jnp.concatenate([h_re.reshape(n, S5_FLAT), h_im.reshape(n, S5_FLAT)], axis=1)
    y_a, h_t = _s5(proj[:, :S5_WIDTH], h0, s5p, n, 1)
    kc = k_buf.reshape(n, WINDOW, SWA_KV_WIDTH)
    vc = v_buf.reshape(n, WINDOW, SWA_KV_WIDTH)
    y_b = _swa_step(proj, kc, vc, sinks, start)
    x = _mm_ln(jnp.concatenate([y_a, y_b], axis=-1), w_out, x, g, b)
    kn = proj[:, None, S5_WIDTH + SWA_WIDTH:S5_WIDTH + SWA_WIDTH + SWA_KV_WIDTH]
    vn = proj[:, None, S5_WIDTH + SWA_WIDTH + SWA_KV_WIDTH:]
    new_k = jnp.concatenate([kc[:, 1:], kn], axis=1).reshape(k_buf.shape)
    new_v = jnp.concatenate([vc[:, 1:], vn], axis=1).reshape(v_buf.shape)
    return (x, h_t[:, :S5_FLAT].reshape(h_re.shape), h_t[:, S5_FLAT:].reshape(h_im.shape), new_k, new_v)


def _odd_weights(w_in):
    w_z = w_in[:, :SSD_INNER].astype(BF16)
    w_x = w_in[:, SSD_INNER:SSD_INNER + SSD_CONV_DIM].astype(BF16)
    w_dt = jnp.pad(w_in[:, SSD_INNER + SSD_CONV_DIM:], ((0, 0), (0, LANES - SSD_HEADS))).astype(BF16)
    return w_z, w_x, w_dt


def _odd_prompt(x, bsz, seq, w3, ssdp, w_out, g, b):
    w_z, w_x, w_dt = w3
    z = _mm(x, w_z).reshape(bsz, seq, -1)
    xbc = _mm(x, w_x).reshape(bsz, seq, -1)
    dtraw = _mm(x, w_dt).reshape(bsz, seq, -1)
    y, st = _ssd_prompt(z, xbc, dtraw, ssdp)
    x = _mm_ln(y.reshape(bsz * seq, -1), w_out, x, g, b)
    return (x, st.reshape(bsz, SSD_HEADS, SSD_HEAD_DIM, SSD_STATE), xbc[:, seq - (SSD_CONV - 1):])


def _odd_sample(x, ssm, conv, w3, ssdp, w_out, g, b):
    n = x.shape[0]
    w_z, w_x, w_dt = w3
    z = _mm(x, w_z)
    xbc = _mm(x, w_x)
    dtraw = _mm(x, w_dt)
    y, st = _ssd_step(z, xbc, dtraw, conv, ssm.reshape(n, SSD_INNER, SSD_STATE), ssdp)
    x = _mm_ln(y, w_out, x, g, b)
    new_conv = jnp.concatenate([conv[:, 1:], xbc[:, None, :]], axis=1)
    return x, st.reshape(ssm.shape), new_conv


def kernel(x_prompt, x_sample, state_s5_re, state_s5_im, cache_swa_k, cache_swa_v, state_ssd, state_conv,
           w_in_even, s5_lambda_re, s5_lambda_im, s5_log_dt, s5_b_re, s5_b_im, s5_c_re, s5_c_im, s5_d,
           s5_w_glu, swa_sinks, w_out_even,
           w_in_odd, ssd_conv_w, ssd_conv_b, ssd_dt_bias, ssd_a_log, ssd_d, ssd_norm_w, w_out_odd,
           ln1_g, ln1_b, ln2_g, ln2_b, peer_w_q, peer_k1, peer_k2, peer_u, peer_v):
    bsz, seq, _ = x_prompt.shape
    nsm = x_sample.shape[0]
    xp = x_prompt.reshape(bsz * seq, D_MODEL)
    xs = x_sample.reshape(nsm, D_MODEL)
    p_out = [[] for _ in range(6)]
    s_out = [[] for _ in range(6)]
    for layer in range(DEPTH):
        i = layer // 2
        if layer % 2 == 0:
            w_in = w_in_even[i].astype(BF16)
            w_out = w_out_even[i].astype(BF16)
            s5p = _s5_params(s5_lambda_re[i], s5_lambda_im[i], s5_log_dt[i], s5_b_re[i], s5_b_im[i],
                             s5_c_re[i], s5_c_im[i], s5_d[i], s5_w_glu[i])
            xp, hr, hi, nk, nv = _even_prompt(xp, bsz, seq, w_in, s5p, swa_sinks[i], w_out,
                                              ln1_g[layer], ln1_b[layer])
            for lst, val in zip(p_out[:4], (hr, hi, nk, nv)):
                lst.append(val)
            xs, hr, hi, nk, nv = _even_sample(xs, state_s5_re[i], state_s5_im[i], cache_swa_k[i], cache_swa_v[i],
                                              PAST_LEN, w_in, s5p, swa_sinks[i], w_out,
                                              ln1_g[layer], ln1_b[layer])
            for lst, val in zip(s_out[:4], (hr, hi, nk, nv)):
                lst.append(val)
        else:
            w3 = _odd_weights(w_in_odd[i])
            w_out = w_out_odd[i].astype(BF16)
            ssdp = _ssd_params(ssd_conv_w[i], ssd_conv_b[i], ssd_dt_bias[i], ssd_a_log[i], ssd_d[i],
                               ssd_norm_w[i])
            xp, st, cb = _odd_prompt(xp, bsz, seq, w3, ssdp, w_out, ln1_g[layer], ln1_b[layer])
            p_out[4].append(st)
            p_out[5].append(cb)
            xs, st, cb = _odd_sample(xs, state_ssd[i], state_conv[i], w3, ssdp, w_out,
                                     ln1_g[layer], ln1_b[layer])
            s_out[4].append(st)
            s_out[5].append(cb)
        wq = peer_w_q[layer].astype(BF16)
        k1 = peer_k1[layer].astype(BF16)
        k2 = peer_k2[layer].astype(BF16)
        u = peer_u[layer].astype(BF16)
        vt = peer_v[layer].T.astype(BF16)
        xp = _peer_ln(xp, wq, k1, k2, u, vt, ln2_g[layer], ln2_b[layer])
        xs = _peer_ln(xs, wq, k1, k2, u, vt, ln2_g[layer], ln2_b[layer])
    stack = lambda lists: tuple(jnp.stack(l) for l in lists)
    return ((xp.reshape(bsz, seq, D_MODEL), xs.reshape(nsm, 1, D_MODEL)) + stack(p_out) + stack(s_out))
```

```python
import functools
import math

import jax
import jax.numpy as jnp
from jax import lax
from jax.experimental import pallas as pl
from jax.experimental.pallas import tpu as pltpu

F32 = jnp.float32
BF16 = jnp.bfloat16

D_MODEL = 1024
DEPTH = 4
PAST_LEN = 8192
S5_WIDTH = 512
S5_GROUP = 16
S5_GROUPS = 32
S5_STATE = 64
S5_FLAT = S5_GROUPS * S5_STATE
SWA_HEADS = 8
SWA_KV_HEADS = 2
SWA_REP = SWA_HEADS // SWA_KV_HEADS
SWA_HEAD_DIM = 64
SWA_WIDTH = SWA_HEADS * SWA_HEAD_DIM
SWA_KV_WIDTH = SWA_KV_HEADS * SWA_HEAD_DIM
WINDOW = 128
SSD_INNER = 2048
SSD_HEAD_DIM = 64
SSD_HEADS = 32
SSD_GROUPS = 4
SSD_HPG = SSD_HEADS // SSD_GROUPS
SSD_GW = SSD_HPG * SSD_HEAD_DIM
SSD_STATE = 128
SSD_CONV = 4
SSD_CHUNK = 128
SSD_CONV_DIM = SSD_INNER + 2 * SSD_GROUPS * SSD_STATE
PEER_HEADS = 8
PEER_KEYS = 128
PEER_EXPERTS = PEER_KEYS * PEER_KEYS
PEER_DK = 256
PEER_TOPK = 16
DN_ALPHA = (2 * DEPTH) ** 0.25
LN_EPS = 1e-5
RMS_EPS = 1e-5
NEG_INF = -1e30

LANES = 128
SUBLANES = 8
VMEM_LIMIT = 56 * 1024 * 1024


def _cp(*sem):
    return pltpu.CompilerParams(dimension_semantics=sem, vmem_limit_bytes=VMEM_LIMIT)


def _ln(x, g, b):
    mu = jnp.mean(x, axis=-1, keepdims=True)
    xc = x - mu
    var = jnp.mean(xc * xc, axis=-1, keepdims=True)
    return xc * lax.rsqrt(var + LN_EPS) * g + b


def _softplus(x):
    return jnp.maximum(x, 0.0) + jnp.log(1.0 + jnp.exp(-jnp.abs(x)))


def _silu(x):
    return x * jax.nn.sigmoid(x)


def _dot(a, b):
    return jnp.dot(a, b, preferred_element_type=F32)


def _dot_nt(a, b):
    return lax.dot_general(a, b, (((1,), (1,)), ((), ())), preferred_element_type=F32)


def _dot_tn(a, b):
    return lax.dot_general(a, b, (((0,), (0,)), ((), ())), preferred_element_type=F32)


def _split3(x):
    x1 = x.astype(BF16)
    r1 = x - x1.astype(F32)
    x2 = r1.astype(BF16)
    r2 = r1 - x2.astype(F32)
    return x1, x2, r2.astype(BF16)


def _dot_exact01(x, m01):
    x1, x2, x3 = _split3(x)
    return _dot(x1, m01) + _dot(x2, m01) + _dot(x3, m01)


def _mm_kernel(x_ref, w_ref, o_ref):
    o_ref[...] = _dot(x_ref[...].astype(BF16), w_ref[...]).astype(o_ref.dtype)


def _mm(x, w, tn=512, out_dtype=F32):
    m, k = x.shape
    n = w.shape[1]
    tm = min(m, 512)
    tn = min(n, tn)
    return pl.pallas_call(
        _mm_kernel,
        grid=(m // tm, n // tn),
        in_specs=[pl.BlockSpec((tm, k), lambda i, j: (i, 0)),
                  pl.BlockSpec((k, tn), lambda i, j: (0, j))],
        out_specs=pl.BlockSpec((tm, tn), lambda i, j: (i, j)),
        out_shape=jax.ShapeDtypeStruct((m, n), out_dtype),
        compiler_params=_cp("parallel", "arbitrary"),
        name="mm",
    )(x, w)


def _mm_ln_kernel(x_ref, w_ref, r_ref, g_ref, b_ref, o_ref):
    y = _dot(x_ref[...].astype(BF16), w_ref[...])
    o_ref[...] = _ln(DN_ALPHA * r_ref[...] + y, g_ref[...], b_ref[...])


def _mm_ln(x, w, res, g, b):
    m, k = x.shape
    n = w.shape[1]
    tm = min(m, 512)
    return pl.pallas_call(
        _mm_ln_kernel,
        grid=(m // tm,),
        in_specs=[pl.BlockSpec((tm, k), lambda i: (i, 0)),
                  pl.BlockSpec((k, n), lambda i: (0, 0)),
                  pl.BlockSpec((tm, n), lambda i: (i, 0)),
                  pl.BlockSpec((1, n), lambda i: (0, 0)),
                  pl.BlockSpec((1, n), lambda i: (0, 0))],
        out_specs=pl.BlockSpec((tm, n), lambda i: (i, 0)),
        out_shape=jax.ShapeDtypeStruct((m, n), F32),
        compiler_params=_cp("parallel"),
        name="mm_ln",
    )(x, w, res, g.reshape(1, n), b.reshape(1, n))


def _s5_kernel(u_ref, h0_ref, bd_ref, ar_ref, ai_ref, cbd_ref, d_ref, wg_ref,
               y_ref, hT_ref, hbuf, state, *, rows, steps):
    c = pl.program_id(0)

    @pl.when(c == 0)
    def _():
        state[...] = h0_ref[...]

    u = u_ref[...]
    hbuf[...] = _dot(u.astype(BF16), bd_ref[...])
    ar = ar_ref[...]
    ai = ai_ref[...]

    def step(t, carry):
        hr, hi = carry
        r0 = pl.multiple_of(t * rows, rows)
        br = hbuf[pl.ds(r0, rows), 0:S5_FLAT]
        bi = hbuf[pl.ds(r0, rows), S5_FLAT:2 * S5_FLAT]
        nr = ar * hr - ai * hi + br
        ni = ar * hi + ai * hr + bi
        hbuf[pl.ds(r0, rows), 0:S5_FLAT] = nr
        hbuf[pl.ds(r0, rows), S5_FLAT:2 * S5_FLAT] = ni
        return nr, ni

    hr, hi = lax.fori_loop(0, steps, step, (state[:, 0:S5_FLAT], state[:, S5_FLAT:2 * S5_FLAT]))
    state[:, 0:S5_FLAT] = hr
    state[:, S5_FLAT:2 * S5_FLAT] = hi

    y = _dot(hbuf[...].astype(BF16), cbd_ref[...]) + d_ref[...] * u
    g = jax.nn.gelu(y)
    y_ref[...] = g * jax.nn.sigmoid(_dot(g.astype(BF16), wg_ref[...]))

    @pl.when(c == pl.num_programs(0) - 1)
    def _():
        hT_ref[...] = state[...]


def _s5(u_tm, h0, prm, rows, steps):
    n = u_tm.shape[0]
    blk = rows * steps
    bd, ar, ai, cbd, d, wg = prm
    const = lambda shape: pl.BlockSpec(shape, lambda c: (0, 0))
    return pl.pallas_call(
        functools.partial(_s5_kernel, rows=rows, steps=steps),
        grid=(n // blk,),
        in_specs=[pl.BlockSpec((blk, S5_WIDTH), lambda c: (c, 0)),
                  const((rows, 2 * S5_FLAT)),
                  const((S5_WIDTH, 2 * S5_FLAT)),
                  const((1, S5_FLAT)), const((1, S5_FLAT)),
                  const((2 * S5_FLAT, S5_WIDTH)),
                  const((1, S5_WIDTH)),
                  const((S5_WIDTH, S5_WIDTH))],
        out_specs=[pl.BlockSpec((blk, S5_WIDTH), lambda c: (c, 0)),
                   const((rows, 2 * S5_FLAT))],
        out_shape=[jax.ShapeDtypeStruct((n, S5_WIDTH), F32),
                   jax.ShapeDtypeStruct((rows, 2 * S5_FLAT), F32)],
        scratch_shapes=[pltpu.VMEM((blk, 2 * S5_FLAT), F32),
                        pltpu.VMEM((rows, 2 * S5_FLAT), F32)],
        compiler_params=_cp("arbitrary"),
        name="s5",
    )(u_tm, h0, bd, ar, ai, cbd, d, wg)


def _s5_params(lam_re, lam_im, log_dt, b_re, b_im, c_re, c_im, d_skip, w_glu):
    dt = jnp.exp(log_dt)[:, None]
    mag = jnp.exp(lam_re * dt)
    ab_re, ab_im = mag * jnp.cos(lam_im * dt), mag * jnp.sin(lam_im * dt)
    den = lam_re * lam_re + lam_im * lam_im
    nr, ni = ab_re - 1.0, ab_im
    coef_re = ((nr * lam_re + ni * lam_im) / den)[..., None]
    coef_im = ((ni * lam_re - nr * lam_im) / den)[..., None]
    bb_re = coef_re * b_re - coef_im * b_im
    bb_im = coef_re * b_im + coef_im * b_re
    eye = jnp.eye(S5_GROUPS, dtype=F32)
    pack_in = lambda bb: jnp.einsum('gpc,gh->gchp', bb, eye).reshape(S5_WIDTH, S5_FLAT)
    bd = jnp.concatenate([pack_in(bb_re), pack_in(bb_im)], axis=1).astype(BF16)
    pack_out = lambda cc: jnp.einsum('gcp,gh->gphc', cc, eye).reshape(S5_FLAT, S5_WIDTH)
    cbd = jnp.concatenate([pack_out(c_re), -pack_out(c_im)], axis=0).astype(BF16)
    return (bd, ab_re.reshape(1, S5_FLAT), ab_im.reshape(1, S5_FLAT), cbd,
            d_skip.reshape(1, S5_WIDTH), w_glu.astype(BF16))


def _swa_kernel(sink_ref, q_ref, kp_ref, kc_ref, vp_ref, vc_ref, o_ref):
    n = pl.program_id(1)
    q = q_ref[0].astype(BF16)
    k = jnp.concatenate([kp_ref[0], kc_ref[0]], axis=0).astype(BF16)
    v = jnp.concatenate([vp_ref[0], vc_ref[0]], axis=0).astype(BF16)
    qi = lax.broadcasted_iota(jnp.int32, (WINDOW, 2 * WINDOW), 0)
    kj = lax.broadcasted_iota(jnp.int32, (WINDOW, 2 * WINDOW), 1)
    dist = WINDOW + qi - kj
    valid = (dist >= 0) & (dist < WINDOW) & ((n > 0) | (kj >= WINDOW))
    distf = dist.astype(F32)
    outs = []
    for h in range(SWA_HEADS):
        g = h // SWA_REP
        qh = q[:, h * SWA_HEAD_DIM:(h + 1) * SWA_HEAD_DIM]
        kg = k[:, g * SWA_HEAD_DIM:(g + 1) * SWA_HEAD_DIM]
        vg = v[:, g * SWA_HEAD_DIM:(g + 1) * SWA_HEAD_DIM]
        s = _dot_nt(qh, kg) * (SWA_HEAD_DIM ** -0.5)
        s = s - (2.0 ** -(h + 1)) * distf
        s = jnp.where(valid, s, NEG_INF)
        sink = sink_ref[h]
        m = jnp.maximum(jnp.max(s, axis=-1, keepdims=True), sink)
        p = jnp.exp(s - m)
        p = p / (jnp.sum(p, axis=-1, keepdims=True) + jnp.exp(sink - m))
        outs.append(_dot(p.astype(BF16), vg))
    o_ref[0] = jnp.concatenate(outs, axis=-1)


def _swa_prompt(proj, sinks):
    bsz, seq, _ = proj.shape
    kcol = (S5_WIDTH + SWA_WIDTH) // SWA_KV_WIDTH
    prev = lambda b, n: (b, jnp.maximum(n - 1, 0), kcol)
    prev_v = lambda b, n: (b, jnp.maximum(n - 1, 0), kcol + 1)
    blk = (1, WINDOW, SWA_KV_WIDTH)
    return pl.pallas_call(
        _swa_kernel,
        grid=(bsz, seq // WINDOW),
        in_specs=[pl.BlockSpec(memory_space=pltpu.SMEM),
                  pl.BlockSpec((1, WINDOW, SWA_WIDTH), lambda b, n: (b, n, 1)),
                  pl.BlockSpec(blk, prev),
                  pl.BlockSpec(blk, lambda b, n: (b, n, kcol)),
                  pl.BlockSpec(blk, prev_v),
                  pl.BlockSpec(blk, lambda b, n: (b, n, kcol + 1))],
        out_specs=pl.BlockSpec((1, WINDOW, SWA_WIDTH), lambda b, n: (b, n, 0)),
        out_shape=jax.ShapeDtypeStruct((bsz, seq, SWA_WIDTH), F32),
        compiler_params=_cp("parallel", "arbitrary"),
        name="swa_prompt",
    )(sinks, proj, proj, proj, proj, proj)


def _per_head_column(values):
    r = lax.broadcasted_iota(jnp.int32, (SWA_REP, 1), 0)
    col = jnp.full((SWA_REP, 1), values[-1], F32)
    for i in range(SWA_REP - 2, -1, -1):
        col = jnp.where(r == i, values[i], col)
    return col


def _swa_step_kernel(sink_ref, p_ref, kc_ref, vc_ref, o_ref, *, nseq, start):
    kj = lax.broadcasted_iota(jnp.int32, (SWA_REP, WINDOW), 1)
    dist = WINDOW - kj
    valid = (dist < WINDOW) & (start - WINDOW + kj >= 0)
    distf = dist.astype(F32)
    for i in range(nseq):
        row = p_ref[i]
        kn = row[:, S5_WIDTH + SWA_WIDTH:S5_WIDTH + SWA_WIDTH + SWA_KV_WIDTH].astype(BF16)
        vn = row[:, S5_WIDTH + SWA_WIDTH + SWA_KV_WIDTH:].astype(BF16)
        kc = kc_ref[i].astype(BF16)
        vc = vc_ref[i].astype(BF16)
        outs = []
        for g in range(SWA_KV_HEADS):
            q4 = jnp.concatenate(
                [row[:, S5_WIDTH + (g * SWA_REP + r) * SWA_HEAD_DIM:S5_WIDTH + (g * SWA_REP + r + 1) * SWA_HEAD_DIM]
                 for r in range(SWA_REP)], axis=0).astype(BF16)
            sl = slice(g * SWA_HEAD_DIM, (g + 1) * SWA_HEAD_DIM)
            slope = _per_head_column([2.0 ** -(g * SWA_REP + r + 1) for r in range(SWA_REP)])
            sink = _per_head_column([sink_ref[g * SWA_REP + r] for r in range(SWA_REP)])
            scale = SWA_HEAD_DIM ** -0.5
            sc = _dot_nt(q4, kc[:, sl]) * scale - slope * distf
            sc = jnp.where(valid, sc, NEG_INF)
            sn = jnp.sum(q4.astype(F32) * kn[:, sl].astype(F32), axis=-1, keepdims=True) * scale
            m = jnp.maximum(jnp.maximum(jnp.max(sc, axis=-1, keepdims=True), sn), sink)
            pc = jnp.exp(sc - m)
            pn = jnp.exp(sn - m)
            den = jnp.sum(pc, axis=-1, keepdims=True) + pn + jnp.exp(sink - m)
            o4 = _dot((pc / den).astype(BF16), vc[:, sl])
            o4 = o4 + (pn / den).astype(BF16).astype(F32) * vn[:, sl].astype(F32)
            outs.extend([o4[r:r + 1, :] for r in range(SWA_REP)])
        o_ref[i] = jnp.concatenate(outs, axis=-1)


def _swa_step(proj, k_cache, v_cache, sinks, start):
    n = proj.shape[0]
    nseq = min(n, SUBLANES)
    out = pl.pallas_call(
        functools.partial(_swa_step_kernel, nseq=nseq, start=start),
        grid=(n // nseq,),
        in_specs=[pl.BlockSpec(memory_space=pltpu.SMEM),
                  pl.BlockSpec((nseq, 1, proj.shape[1]), lambda i: (i, 0, 0)),
                  pl.BlockSpec((nseq, WINDOW, SWA_KV_WIDTH), lambda i: (i, 0, 0)),
                  pl.BlockSpec((nseq, WINDOW, SWA_KV_WIDTH), lambda i: (i, 0, 0))],
        out_specs=pl.BlockSpec((nseq, 1, SWA_WIDTH), lambda i: (i, 0, 0)),
        out_shape=jax.ShapeDtypeStruct((n, 1, SWA_WIDTH), F32),
        compiler_params=_cp("parallel"),
        name="swa_step",
    )(sinks, proj.reshape(n, 1, -1), k_cache, v_cache)
    return out.reshape(n, SWA_WIDTH)


def _head_expand(width):
    r = lax.broadcasted_iota(jnp.int32, (LANES, SSD_HEADS * width), 0)
    c = lax.broadcasted_iota(jnp.int32, (LANES, SSD_HEADS * width), 1)
    return jnp.where((c >= r * width) & (c < (r + 1) * width), 1.0, 0.0).astype(BF16)


def _ssd_kernel(z_ref, xbc_ref, dt_ref, cw_ref, cb_ref, dtb_ref, a_ref, d_ref, nw_ref,
                y_ref, st_ref, tail, state, ybuf):
    c = pl.program_id(1)
    q = SSD_CHUNK

    @pl.when(c == 0)
    def _():
        tail[...] = jnp.zeros_like(tail)
        state[...] = jnp.zeros_like(state)

    cur = xbc_ref[0]
    tl = tail[...]
    row8 = lax.broadcasted_iota(jnp.int32, (SUBLANES, SSD_CONV_DIM), 0)
    acc = None
    for tap in range(SSD_CONV):
        sh = SSD_CONV - 1 - tap
        if sh == 0:
            term = cur
        else:
            rolled = pltpu.roll(cur, sh, 0)
            head = jnp.where(row8 < sh, pltpu.roll(tl, sh, 0), rolled[0:SUBLANES])
            term = jnp.concatenate([head, rolled[SUBLANES:]], axis=0)
        term = term * cw_ref[tap:tap + 1, :]
        acc = term if acc is None else acc + term
    tail[...] = cur[q - SUBLANES:q]
    xbc = _silu(acc + cb_ref[...])
    xs = xbc[:, 0:SSD_INNER]
    bm = xbc[:, SSD_INNER:SSD_INNER + SSD_GROUPS * SSD_STATE].astype(BF16)
    cm = xbc[:, SSD_INNER + SSD_GROUPS * SSD_STATE:].astype(BF16)

    dt = _softplus(dt_ref[0] + dtb_ref[...])
    da = dt * a_ref[...]
    ri = lax.broadcasted_iota(jnp.int32, (q, q), 0)
    ci = lax.broadcasted_iota(jnp.int32, (q, q), 1)
    causal = ri >= ci
    tri = jnp.where(causal, 1.0, 0.0).astype(BF16)
    da1, da2, da3 = _split3(da)
    acs = _dot(tri, da1) + _dot(tri, da2) + _dot(tri, da3)
    acs_t = acs.T
    e64 = _head_expand(SSD_HEAD_DIM)
    dt_x = _dot_exact01(dt, e64)
    acs_x = _dot_exact01(acs, e64)
    last_x = acs_x[q - 1:q, :]
    dx = dt_x * xs
    dsx = (jnp.exp(last_x - acs_x) * dx).astype(BF16)
    eacs_x = jnp.exp(acs_x)
    dxb = dx.astype(BF16)

    for g in range(SSD_GROUPS):
        ns = slice(g * SSD_STATE, (g + 1) * SSD_STATE)
        gs = slice(g * SSD_GW, (g + 1) * SSD_GW)
        cb = _dot_nt(cm[:, ns], bm[:, ns])
        for r in range(SSD_HPG):
            h = g * SSD_HPG + r
            seg = acs[:, h:h + 1] - acs_t[h:h + 1, :]
            decay = jnp.where(causal, jnp.exp(jnp.where(causal, seg, 0.0)), 0.0)
            hs = slice(h * SSD_HEAD_DIM, (h + 1) * SSD_HEAD_DIM)
            ybuf[:, hs] = _dot((cb * decay).astype(BF16), dxb[:, hs])
        s_in = state[g]
        y_off = _dot(cm[:, ns], s_in.astype(BF16)) * eacs_x[:, gs]
        ybuf[:, gs] = ybuf[:, gs] + y_off
        state[g] = jnp.exp(last_x[:, gs]) * s_in + _dot_tn(bm[:, ns], dsx[:, gs])

    y = ybuf[...] + d_ref[...] * xs
    y = y * _silu(z_ref[0])
    y = y * lax.rsqrt(jnp.mean(y * y, axis=-1, keepdims=True) + RMS_EPS) * nw_ref[...]
    y_ref[0] = y.astype(y_ref.dtype)

    @pl.when(c == pl.num_programs(1) - 1)
    def _():
        for g in range(SSD_GROUPS):
            st_ref[0, g * SSD_GW:(g + 1) * SSD_GW, :] = state[g].T


def _ssd_prompt(z, xbc, dtraw, prm):
    bsz, seq, _ = z.shape
    cw, cb, dtb, a, d_x, nw = prm
    const = lambda shape: pl.BlockSpec(shape, lambda b, c: (0, 0))
    seqblk = lambda w: pl.BlockSpec((1, SSD_CHUNK, w), lambda b, c: (b, c, 0))
    return pl.pallas_call(
        _ssd_kernel,
        grid=(bsz, seq // SSD_CHUNK),
        in_specs=[seqblk(SSD_INNER), seqblk(SSD_CONV_DIM), seqblk(LANES),
                  const((SSD_CONV, SSD_CONV_DIM)), const((1, SSD_CONV_DIM)),
                  const((1, LANES)), const((1, LANES)),
                  const((1, SSD_INNER)), const((1, SSD_INNER))],
        out_specs=[seqblk(SSD_INNER),
                   pl.BlockSpec((1, SSD_INNER, SSD_STATE), lambda b, c: (b, 0, 0))],
        out_shape=[jax.ShapeDtypeStruct((bsz, seq, SSD_INNER), BF16),
                   jax.ShapeDtypeStruct((bsz, SSD_INNER, SSD_STATE), F32)],
        scratch_shapes=[pltpu.VMEM((SUBLANES, SSD_CONV_DIM), F32),
                        pltpu.VMEM((SSD_GROUPS, SSD_STATE, SSD_GW), F32),
                        pltpu.VMEM((SSD_CHUNK, SSD_INNER), F32)],
        compiler_params=_cp("parallel", "arbitrary"),
        name="ssd_prompt",
    )(z, xbc, dtraw, cw, cb, dtb, a, d_x, nw)


def _pad_rows(row):
    return jnp.concatenate([row, jnp.zeros((SUBLANES - 1, row.shape[1]), row.dtype)], axis=0)


def _ssd_step_kernel(z_ref, x_ref, dt_ref, buf_ref, s0_ref, cw_ref, cb_ref, dtb_ref, a_ref, d_ref, nw_ref,
                     y_ref, s1_ref, *, nseq):
    e64 = _head_expand(SSD_HEAD_DIM)
    ones3 = jnp.where(lax.broadcasted_iota(jnp.int32, (SUBLANES, SSD_STATE), 0) < 3, 1.0, 0.0).astype(BF16)
    for i in range(nseq):
        buf = buf_ref[i]
        acc = buf[0:1] * cw_ref[0:1, :]
        acc = acc + buf[1:2] * cw_ref[1:2, :]
        acc = acc + buf[2:3] * cw_ref[2:3, :]
        acc = acc + x_ref[i] * cw_ref[3:4, :]
        xbc = _silu(acc + cb_ref[...])
        xs = xbc[:, 0:SSD_INNER]
        bm = xbc[:, SSD_INNER:SSD_INNER + SSD_GROUPS * SSD_STATE].astype(BF16)
        cm = xbc[:, SSD_INNER + SSD_GROUPS * SSD_STATE:].astype(BF16)
        dt = _softplus(dt_ref[i] + dtb_ref[...])
        ed = jnp.exp(dt * a_ref[...])
        dt_x = _dot_exact01(_pad_rows(dt), e64)[0:1]
        ed_x = _dot_exact01(_pad_rows(ed), e64)[0:1]
        dx = dt_x * xs
        dxb = dx.astype(BF16)
        ys = []
        for g in range(SSD_GROUPS):
            ns = slice(g * SSD_STATE, (g + 1) * SSD_STATE)
            gs = slice(g * SSD_GW, (g + 1) * SSD_GW)
            s0 = s0_ref[i, gs, :]
            cmg = _pad_rows(cm[:, ns])
            cbv = jnp.sum(cm[:, ns].astype(F32) * bm[:, ns].astype(F32), axis=-1, keepdims=True)
            y_diag = cbv.astype(BF16).astype(F32) * dxb[:, gs].astype(F32)
            y_off = _dot_nt(cmg, s0.astype(BF16))[0:1] * ed_x[:, gs]
            ys.append(y_diag + y_off)
            e1, e2, e3 = _split3(ed_x[:, gs])
            ed_rows = jnp.concatenate([e1, e2, e3, jnp.zeros((SUBLANES - 3, SSD_GW), BF16)], axis=0)
            ed_full = _dot_tn(ed_rows, ones3)
            outer = _dot_tn(_pad_rows(dxb[:, gs]), _pad_rows(bm[:, ns]))
            s1_ref[i, gs, :] = ed_full * s0 + outer
        y = jnp.concatenate(ys, axis=-1) + d_ref[...] * xs
        y = y * _silu(z_ref[i])
        y = y * lax.rsqrt(jnp.mean(y * y, axis=-1, keepdims=True) + RMS_EPS) * nw_ref[...]
        y_ref[i] = y.astype(y_ref.dtype)


def _ssd_step(z, xbc, dtraw, conv_buf, s0, prm):
    n = z.shape[0]
    nseq = min(n, 4)
    cw, cb, dtb, a, d_x, nw = prm
    const = lambda shape: pl.BlockSpec(shape, lambda i: (0, 0))
    tok = lambda w: pl.BlockSpec((nseq, 1, w), lambda i: (i, 0, 0))
    y, s1 = pl.pallas_call(
        functools.partial(_ssd_step_kernel, nseq=nseq),
        grid=(n // nseq,),
        in_specs=[tok(SSD_INNER), tok(SSD_CONV_DIM), tok(LANES),
                  pl.BlockSpec((nseq, SSD_CONV - 1, SSD_CONV_DIM), lambda i: (i, 0, 0)),
                  pl.BlockSpec((nseq, SSD_INNER, SSD_STATE), lambda i: (i, 0, 0)),
                  const((SSD_CONV, SSD_CONV_DIM)), const((1, SSD_CONV_DIM)),
                  const((1, LANES)), const((1, LANES)),
                  const((1, SSD_INNER)), const((1, SSD_INNER))],
        out_specs=[tok(SSD_INNER),
                   pl.BlockSpec((nseq, SSD_INNER, SSD_STATE), lambda i: (i, 0, 0))],
        out_shape=[jax.ShapeDtypeStruct((n, 1, SSD_INNER), BF16),
                   jax.ShapeDtypeStruct((n, SSD_INNER, SSD_STATE), F32)],
        compiler_params=_cp("parallel"),
        name="ssd_step",
    )(z.reshape(n, 1, -1), xbc.reshape(n, 1, -1), dtraw.reshape(n, 1, -1), conv_buf, s0,
      cw, cb, dtb, a, d_x, nw)
    return y.reshape(n, SSD_INNER), s1


def _ssd_params(conv_w, conv_b, dt_bias, a_log, d_skip, norm_w):
    pad = lambda v: jnp.pad(v, (0, LANES - SSD_HEADS)).reshape(1, LANES)
    return (conv_w, conv_b.reshape(1, -1), pad(dt_bias), pad(-jnp.exp(a_log)),
            jnp.repeat(d_skip, SSD_HEAD_DIM).reshape(1, SSD_INNER), norm_w.reshape(1, SSD_INNER))


PEER_TE = 1024
PEER_ROWS = PEER_TE // PEER_KEYS


PEER_CHUNK_ROWS = 2
PEER_TOK_CHUNK = 256
PACK = 2 * SUBLANES


def _batcher_sort_pairs(n):
    def merge(lo, hi, r):
        step = r * 2
        if step < hi - lo:
            yield from merge(lo, hi, step)
            yield from merge(lo + r, hi, step)
            for i in range(lo + r, hi - r, step):
                yield (i, i + r)
        else:
            yield (lo, lo + r)

    def sort(lo, hi):
        if hi - lo >= 1:
            mid = lo + (hi - lo) // 2
            yield from sort(lo, mid)
            yield from sort(mid + 1, hi)
            yield from merge(lo, hi, 1)

    return tuple(sort(0, n - 1))


_SORT16 = _batcher_sort_pairs(PEER_TOPK)
_BITONIC16 = tuple((i, i + d) for d in (8, 4, 2, 1) for i in range(PEER_TOPK) if not i & d)


def _mx(a, b):
    return b if a is None else a if b is None else jnp.maximum(a, b)


def _mn(a, b):
    return None if a is None or b is None else jnp.minimum(a, b)


def _top16(rows):
    v = list(rows) + [None] * (PEER_TOPK - len(rows))
    for i, j in _SORT16:
        v[i], v[j] = _mx(v[i], v[j]), _mn(v[i], v[j])
    for shift in (4, 2, 1):
        w = [None if x is None else pltpu.roll(x, shift, 0) for x in v]
        v = [_mx(v[k], w[PEER_TOPK - 1 - k]) for k in range(PEER_TOPK)]
        for i, j in _BITONIC16:
            v[i], v[j] = _mx(v[i], v[j]), _mn(v[i], v[j])
    return v


def _prefix_count(test, thr):
    m8 = test(thr[7])
    m4 = test(jnp.where(m8, thr[11], thr[3]))
    m2 = test(jnp.where(m8, jnp.where(m4, thr[13], thr[9]), jnp.where(m4, thr[5], thr[1])))
    hi = jnp.where(m4, jnp.where(m2, thr[14], thr[12]), jnp.where(m2, thr[10], thr[8]))
    lo = jnp.where(m4, jnp.where(m2, thr[6], thr[4]), jnp.where(m2, thr[2], thr[0]))
    m1 = test(jnp.where(m8, hi, lo))
    return (jnp.where(m8, 8.0, 0.0) + jnp.where(m4, 4.0, 0.0)) + (jnp.where(m2, 2.0, 0.0) + jnp.where(m1, 1.0, 0.0))


def _peer_route(xb, wq_ref, k1_ref, k2_ref, c_ref, e1_ref, rk_ref, e2_ref):
    tm = xb.shape[0]
    q = _dot(xb, wq_ref[...]).astype(BF16)
    half = PEER_DK // 2
    nblk = PEER_KEYS // SUBLANES
    sub = lax.broadcasted_iota(jnp.int32, (SUBLANES, tm), 0)

    def by_sublane(vals):
        out = vals[SUBLANES - 1]
        for s in range(SUBLANES - 2, -1, -1):
            out = jnp.where(sub == s, vals[s], out)
        return out

    for h in range(PEER_HEADS):
        s1 = _dot_nt(k1_ref[h], q[:, h * PEER_DK:h * PEER_DK + half])
        s2 = _dot_nt(k2_ref[h], q[:, h * PEER_DK + half:(h + 1) * PEER_DK])
        r1 = [s1[k * SUBLANES:(k + 1) * SUBLANES] for k in range(nblk)]
        r2 = [s2[k * SUBLANES:(k + 1) * SUBLANES] for k in range(nblk)]
        a = _top16(r1)
        b = _top16(r2)
        r1 = [x - a[0] for x in r1]
        r2 = [x - b[0] for x in r2]
        a = [x - a[0] for x in a]
        b = [x - b[0] for x in b]
        b_lo, b_hi, a_hi = by_sublane(b[0:8]), by_sublane(b[8:16]), by_sublane(a[8:16])
        cand = [a[0] + b_lo, a[0] + b_hi] + [a[i] + b_lo for i in range(1, 8)] + [a_hi + b[0]]
        top = _top16(cand)
        tau = top[PEER_TOPK - 1]
        zsum = jnp.exp(top[0])
        for k in range(1, PEER_TOPK):
            zsum = zsum + jnp.exp(top[k])
        inv_z = 1.0 / zsum
        for k in range(nblk):
            x1 = r1[k]
            cnt = _prefix_count(lambda t: (x1 + t) >= tau, b[0:15])
            cnt = cnt + jnp.where((x1 + b[15]) >= tau, 1.0, 0.0)
            c_ref[h, k] = cnt
            e1_ref[h, k] = jnp.exp(x1) * inv_z
        ranks, e2s = [], []
        for k in range(nblk):
            x2 = r2[k]
            rnk = _prefix_count(lambda t: t > x2, b[0:15])
            ranks.append(jnp.where(x2 >= b[15], rnk, float(PEER_TOPK)))
            e2s.append(jnp.exp(x2))
        rk_ref[h] = jnp.concatenate(ranks, axis=0).astype(BF16)
        e2_ref[h] = jnp.concatenate(e2s, axis=0).astype(BF16)


def _gelu_tanh(x):
    c0 = math.sqrt(2.0 / math.pi)
    return (0.5 * x) * (1.0 + jnp.tanh(x * (c0 + (c0 * 0.044715) * (x * x))))


def _peer_kernel(x_ref, wq_ref, k1_ref, k2_ref, u_ref, vt_ref, g_ref, b_ref, o_ref,
                 acc, c_ref, e1_ref, rk_ref, e2_ref, cb_ref, eb_ref, act_ref):
    j = pl.program_id(1)
    tm = x_ref.shape[0]
    xb = x_ref[...].astype(BF16)

    @pl.when(j == 0)
    def _():
        acc[...] = jnp.zeros_like(acc)
        _peer_route(xb, wq_ref, k1_ref, k2_ref, c_ref, e1_ref, rk_ref, e2_ref)

        act_ref[1] = jnp.zeros((PEER_TE, tm), BF16)

    jt = jnp.minimum(j, PEER_EXPERTS // PEER_TE - 1)
    slot = j % 2

    for h in range(PEER_HEADS):
        for r in range(PEER_ROWS):
            cb_ref[h, r] = jnp.broadcast_to(c_ref[h, jt, r:r + 1, :], (PACK, tm)).astype(BF16)
            eb_ref[h, r] = jnp.broadcast_to(e1_ref[h, jt, r:r + 1, :], (PACK, tm)).astype(BF16)

    nch = PEER_ROWS // PEER_CHUNK_ROWS
    tc = min(tm, PEER_TOK_CHUNK)
    apply_after = {nch // 2 - 1 + k: k * tc for k in range(tm // tc)}
    for ch in range(nch):
        e0 = ch * PEER_CHUNK_ROWS * PEER_KEYS
        h_t = _dot_nt(u_ref[e0:e0 + PEER_CHUNK_ROWS * PEER_KEYS, :], xb)
        if ch in apply_after:
            t0 = apply_after[ch]
            acc[:, t0:t0 + tc] += _dot(vt_ref[...], act_ref[1 - slot, :, t0:t0 + tc])
        for tl in range(tm // LANES):
            ts = slice(tl * LANES, (tl + 1) * LANES)
            for b2 in range(PEER_KEYS // PACK):
                ks = slice(b2 * PACK, (b2 + 1) * PACK)
                gate = [None] * PEER_CHUNK_ROWS
                for h in range(PEER_HEADS):
                    rk = rk_ref[h, ks, ts]
                    e2 = e2_ref[h, ks, ts]
                    for ri in range(PEER_CHUNK_ROWS):
                        r = ch * PEER_CHUNK_ROWS + ri
                        w = jnp.where(rk < cb_ref[h, r, :, ts], eb_ref[h, r, :, ts] * e2, 0.0)
                        gate[ri] = w if gate[ri] is None else gate[ri] + w
                for ri in range(PEER_CHUNK_ROWS):
                    lo = ri * PEER_KEYS + b2 * PACK
                    hv = _gelu_tanh(h_t[lo:lo + PACK, ts]).astype(BF16)
                    act_ref[slot, e0 + lo:e0 + lo + PACK, ts] = hv * gate[ri]

    @pl.when(j == pl.num_programs(1) - 1)
    def _():
        o_ref[...] = _ln(DN_ALPHA * x_ref[...] + acc[...].T, g_ref[...], b_ref[...])


def _peer_ln(x, wq, k1, k2, u, vt, g, b):
    t = x.shape[0]
    tm = min(t, 512)
    const2 = lambda shape: pl.BlockSpec(shape, lambda i, j: (0, 0))
    const3 = lambda shape: pl.BlockSpec(shape, lambda i, j: (0, 0, 0))
    last = PEER_EXPERTS // PEER_TE - 1
    return pl.pallas_call(
        _peer_kernel,
        grid=(t // tm, last + 2),
        in_specs=[pl.BlockSpec((tm, D_MODEL), lambda i, j: (i, 0)),
                  const2((D_MODEL, PEER_HEADS * PEER_DK)),
                  const3((PEER_HEADS, PEER_KEYS, PEER_DK // 2)),
                  const3((PEER_HEADS, PEER_KEYS, PEER_DK // 2)),
                  pl.BlockSpec((PEER_TE, D_MODEL), lambda i, j: (jnp.minimum(j, last), 0)),
                  pl.BlockSpec((D_MODEL, PEER_TE), lambda i, j: (0, jnp.maximum(j - 1, 0))),
                  const2((1, D_MODEL)), const2((1, D_MODEL))],
        out_specs=pl.BlockSpec((tm, D_MODEL), lambda i, j: (i, 0)),
        out_shape=jax.ShapeDtypeStruct((t, D_MODEL), F32),
        scratch_shapes=[pltpu.VMEM((D_MODEL, tm), F32),
                        pltpu.VMEM((PEER_HEADS, PEER_KEYS // PEER_ROWS, PEER_ROWS, tm), F32),
                        pltpu.VMEM((PEER_HEADS, PEER_KEYS // PEER_ROWS, PEER_ROWS, tm), F32),
                        pltpu.VMEM((PEER_HEADS, PEER_KEYS, tm), BF16),
                        pltpu.VMEM((PEER_HEADS, PEER_KEYS, tm), BF16),
                        pltpu.VMEM((PEER_HEADS, PEER_ROWS, PACK, tm), BF16),
                        pltpu.VMEM((PEER_HEADS, PEER_ROWS, PACK, tm), BF16),
                        pltpu.VMEM((2, PEER_TE, tm), BF16)],
        compiler_params=_cp("parallel", "arbitrary"),
        name="peer",
    )(x, wq, k1, k2, u, vt, g.reshape(1, -1), b.reshape(1, -1))


def _even_prompt(x, bsz, seq, w_in, s5p, sinks, w_out, g, b):
    proj = _mm(x, w_in, tn=640)
    p3 = proj.reshape(bsz, seq, -1)
    u_tm = p3[:, :, :S5_WIDTH].transpose(1, 0, 2).reshape(seq * bsz, S5_WIDTH)
    steps = min(seq, 64)
    y_a, h_t = _s5(u_tm, jnp.zeros((bsz, 2 * S5_FLAT), F32), s5p, bsz, steps)
    y_a = y_a.reshape(seq, bsz, S5_WIDTH).transpose(1, 0, 2)
    y_b = _swa_prompt(p3, sinks)
    mix = jnp.concatenate([y_a, y_b], axis=-1).reshape(bsz * seq, -1)
    x = _mm_ln(mix, w_out, x, g, b)
    kv = p3[:, seq - WINDOW:, S5_WIDTH + SWA_WIDTH:]
    new_k = kv[..., :SWA_KV_WIDTH].reshape(bsz, WINDOW, SWA_KV_HEADS, SWA_HEAD_DIM)
    new_v = kv[..., SWA_KV_WIDTH:].reshape(bsz, WINDOW, SWA_KV_HEADS, SWA_HEAD_DIM)
    h_re = h_t[:, :S5_FLAT].reshape(bsz, S5_GROUPS, S5_STATE)
    h_im = h_t[:, S5_FLAT:].reshape(bsz, S5_GROUPS, S5_STATE)
    return x, h_re, h_im, new_k, new_v


def _even_sample(x, h_re, h_im, k_buf, v_buf, start, w_in, s5p, sinks, w_out, g, b):
    n = x.shape[0]
    proj = _mm(x, w_in, tn=640)
    h0 = jnp.concatenate([h_re.reshape(n, S5_FLAT), h_im.reshape(n, S5_FLAT)], axis=1)
    y_a, h_t = _s5(proj[:, :S5_WIDTH], h0, s5p, n, 1)
    kc = k_buf.reshape(n, WINDOW, SWA_KV_WIDTH)
    vc = v_buf.reshape(n, WINDOW, SWA_KV_WIDTH)
    y_b = _swa_step(proj, kc, vc, sinks, start)
    x = _mm_ln(jnp.concatenate([y_a, y_b], axis=-1), w_out, x, g, b)
    kn = proj[:, None, S5_WIDTH + SWA_WIDTH:S5_WIDTH + SWA_WIDTH + SWA_KV_WIDTH]
    vn = proj[:, None, S5_WIDTH + SWA_WIDTH + SWA_KV_WIDTH:]
    new_k = jnp.concatenate([kc[:, 1:], kn], axis=1).reshape(k_buf.shape)
    new_v = jnp.concatenate([vc[:, 1:], vn], axis=1).reshape(v_buf.shape)
    return (x, h_t[:, :S5_FLAT].reshape(h_re.shape), h_t[:, S5_FLAT:].reshape(h_im.shape), new_k, new_v)


def _odd_weights(w_in):
    w_z = w_in[:, :SSD_INNER].astype(BF16)
    w_x = w_in[:, SSD_INNER:SSD_INNER + SSD_CONV_DIM].astype(BF16)
    w_dt = jnp.pad(w_in[:, SSD_INNER + SSD_CONV_DIM:], ((0, 0), (0, LANES - SSD_HEADS))).astype(BF16)
    return w_z, w_x, w_dt


def _odd_prompt(x, bsz, seq, w3, ssdp, w_out, g, b):
    w_z, w_x, w_dt = w3
    z = _mm(x, w_z).reshape(bsz, seq, -1)
    xbc = _mm(x, w_x).reshape(bsz, seq, -1)
    dtraw = _mm(x, w_dt).reshape(bsz, seq, -1)
    y, st = _ssd_prompt(z, xbc, dtraw, ssdp)
    x = _mm_ln(y.reshape(bsz * seq, -1), w_out, x, g, b)
    return (x, st.reshape(bsz, SSD_HEADS, SSD_HEAD_DIM, SSD_STATE), xbc[:, seq - (SSD_CONV - 1):])


def _odd_sample(x, ssm, conv, w3, ssdp, w_out, g, b):
    n = x.shape[0]
    w_z, w_x, w_dt = w3
    z = _mm(x, w_z)
    xbc = _mm(x, w_x)
    dtraw = _mm(x, w_dt)
    y, st = _ssd_step(z, xbc, dtraw, conv, ssm.reshape(n, SSD_INNER, SSD_STATE), ssdp)
    x = _mm_ln(y, w_out, x, g, b)
    new_conv = jnp.concatenate([conv[:, 1:], xbc[:, None, :]], axis=1)
    return x, st.reshape(ssm.shape), new_conv


def kernel(x_prompt, x_sample, state_s5_re, state_s5_im, cache_swa_k, cache_swa_v, state_ssd, state_conv,
           w_in_even, s5_lambda_re, s5_lambda_im, s5_log_dt, s5_b_re, s5_b_im, s5_c_re, s5_c_im, s5_d,
           s5_w_glu, swa_sinks, w_out_even,
           w_in_odd, ssd_conv_w, ssd_conv_b, ssd_dt_bias, ssd_a_log, ssd_d, ssd_norm_w, w_out_odd,
           ln1_g, ln1_b, ln2_g, ln2_b, peer_w_q, peer_k1, peer_k2, peer_u, peer_v):
    bsz, seq, _ = x_prompt.shape
    nsm = x_sample.shape[0]
    xp = x_prompt.reshape(bsz * seq, D_MODEL)
    xs = x_sample.reshape(nsm, D_MODEL)
    p_out = [[] for _ in range(6)]
    s_out = [[] for _ in range(6)]
    for layer in range(DEPTH):
        i = layer // 2
        if layer % 2 == 0:
            w_in = w_in_even[i].astype(BF16)
            w_out = w_out_even[i].astype(BF16)
            s5p = _s5_params(s5_lambda_re[i], s5_lambda_im[i], s5_log_dt[i], s5_b_re[i], s5_b_im[i],
                             s5_c_re[i], s5_c_im[i], s5_d[i], s5_w_glu[i])
            xp, hr, hi, nk, nv = _even_prompt(xp, bsz, seq, w_in, s5p, swa_sinks[i], w_out,
                                              ln1_g[layer], ln1_b[layer])
            for lst, val in zip(p_out[:4], (hr, hi, nk, nv)):
                lst.append(val)
            xs, hr, hi, nk, nv = _even_sample(xs, state_s5_re[i], state_s5_im[i], cache_swa_k[i], cache_swa_v[i],
                                              PAST_LEN, w_in, s5p, swa_sinks[i], w_out,
                                              ln1_g[layer], ln1_b[layer])
            for lst, val in zip(s_out[:4], (hr, hi, nk, nv)):
                lst.append(val)
        else:
            w3 = _odd_weights(w_in_odd[i])
            w_out = w_out_odd[i].astype(BF16)
            ssdp = _ssd_params(ssd_conv_w[i], ssd_conv_b[i], ssd_dt_bias[i], ssd_a_log[i], ssd_d[i],
                               ssd_norm_w[i])
            xp, st, cb = _odd_prompt(xp, bsz, seq, w3, ssdp, w_out, ln1_g[layer], ln1_b[layer])
            p_out[4].append(st)
            p_out[5].append(cb)
            xs, st, cb = _odd_sample(xs, state_ssd[i], state_conv[i], w3, ssdp, w_out,
                                     ln1_g[layer], ln1_b[layer])
            s_out[4].append(st)
            s_out[5].append(cb)
        wq = peer_w_q[layer].astype(BF16)
        k1 = peer_k1[layer].astype(BF16)
        k2 = peer_k2[layer].astype(BF16)
        u = peer_u[layer].astype(BF16)
        vt = peer_v[layer].T.astype(BF16)
        xp = _peer_ln(xp, wq, k1, k2, u, vt, ln2_g[layer], ln2_b[layer])
        xs = _peer_ln(xs, wq, k1, k2, u, vt, ln2_g[layer], ln2_b[layer])
    stack = lambda lists: tuple(jnp.stack(l) for l in lists)
    return ((xp.reshape(bsz, seq, D_MODEL), xs.reshape(nsm, 1, D_MODEL)) + stack(p_out) + stack(s_out))
```

```python
import functools
import math

import jax
import jax.numpy as jnp
from jax import lax
from jax.experimental import pallas as pl
from jax.experimental.pallas import tpu as pltpu

F32 = jnp.float32
BF16 = jnp.bfloat16

D_MODEL = 1024
DEPTH = 4
PAST_LEN = 8192
S5_WIDTH = 512
S5_GROUP = 16
S5_GROUPS = 32
S5_STATE = 64
S5_FLAT = S5_GROUPS * S5_STATE
SWA_HEADS = 8
SWA_KV_HEADS = 2
SWA_REP = SWA_HEADS // SWA_KV_HEADS
SWA_HEAD_DIM = 64
SWA_WIDTH = SWA_HEADS * SWA_HEAD_DIM
SWA_KV_WIDTH = SWA_KV_HEADS * SWA_HEAD_DIM
WINDOW = 128
SSD_INNER = 2048
SSD_HEAD_DIM = 64
SSD_HEADS = 32
SSD_GROUPS = 4
SSD_HPG = SSD_HEADS // SSD_GROUPS
SSD_GW = SSD_HPG * SSD_HEAD_DIM
SSD_STATE = 128
SSD_CONV = 4
SSD_CHUNK = 128
SSD_CONV_DIM = SSD_INNER + 2 * SSD_GROUPS * SSD_STATE
PEER_HEADS = 8
PEER_KEYS = 128
PEER_EXPERTS = PEER_KEYS * PEER_KEYS
PEER_DK = 256
PEER_TOPK = 16
DN_ALPHA = (2 * DEPTH) ** 0.25
LN_EPS = 1e-5
RMS_EPS = 1e-5
NEG_INF = -1e30

LANES = 128
SUBLANES = 8
VMEM_LIMIT = 56 * 1024 * 1024


def _cp(*sem):
    return pltpu.CompilerParams(dimension_semantics=sem, vmem_limit_bytes=VMEM_LIMIT)


def _ln(x, g, b):
    mu = jnp.mean(x, axis=-1, keepdims=True)
    xc = x - mu
    var = jnp.mean(xc * xc, axis=-1, keepdims=True)
    return xc * lax.rsqrt(var + LN_EPS) * g + b


def _softplus(x):
    return jnp.maximum(x, 0.0) + jnp.log(1.0 + jnp.exp(-jnp.abs(x)))


def _silu(x):
    return x * jax.nn.sigmoid(x)


def _dot(a, b):
    return jnp.dot(a, b, preferred_element_type=F32)


def _dot_nt(a, b):
    return lax.dot_general(a, b, (((1,), (1,)), ((), ())), preferred_element_type=F32)


def _dot_tn(a, b):
    return lax.dot_general(a, b, (((0,), (0,)), ((), ())), preferred_element_type=F32)


def _split3(x):
    x1 = x.astype(BF16)
    r1 = x - x1.astype(F32)
    x2 = r1.astype(BF16)
    r2 = r1 - x2.astype(F32)
    return x1, x2, r2.astype(BF16)


def _dot_exact01(x, m01):
    x1, x2, x3 = _split3(x)
    return _dot(x1, m01) + _dot(x2, m01) + _dot(x3, m01)


def _mm_kernel(x_ref, w_ref, o_ref):
    o_ref[...] = _dot(x_ref[...].astype(BF16), w_ref[...]).astype(o_ref.dtype)


def _mm(x, w, tn=512, out_dtype=F32):
    m, k = x.shape
    n = w.shape[1]
    tm = min(m, 512)
    tn = min(n, tn)
    return pl.pallas_call(
        _mm_kernel,
        grid=(m // tm, n // tn),
        in_specs=[pl.BlockSpec((tm, k), lambda i, j: (i, 0)),
                  pl.BlockSpec((k, tn), lambda i, j: (0, j))],
        out_specs=pl.BlockSpec((tm, tn), lambda i, j: (i, j)),
        out_shape=jax.ShapeDtypeStruct((m, n), out_dtype),
        compiler_params=_cp("parallel", "arbitrary"),
        name="mm",
    )(x, w)


def _mm_ln_kernel(x_ref, w_ref, r_ref, g_ref, b_ref, o_ref):
    y = _dot(x_ref[...].astype(BF16), w_ref[...])
    o_ref[...] = _ln(DN_ALPHA * r_ref[...] + y, g_ref[...], b_ref[...])


def _mm_ln(x, w, res, g, b):
    m, k = x.shape
    n = w.shape[1]
    tm = min(m, 512)
    return pl.pallas_call(
        _mm_ln_kernel,
        grid=(m // tm,),
        in_specs=[pl.BlockSpec((tm, k), lambda i: (i, 0)),
                  pl.BlockSpec((k, n), lambda i: (0, 0)),
                  pl.BlockSpec((tm, n), lambda i: (i, 0)),
                  pl.BlockSpec((1, n), lambda i: (0, 0)),
                  pl.BlockSpec((1, n), lambda i: (0, 0))],
        out_specs=pl.BlockSpec((tm, n), lambda i: (i, 0)),
        out_shape=jax.ShapeDtypeStruct((m, n), F32),
        compiler_params=_cp("parallel"),
        name="mm_ln",
    )(x, w, res, g.reshape(1, n), b.reshape(1, n))


def _s5_kernel(u_ref, h0_ref, bd_ref, ar_ref, ai_ref, cbd_ref, d_ref, wg_ref,
               y_ref, hT_ref, hbuf, state, *, rows, steps):
    c = pl.program_id(0)

    @pl.when(c == 0)
    def _():
        state[...] = h0_ref[...]

    u = u_ref[...]
    hbuf[...] = _dot(u.astype(BF16), bd_ref[...])
    ar = ar_ref[...]
    ai = ai_ref[...]

    def step(t, carry):
        hr, hi = carry
        r0 = pl.multiple_of(t * rows, rows)
        br = hbuf[pl.ds(r0, rows), 0:S5_FLAT]
        bi = hbuf[pl.ds(r0, rows), S5_FLAT:2 * S5_FLAT]
        nr = ar * hr - ai * hi + br
        ni = ar * hi + ai * hr + bi
        hbuf[pl.ds(r0, rows), 0:S5_FLAT] = nr
        hbuf[pl.ds(r0, rows), S5_FLAT:2 * S5_FLAT] = ni
        return nr, ni

    hr, hi = lax.fori_loop(0, steps, step, (state[:, 0:S5_FLAT], state[:, S5_FLAT:2 * S5_FLAT]))
    state[:, 0:S5_FLAT] = hr
    state[:, S5_FLAT:2 * S5_FLAT] = hi

    y = _dot(hbuf[...].astype(BF16), cbd_ref[...]) + d_ref[...] * u
    g = jax.nn.gelu(y)
    y_ref[...] = g * jax.nn.sigmoid(_dot(g.astype(BF16), wg_ref[...]))

    @pl.when(c == pl.num_programs(0) - 1)
    def _():
        hT_ref[...] = state[...]


def _s5(u_tm, h0, prm, rows, steps):
    n = u_tm.shape[0]
    blk = rows * steps
    bd, ar, ai, cbd, d, wg = prm
    const = lambda shape: pl.BlockSpec(shape, lambda c: (0, 0))
    return pl.pallas_call(
        functools.partial(_s5_kernel, rows=rows, steps=steps),
        grid=(n // blk,),
        in_specs=[pl.BlockSpec((blk, S5_WIDTH), lambda c: (c, 0)),
                  const((rows, 2 * S5_FLAT)),
                  const((S5_WIDTH, 2 * S5_FLAT)),
                  const((1, S5_FLAT)), const((1, S5_FLAT)),
                  const((2 * S5_FLAT, S5_WIDTH)),
                  const((1, S5_WIDTH)),
                  const((S5_WIDTH, S5_WIDTH))],
        out_specs=[pl.BlockSpec((blk, S5_WIDTH), lambda c: (c, 0)),
                   const((rows, 2 * S5_FLAT))],
        out_shape=[jax.ShapeDtypeStruct((n, S5_WIDTH), F32),
                   jax.ShapeDtypeStruct((rows, 2 * S5_FLAT), F32)],
        scratch_shapes=[pltpu.VMEM((blk, 2 * S5_FLAT), F32),
                        pltpu.VMEM((rows, 2 * S5_FLAT), F32)],
        compiler_params=_cp("arbitrary"),
        name="s5",
    )(u_tm, h0, bd, ar, ai, cbd, d, wg)


def _s5_params(lam_re, lam_im, log_dt, b_re, b_im, c_re, c_im, d_skip, w_glu):
    dt = jnp.exp(log_dt)[:, None]
    mag = jnp.exp(lam_re * dt)
    ab_re, ab_im = mag * jnp.cos(lam_im * dt), mag * jnp.sin(lam_im * dt)
    den = lam_re * lam_re + lam_im * lam_im
    nr, ni = ab_re - 1.0, ab_im
    coef_re = ((nr * lam_re + ni * lam_im) / den)[..., None]
    coef_im = ((ni * lam_re - nr * lam_im) / den)[..., None]
    bb_re = coef_re * b_re - coef_im * b_im
    bb_im = coef_re * b_im + coef_im * b_re
    eye = jnp.eye(S5_GROUPS, dtype=F32)
    pack_in = lambda bb: jnp.einsum('gpc,gh->gchp', bb, eye).reshape(S5_WIDTH, S5_FLAT)
    bd = jnp.concatenate([pack_in(bb_re), pack_in(bb_im)], axis=1).astype(BF16)
    pack_out = lambda cc: jnp.einsum('gcp,gh->gphc', cc, eye).reshape(S5_FLAT, S5_WIDTH)
    cbd = jnp.concatenate([pack_out(c_re), -pack_out(c_im)], axis=0).astype(BF16)
    return (bd, ab_re.reshape(1, S5_FLAT), ab_im.reshape(1, S5_FLAT), cbd,
            d_skip.reshape(1, S5_WIDTH), w_glu.astype(BF16))


def _swa_kernel(sink_ref, q_ref, kp_ref, kc_ref, vp_ref, vc_ref, o_ref):
    n = pl.program_id(1)
    q = q_ref[0].astype(BF16)
    k = jnp.concatenate([kp_ref[0], kc_ref[0]], axis=0).astype(BF16)
    v = jnp.concatenate([vp_ref[0], vc_ref[0]], axis=0).astype(BF16)
    qi = lax.broadcasted_iota(jnp.int32, (WINDOW, 2 * WINDOW), 0)
    kj = lax.broadcasted_iota(jnp.int32, (WINDOW, 2 * WINDOW), 1)
    dist = WINDOW + qi - kj
    valid = (dist >= 0) & (dist < WINDOW) & ((n > 0) | (kj >= WINDOW))
    distf = dist.astype(F32)
    outs = []
    for h in range(SWA_HEADS):
        g = h // SWA_REP
        qh = q[:, h * SWA_HEAD_DIM:(h + 1) * SWA_HEAD_DIM]
        kg = k[:, g * SWA_HEAD_DIM:(g + 1) * SWA_HEAD_DIM]
        vg = v[:, g * SWA_HEAD_DIM:(g + 1) * SWA_HEAD_DIM]
        s = _dot_nt(qh, kg) * (SWA_HEAD_DIM ** -0.5)
        s = s - (2.0 ** -(h + 1)) * distf
        s = jnp.where(valid, s, NEG_INF)
        sink = sink_ref[h]
        m = jnp.maximum(jnp.max(s, axis=-1, keepdims=True), sink)
        p = jnp.exp(s - m)
        p = p / (jnp.sum(p, axis=-1, keepdims=True) + jnp.exp(sink - m))
        outs.append(_dot(p.astype(BF16), vg))
    o_ref[0] = jnp.concatenate(outs, axis=-1)


def _swa_prompt(proj, sinks):
    bsz, seq, _ = proj.shape
    kcol = (S5_WIDTH + SWA_WIDTH) // SWA_KV_WIDTH
    prev = lambda b, n: (b, jnp.maximum(n - 1, 0), kcol)
    prev_v = lambda b, n: (b, jnp.maximum(n - 1, 0), kcol + 1)
    blk = (1, WINDOW, SWA_KV_WIDTH)
    return pl.pallas_call(
        _swa_kernel,
        grid=(bsz, seq // WINDOW),
        in_specs=[pl.BlockSpec(memory_space=pltpu.SMEM),
                  pl.BlockSpec((1, WINDOW, SWA_WIDTH), lambda b, n: (b, n, 1)),
                  pl.BlockSpec(blk, prev),
                  pl.BlockSpec(blk, lambda b, n: (b, n, kcol)),
                  pl.BlockSpec(blk, prev_v),
                  pl.BlockSpec(blk, lambda b, n: (b, n, kcol + 1))],
        out_specs=pl.BlockSpec((1, WINDOW, SWA_WIDTH), lambda b, n: (b, n, 0)),
        out_shape=jax.ShapeDtypeStruct((bsz, seq, SWA_WIDTH), F32),
        compiler_params=_cp("parallel", "arbitrary"),
        name="swa_prompt",
    )(sinks, proj, proj, proj, proj, proj)


def _per_head_column(values):
    r = lax.broadcasted_iota(jnp.int32, (SWA_REP, 1), 0)
    col = jnp.full((SWA_REP, 1), values[-1], F32)
    for i in range(SWA_REP - 2, -1, -1):
        col = jnp.where(r == i, values[i], col)
    return col


def _swa_step_kernel(sink_ref, p_ref, kc_ref, vc_ref, o_ref, *, nseq, start):
    kj = lax.broadcasted_iota(jnp.int32, (SWA_REP, WINDOW), 1)
    dist = WINDOW - kj
    valid = (dist < WINDOW) & (start - WINDOW + kj >= 0)
    distf = dist.astype(F32)
    for i in range(nseq):
        row = p_ref[i]
        kn = row[:, S5_WIDTH + SWA_WIDTH:S5_WIDTH + SWA_WIDTH + SWA_KV_WIDTH].astype(BF16)
        vn = row[:, S5_WIDTH + SWA_WIDTH + SWA_KV_WIDTH:].astype(BF16)
        kc = kc_ref[i].astype(BF16)
        vc = vc_ref[i].astype(BF16)
        outs = []
        for g in range(SWA_KV_HEADS):
            q4 = jnp.concatenate(
                [row[:, S5_WIDTH + (g * SWA_REP + r) * SWA_HEAD_DIM:S5_WIDTH + (g * SWA_REP + r + 1) * SWA_HEAD_DIM]
                 for r in range(SWA_REP)], axis=0).astype(BF16)
            sl = slice(g * SWA_HEAD_DIM, (g + 1) * SWA_HEAD_DIM)
            slope = _per_head_column([2.0 ** -(g * SWA_REP + r + 1) for r in range(SWA_REP)])
            sink = _per_head_column([sink_ref[g * SWA_REP + r] for r in range(SWA_REP)])
            scale = SWA_HEAD_DIM ** -0.5
            sc = _dot_nt(q4, kc[:, sl]) * scale - slope * distf
            sc = jnp.where(valid, sc, NEG_INF)
            sn = jnp.sum(q4.astype(F32) * kn[:, sl].astype(F32), axis=-1, keepdims=True) * scale
            m = jnp.maximum(jnp.maximum(jnp.max(sc, axis=-1, keepdims=True), sn), sink)
            pc = jnp.exp(sc - m)
            pn = jnp.exp(sn - m)
            den = jnp.sum(pc, axis=-1, keepdims=True) + pn + jnp.exp(sink - m)
            o4 = _dot((pc / den).astype(BF16), vc[:, sl])
            o4 = o4 + (pn / den).astype(BF16).astype(F32) * vn[:, sl].astype(F32)
            outs.extend([o4[r:r + 1, :] for r in range(SWA_REP)])
        o_ref[i] = jnp.concatenate(outs, axis=-1)


def _swa_step(proj, k_cache, v_cache, sinks, start):
    n = proj.shape[0]
    nseq = min(n, SUBLANES)
    out = pl.pallas_call(
        functools.partial(_swa_step_kernel, nseq=nseq, start=start),
        grid=(n // nseq,),
        in_specs=[pl.BlockSpec(memory_space=pltpu.SMEM),
                  pl.BlockSpec((nseq, 1, proj.shape[1]), lambda i: (i, 0, 0)),
                  pl.BlockSpec((nseq, WINDOW, SWA_KV_WIDTH), lambda i: (i, 0, 0)),
                  pl.BlockSpec((nseq, WINDOW, SWA_KV_WIDTH), lambda i: (i, 0, 0))],
        out_specs=pl.BlockSpec((nseq, 1, SWA_WIDTH), lambda i: (i, 0, 0)),
        out_shape=jax.ShapeDtypeStruct((n, 1, SWA_WIDTH), F32),
        compiler_params=_cp("parallel"),
        name="swa_step",
    )(sinks, proj.reshape(n, 1, -1), k_cache, v_cache)
    return out.reshape(n, SWA_WIDTH)


def _head_expand(width):
    r = lax.broadcasted_iota(jnp.int32, (LANES, SSD_HEADS * width), 0)
    c = lax.broadcasted_iota(jnp.int32, (LANES, SSD_HEADS * width), 1)
    return jnp.where((c >= r * width) & (c < (r + 1) * width), 1.0, 0.0).astype(BF16)


def _ssd_kernel(x_ref, wz_ref, wx_ref, wdt_ref, cw_ref, cb_ref, dtb_ref, a_ref, d_ref, nw_ref,
                wo_ref, g_ref, b_ref, o_ref, st_ref, conv_ref, tail, state, ybuf):
    c = pl.program_id(1)
    q = SSD_CHUNK

    @pl.when(c == 0)
    def _():
        tail[...] = jnp.zeros_like(tail)
        state[...] = jnp.zeros_like(state)

    xin = x_ref[0]
    xb = xin.astype(BF16)
    cur = _dot(xb, wx_ref[...])
    tl = tail[...]
    row8 = lax.broadcasted_iota(jnp.int32, (SUBLANES, SSD_CONV_DIM), 0)
    acc = None
    for tap in range(SSD_CONV):
        sh = SSD_CONV - 1 - tap
        if sh == 0:
            term = cur
        else:
            rolled = pltpu.roll(cur, sh, 0)
            head = jnp.where(row8 < sh, pltpu.roll(tl, sh, 0), rolled[0:SUBLANES])
            term = jnp.concatenate([head, rolled[SUBLANES:]], axis=0)
        term = term * cw_ref[tap:tap + 1, :]
        acc = term if acc is None else acc + term
    tail[...] = cur[q - SUBLANES:q]
    xbc = _silu(acc + cb_ref[...])
    xs = xbc[:, 0:SSD_INNER]
    bm = xbc[:, SSD_INNER:SSD_INNER + SSD_GROUPS * SSD_STATE].astype(BF16)
    cm = xbc[:, SSD_INNER + SSD_GROUPS * SSD_STATE:].astype(BF16)

    dt = _softplus(_dot(xb, wdt_ref[...]) + dtb_ref[...])
    da = dt * a_ref[...]
    ri = lax.broadcasted_iota(jnp.int32, (q, q), 0)
    ci = lax.broadcasted_iota(jnp.int32, (q, q), 1)
    causal = ri >= ci
    tri = jnp.where(causal, 1.0, 0.0).astype(BF16)
    da1, da2, da3 = _split3(da)
    acs = _dot(tri, da1) + _dot(tri, da2) + _dot(tri, da3)
    acs_t = acs.T
    e64 = _head_expand(SSD_HEAD_DIM)
    dt_x = _dot_exact01(dt, e64)
    acs_x = _dot_exact01(acs, e64)
    last_x = acs_x[q - 1:q, :]
    dx = dt_x * xs
    dsx = (jnp.exp(last_x - acs_x) * dx).astype(BF16)
    eacs_x = jnp.exp(acs_x)
    dxb = dx.astype(BF16)

    for g in range(SSD_GROUPS):
        ns = slice(g * SSD_STATE, (g + 1) * SSD_STATE)
        gs = slice(g * SSD_GW, (g + 1) * SSD_GW)
        cb = _dot_nt(cm[:, ns], bm[:, ns])
        for r in range(SSD_HPG):
            h = g * SSD_HPG + r
            seg = acs[:, h:h + 1] - acs_t[h:h + 1, :]
            decay = jnp.where(causal, jnp.exp(jnp.where(causal, seg, 0.0)), 0.0)
            hs = slice(h * SSD_HEAD_DIM, (h + 1) * SSD_HEAD_DIM)
            ybuf[:, hs] = _dot((cb * decay).astype(BF16), dxb[:, hs])
        s_in = state[g]
        y_off = _dot(cm[:, ns], s_in.astype(BF16)) * eacs_x[:, gs]
        ybuf[:, gs] = ybuf[:, gs] + y_off
        state[g] = jnp.exp(last_x[:, gs]) * s_in + _dot_tn(bm[:, ns], dsx[:, gs])

    y = ybuf[...] + d_ref[...] * xs
    y = y * _silu(_dot(xb, wz_ref[...]))
    y = y * lax.rsqrt(jnp.mean(y * y, axis=-1, keepdims=True) + RMS_EPS) * nw_ref[...]
    mix = _dot(y.astype(BF16), wo_ref[...])
    o_ref[0] = _ln(DN_ALPHA * xin + mix, g_ref[...], b_ref[...])

    @pl.when(c == pl.num_programs(1) - 1)
    def _():
        conv_ref[0] = cur[q - SUBLANES:q]
        for g in range(SSD_GROUPS):
            st_ref[0, g * SSD_GW:(g + 1) * SSD_GW, :] = state[g].T


def _ssd_prompt(x, w3, prm, w_out, g, b):
    bsz, seq, _ = x.shape
    w_z, w_x, w_dt = w3
    cw, cb, dtb, a, d_x, nw = prm
    const = lambda shape: pl.BlockSpec(shape, lambda b, c: (0, 0))
    seqblk = lambda w: pl.BlockSpec((1, SSD_CHUNK, w), lambda b, c: (b, c, 0))
    return pl.pallas_call(
        _ssd_kernel,
        grid=(bsz, seq // SSD_CHUNK),
        in_specs=[seqblk(D_MODEL),
                  const((D_MODEL, SSD_INNER)), const((D_MODEL, SSD_CONV_DIM)), const((D_MODEL, LANES)),
                  const((SSD_CONV, SSD_CONV_DIM)), const((1, SSD_CONV_DIM)),
                  const((1, LANES)), const((1, LANES)),
                  const((1, SSD_INNER)), const((1, SSD_INNER)),
                  const((SSD_INNER, D_MODEL)), const((1, D_MODEL)), const((1, D_MODEL))],
        out_specs=[seqblk(D_MODEL),
                   pl.BlockSpec((1, SSD_INNER, SSD_STATE), lambda b, c: (b, 0, 0)),
                   pl.BlockSpec((1, SUBLANES, SSD_CONV_DIM), lambda b, c: (b, 0, 0))],
        out_shape=[jax.ShapeDtypeStruct((bsz, seq, D_MODEL), F32),
                   jax.ShapeDtypeStruct((bsz, SSD_INNER, SSD_STATE), F32),
                   jax.ShapeDtypeStruct((bsz, SUBLANES, SSD_CONV_DIM), F32)],
        scratch_shapes=[pltpu.VMEM((SUBLANES, SSD_CONV_DIM), F32),
                        pltpu.VMEM((SSD_GROUPS, SSD_STATE, SSD_GW), F32),
                        pltpu.VMEM((SSD_CHUNK, SSD_INNER), F32)],
        compiler_params=_cp("parallel", "arbitrary"),
        name="ssd_prompt",
    )(x, w_z, w_x, w_dt, cw, cb, dtb, a, d_x, nw, w_out, g.reshape(1, -1), b.reshape(1, -1))


def _pad_rows(row):
    return jnp.concatenate([row, jnp.zeros((SUBLANES - 1, row.shape[1]), row.dtype)], axis=0)


def _ssd_step_kernel(z_ref, x_ref, dt_ref, buf_ref, s0_ref, cw_ref, cb_ref, dtb_ref, a_ref, d_ref, nw_ref,
                     y_ref, s1_ref, *, nseq):
    e64 = _head_expand(SSD_HEAD_DIM)
    ones3 = jnp.where(lax.broadcasted_iota(jnp.int32, (SUBLANES, SSD_STATE), 0) < 3, 1.0, 0.0).astype(BF16)
    for i in range(nseq):
        buf = buf_ref[i]
        acc = buf[0:1] * cw_ref[0:1, :]
        acc = acc + buf[1:2] * cw_ref[1:2, :]
        acc = acc + buf[2:3] * cw_ref[2:3, :]
        acc = acc + x_ref[i] * cw_ref[3:4, :]
        xbc = _silu(acc + cb_ref[...])
        xs = xbc[:, 0:SSD_INNER]
        bm = xbc[:, SSD_INNER:SSD_INNER + SSD_GROUPS * SSD_STATE].astype(BF16)
        cm = xbc[:, SSD_INNER + SSD_GROUPS * SSD_STATE:].astype(BF16)
        dt = _softplus(dt_ref[i] + dtb_ref[...])
        ed = jnp.exp(dt * a_ref[...])
        dt_x = _dot_exact01(_pad_rows(dt), e64)[0:1]
        ed_x = _dot_exact01(_pad_rows(ed), e64)[0:1]
        dx = dt_x * xs
        dxb = dx.astype(BF16)
        ys = []
        for g in range(SSD_GROUPS):
            ns = slice(g * SSD_STATE, (g + 1) * SSD_STATE)
            gs = slice(g * SSD_GW, (g + 1) * SSD_GW)
            s0 = s0_ref[i, gs, :]
            cmg = _pad_rows(cm[:, ns])
            cbv = jnp.sum(cm[:, ns].astype(F32) * bm[:, ns].astype(F32), axis=-1, keepdims=True)
            y_diag = cbv.astype(BF16).astype(F32) * dxb[:, gs].astype(F32)
            y_off = _dot_nt(cmg, s0.astype(BF16))[0:1] * ed_x[:, gs]
            ys.append(y_diag + y_off)
            e1, e2, e3 = _split3(ed_x[:, gs])
            ed_rows = jnp.concatenate([e1, e2, e3, jnp.zeros((SUBLANES - 3, SSD_GW), BF16)], axis=0)
            ed_full = _dot_tn(ed_rows, ones3)
            outer = _dot_tn(_pad_rows(dxb[:, gs]), _pad_rows(bm[:, ns]))
            s1_ref[i, gs, :] = ed_full * s0 + outer
        y = jnp.concatenate(ys, axis=-1) + d_ref[...] * xs
        y = y * _silu(z_ref[i])
        y = y * lax.rsqrt(jnp.mean(y * y, axis=-1, keepdims=True) + RMS_EPS) * nw_ref[...]
        y_ref[i] = y.astype(y_ref.dtype)


def _ssd_step(z, xbc, dtraw, conv_buf, s0_all, layer, prm):
    n = z.shape[0]
    nseq = min(n, 4)
    cw, cb, dtb, a, d_x, nw = prm
    const = lambda shape: pl.BlockSpec(shape, lambda i: (0, 0))
    tok = lambda w: pl.BlockSpec((nseq, 1, w), lambda i: (i, 0, 0))
    y, s1 = pl.pallas_call(
        functools.partial(_ssd_step_kernel, nseq=nseq),
        grid=(n // nseq,),
        in_specs=[tok(SSD_INNER), tok(SSD_CONV_DIM), tok(LANES),
                  pl.BlockSpec((nseq, SSD_CONV - 1, SSD_CONV_DIM), lambda i: (i, 0, 0)),
                  pl.BlockSpec((None, nseq, SSD_INNER, SSD_STATE), lambda i: (layer, i, 0, 0)),
                  const((SSD_CONV, SSD_CONV_DIM)), const((1, SSD_CONV_DIM)),
                  const((1, LANES)), const((1, LANES)),
                  const((1, SSD_INNER)), const((1, SSD_INNER))],
        out_specs=[tok(SSD_INNER),
                   pl.BlockSpec((nseq, SSD_INNER, SSD_STATE), lambda i: (i, 0, 0))],
        out_shape=[jax.ShapeDtypeStruct((n, 1, SSD_INNER), BF16),
                   jax.ShapeDtypeStruct((n, SSD_INNER, SSD_STATE), F32)],
        compiler_params=_cp("parallel"),
        name="ssd_step",
    )(z.reshape(n, 1, -1), xbc.reshape(n, 1, -1), dtraw.reshape(n, 1, -1), conv_buf, s0_all,
      cw, cb, dtb, a, d_x, nw)
    return y.reshape(n, SSD_INNER), s1


def _ssd_params(conv_w, conv_b, dt_bias, a_log, d_skip, norm_w):
    pad = lambda v: jnp.pad(v, (0, LANES - SSD_HEADS)).reshape(1, LANES)
    return (conv_w, conv_b.reshape(1, -1), pad(dt_bias), pad(-jnp.exp(a_log)),
            jnp.repeat(d_skip, SSD_HEAD_DIM).reshape(1, SSD_INNER), norm_w.reshape(1, SSD_INNER))


PEER_TE = 1024
PEER_ROWS = PEER_TE // PEER_KEYS


PEER_CHUNK_ROWS = 2
PEER_TOK_CHUNK = 256
PACK = 2 * SUBLANES


def _batcher_sort_pairs(n):
    def merge(lo, hi, r):
        step = r * 2
        if step < hi - lo:
            yield from merge(lo, hi, step)
            yield from merge(lo + r, hi, step)
            for i in range(lo + r, hi - r, step):
                yield (i, i + r)
        else:
            yield (lo, lo + r)

    def sort(lo, hi):
        if hi - lo >= 1:
            mid = lo + (hi - lo) // 2
            yield from sort(lo, mid)
            yield from sort(mid + 1, hi)
            yield from merge(lo, hi, 1)

    return tuple(sort(0, n - 1))


_SORT16 = _batcher_sort_pairs(PEER_TOPK)
_BITONIC16 = tuple((i, i + d) for d in (8, 4, 2, 1) for i in range(PEER_TOPK) if not i & d)


def _mx(a, b):
    return b if a is None else a if b is None else jnp.maximum(a, b)


def _mn(a, b):
    return None if a is None or b is None else jnp.minimum(a, b)


def _top16(rows):
    v = list(rows) + [None] * (PEER_TOPK - len(rows))
    for i, j in _SORT16:
        v[i], v[j] = _mx(v[i], v[j]), _mn(v[i], v[j])
    for shift in (4, 2, 1):
        w = [None if x is None else pltpu.roll(x, shift, 0) for x in v]
        v = [_mx(v[k], w[PEER_TOPK - 1 - k]) for k in range(PEER_TOPK)]
        for i, j in _BITONIC16:
            v[i], v[j] = _mx(v[i], v[j]), _mn(v[i], v[j])
    return v


def _prefix_count(test, thr):
    m8 = test(thr[7])
    m4 = test(jnp.where(m8, thr[11], thr[3]))
    m2 = test(jnp.where(m8, jnp.where(m4, thr[13], thr[9]), jnp.where(m4, thr[5], thr[1])))
    hi = jnp.where(m4, jnp.where(m2, thr[14], thr[12]), jnp.where(m2, thr[10], thr[8]))
    lo = jnp.where(m4, jnp.where(m2, thr[6], thr[4]), jnp.where(m2, thr[2], thr[0]))
    m1 = test(jnp.where(m8, hi, lo))
    return (jnp.where(m8, 8.0, 0.0) + jnp.where(m4, 4.0, 0.0)) + (jnp.where(m2, 2.0, 0.0) + jnp.where(m1, 1.0, 0.0))


def _peer_route(xb, wq_ref, k1_ref, k2_ref, c_ref, e1_ref, rk_ref, e2_ref):
    tm = xb.shape[0]
    q = _dot(xb, wq_ref[...]).astype(BF16)
    half = PEER_DK // 2
    nblk = PEER_KEYS // SUBLANES
    sub = lax.broadcasted_iota(jnp.int32, (SUBLANES, tm), 0)

    def by_sublane(vals):
        out = vals[SUBLANES - 1]
        for s in range(SUBLANES - 2, -1, -1):
            out = jnp.where(sub == s, vals[s], out)
        return out

    for h in range(PEER_HEADS):
        s1 = _dot_nt(k1_ref[h], q[:, h * PEER_DK:h * PEER_DK + half])
        s2 = _dot_nt(k2_ref[h], q[:, h * PEER_DK + half:(h + 1) * PEER_DK])
        r1 = [s1[k * SUBLANES:(k + 1) * SUBLANES] for k in range(nblk)]
        r2 = [s2[k * SUBLANES:(k + 1) * SUBLANES] for k in range(nblk)]
        a = _top16(r1)
        b = _top16(r2)
        r1 = [x - a[0] for x in r1]
        r2 = [x - b[0] for x in r2]
        a = [x - a[0] for x in a]
        b = [x - b[0] for x in b]
        b_lo, b_hi, a_hi = by_sublane(b[0:8]), by_sublane(b[8:16]), by_sublane(a[8:16])
        cand = [a[0] + b_lo, a[0] + b_hi] + [a[i] + b_lo for i in range(1, 8)] + [a_hi + b[0]]
        top = _top16(cand)
        tau = top[PEER_TOPK - 1]
        zsum = jnp.exp(top[0])
        for k in range(1, PEER_TOPK):
            zsum = zsum + jnp.exp(top[k])
        inv_z = 1.0 / zsum
        for k in range(nblk):
            x1 = r1[k]
            cnt = _prefix_count(lambda t: (x1 + t) >= tau, b[0:15])
            cnt = cnt + jnp.where((x1 + b[15]) >= tau, 1.0, 0.0)
            c_ref[h, k, :, 0:tm] = cnt
            e1_ref[h, k, :, 0:tm] = jnp.exp(x1) * inv_z
        ranks, e2s = [], []
        for k in range(nblk):
            x2 = r2[k]
            rnk = _prefix_count(lambda t: t > x2, b[0:15])
            ranks.append(jnp.where(x2 >= b[15], rnk, float(PEER_TOPK)))
            e2s.append(jnp.exp(x2))
        rk_ref[h, :, 0:tm] = jnp.concatenate(ranks, axis=0).astype(BF16)
        e2_ref[h, :, 0:tm] = jnp.concatenate(e2s, axis=0).astype(BF16)


def _gelu_tanh(x):
    c0 = math.sqrt(2.0 / math.pi)
    return (0.5 * x) * (1.0 + jnp.tanh(x * (c0 + (c0 * 0.044715) * (x * x))))


def _peer_kernel(x_ref, wq_ref, k1_ref, k2_ref, u_ref, vt_ref, g_ref, b_ref, o_ref,
                 acc, c_ref, e1_ref, rk_ref, e2_ref, act_ref):
    j = pl.program_id(1)
    tm = x_ref.shape[0]
    xb = x_ref[...].astype(BF16)

    @pl.when(j == 0)
    def _():
        acc[:, 0:tm] = jnp.zeros((D_MODEL, tm), F32)
        _peer_route(xb, wq_ref, k1_ref, k2_ref, c_ref, e1_ref, rk_ref, e2_ref)
        act_ref[1, :, 0:tm] = jnp.zeros((PEER_TE, tm), BF16)

    nb2 = PEER_KEYS // PACK
    jt = jnp.minimum(j, PEER_EXPERTS // PEER_TE - 1)
    slot = j % 2

    crows = PEER_CHUNK_ROWS
    nch = PEER_ROWS // crows
    tc = min(tm, PEER_TOK_CHUNK)
    apply_after = {nch // 2 - 1 + k: k * tc for k in range(tm // tc)}
    for ch in range(nch):
        e0 = ch * crows * PEER_KEYS
        h_t = _dot_nt(u_ref[e0:e0 + crows * PEER_KEYS, :], xb)
        if ch in apply_after:
            t0 = apply_after[ch]
            acc[:, t0:t0 + tc] += _dot(vt_ref[...], act_ref[1 - slot, :, t0:t0 + tc])
        rows = [ch * crows + ri for ri in range(crows)]
        for tl in range(tm // LANES):
            ts = slice(tl * LANES, (tl + 1) * LANES)
            gate = [[None] * nb2 for _ in range(crows)]
            for h in range(PEER_HEADS):
                cbv = [jnp.broadcast_to(c_ref[h, jt, r:r + 1, ts], (PACK, LANES)).astype(BF16) for r in rows]
                ebv = [jnp.broadcast_to(e1_ref[h, jt, r:r + 1, ts], (PACK, LANES)).astype(BF16) for r in rows]
                for b2 in range(nb2):
                    ks = slice(b2 * PACK, (b2 + 1) * PACK)
                    rk = rk_ref[h, ks, ts]
                    e2 = e2_ref[h, ks, ts]
                    for ri in range(crows):
                        w = jnp.where(rk < cbv[ri], ebv[ri] * e2, 0.0)
                        gate[ri][b2] = w if gate[ri][b2] is None else gate[ri][b2] + w
            for ri in range(crows):
                for b2 in range(nb2):
                    lo = ri * PEER_KEYS + b2 * PACK
                    hv = _gelu_tanh(h_t[lo:lo + PACK, ts]).astype(BF16)
                    act_ref[slot, e0 + lo:e0 + lo + PACK, ts] = hv * gate[ri][b2]

    @pl.when(j == pl.num_programs(1) - 1)
    def _():
        o_ref[...] = _ln(DN_ALPHA * x_ref[...] + acc[:, 0:tm].T, g_ref[...], b_ref[...])


def _peer_ln(x, wq, k1, k2, u, vt, g, b):
    t = x.shape[0]
    tm = min(t, 512)
    const2 = lambda shape: pl.BlockSpec(shape, lambda i, j: (0, 0))
    const3 = lambda shape: pl.BlockSpec(shape, lambda i, j: (0, 0, 0))
    last = PEER_EXPERTS // PEER_TE - 1
    return pl.pallas_call(
        _peer_kernel,
        grid=(t // tm, last + 2),
        in_specs=[pl.BlockSpec((tm, D_MODEL), lambda i, j: (i, 0)),
                  const2((D_MODEL, PEER_HEADS * PEER_DK)),
                  const3((PEER_HEADS, PEER_KEYS, PEER_DK // 2)),
                  const3((PEER_HEADS, PEER_KEYS, PEER_DK // 2)),
                  pl.BlockSpec((PEER_TE, D_MODEL), lambda i, j: (jnp.minimum(j, last), 0)),
                  pl.BlockSpec((D_MODEL, PEER_TE), lambda i, j: (0, jnp.maximum(j - 1, 0))),
                  const2((1, D_MODEL)), const2((1, D_MODEL))],
        out_specs=pl.BlockSpec((tm, D_MODEL), lambda i, j: (i, 0)),
        out_shape=jax.ShapeDtypeStruct((t, D_MODEL), F32),
        scratch_shapes=[pltpu.VMEM((D_MODEL, tm), F32),
                        pltpu.VMEM((PEER_HEADS, PEER_KEYS // PEER_ROWS, PEER_ROWS, tm), F32),
                        pltpu.VMEM((PEER_HEADS, PEER_KEYS // PEER_ROWS, PEER_ROWS, tm), F32),
                        pltpu.VMEM((PEER_HEADS, PEER_KEYS, tm), BF16),
                        pltpu.VMEM((PEER_HEADS, PEER_KEYS, tm), BF16),
                        pltpu.VMEM((2, PEER_TE, tm), BF16)],
        compiler_params=_cp("parallel", "arbitrary"),
        name="peer",
    )(x, wq, k1, k2, u, vt, g.reshape(1, -1), b.reshape(1, -1))


def _even_prompt(x, bsz, seq, w_in, s5p, sinks, w_out, g, b):
    proj = _mm(x, w_in, tn=640)
    p3 = proj.reshape(bsz, seq, -1)
    u_tm = p3[:, :, :S5_WIDTH].transpose(1, 0, 2).reshape(seq * bsz, S5_WIDTH)
    steps = min(seq, 64)
    y_a, h_t = _s5(u_tm, jnp.zeros((bsz, 2 * S5_FLAT), F32), s5p, bsz, steps)
    y_a = y_a.reshape(seq, bsz, S5_WIDTH).transpose(1, 0, 2)
    y_b = _swa_prompt(p3, sinks)
    mix = jnp.concatenate([y_a, y_b], axis=-1).reshape(bsz * seq, -1)
    x = _mm_ln(mix, w_out, x, g, b)
    kv = p3[:, seq - WINDOW:, S5_WIDTH + SWA_WIDTH:]
    new_k = kv[..., :SWA_KV_WIDTH].reshape(bsz, WINDOW, SWA_KV_HEADS, SWA_HEAD_DIM)
    new_v = kv[..., SWA_KV_WIDTH:].reshape(bsz, WINDOW, SWA_KV_HEADS, SWA_HEAD_DIM)
    h_re = h_t[:, :S5_FLAT].reshape(bsz, S5_GROUPS, S5_STATE)
    h_im = h_t[:, S5_FLAT:].reshape(bsz, S5_GROUPS, S5_STATE)
    return x, h_re, h_im, new_k, new_v


def _even_sample(x, h_re, h_im, k_buf, v_buf, start, w_in, s5p, sinks, w_out, g, b):
    n = x.shape[0]
    proj = _mm(x, w_in, tn=640)
    h0 = jnp.concatenate([h_re.reshape(n, S5_FLAT), h_im.reshape(n, S5_FLAT)], axis=1)
    y_a, h_t = _s5(proj[:, :S5_WIDTH], h0, s5p, n, 1)
    kc = k_buf.reshape(n, WINDOW, SWA_KV_WIDTH)
    vc = v_buf.reshape(n, WINDOW, SWA_KV_WIDTH)
    y_b = _swa_step(proj, kc, vc, sinks, start)
    x = _mm_ln(jnp.concatenate([y_a, y_b], axis=-1), w_out, x, g, b)
    kn = proj[:, None, S5_WIDTH + SWA_WIDTH:S5_WIDTH + SWA_WIDTH + SWA_KV_WIDTH]
    vn = proj[:, None, S5_WIDTH + SWA_WIDTH + SWA_KV_WIDTH:]
    new_k = jnp.concatenate([kc[:, 1:], kn], axis=1).reshape(k_buf.shape)
    new_v = jnp.concatenate([vc[:, 1:], vn], axis=1).reshape(v_buf.shape)
    return (x, h_t[:, :S5_FLAT].reshape(h_re.shape), h_t[:, S5_FLAT:].reshape(h_im.shape), new_k, new_v)


def _odd_weights(w_in):
    w_z = w_in[:, :SSD_INNER].astype(BF16)
    w_x = w_in[:, SSD_INNER:SSD_INNER + SSD_CONV_DIM].astype(BF16)
    w_dt = jnp.pad(w_in[:, SSD_INNER + SSD_CONV_DIM:], ((0, 0), (0, LANES - SSD_HEADS))).astype(BF16)
    return w_z, w_x, w_dt


def _odd_prompt(x, bsz, seq, w3, ssdp, w_out, g, b):
    x, st, tail = _ssd_prompt(x.reshape(bsz, seq, -1), w3, ssdp, w_out, g, b)
    return (x.reshape(bsz * seq, -1), st.reshape(bsz, SSD_HEADS, SSD_HEAD_DIM, SSD_STATE),
            tail[:, SUBLANES - (SSD_CONV - 1):])


def _odd_sample(x, ssm_all, layer, conv, w3, ssdp, w_out, g, b):
    n = x.shape[0]
    w_z, w_x, w_dt = w3
    z = _mm(x, w_z)
    xbc = _mm(x, w_x)
    dtraw = _mm(x, w_dt)
    s0_all = ssm_all.reshape(ssm_all.shape[0], n, SSD_INNER, SSD_STATE)
    y, st = _ssd_step(z, xbc, dtraw, conv, s0_all, layer, ssdp)
    x = _mm_ln(y, w_out, x, g, b)
    new_conv = jnp.concatenate([conv[:, 1:], xbc[:, None, :]], axis=1)
    return x, st.reshape(ssm_all.shape[1:]), new_conv


def kernel(x_prompt, x_sample, state_s5_re, state_s5_im, cache_swa_k, cache_swa_v, state_ssd, state_conv,
           w_in_even, s5_lambda_re, s5_lambda_im, s5_log_dt, s5_b_re, s5_b_im, s5_c_re, s5_c_im, s5_d,
           s5_w_glu, swa_sinks, w_out_even,
           w_in_odd, ssd_conv_w, ssd_conv_b, ssd_dt_bias, ssd_a_log, ssd_d, ssd_norm_w, w_out_odd,
           ln1_g, ln1_b, ln2_g, ln2_b, peer_w_q, peer_k1, peer_k2, peer_u, peer_v):
    bsz, seq, _ = x_prompt.shape
    nsm = x_sample.shape[0]
    xp = x_prompt.reshape(bsz * seq, D_MODEL)
    xs = x_sample.reshape(nsm, D_MODEL)
    p_out = [[] for _ in range(6)]
    s_out = [[] for _ in range(6)]
    for layer in range(DEPTH):
        i = layer // 2
        if layer % 2 == 0:
            w_in = w_in_even[i].astype(BF16)
            w_out = w_out_even[i].astype(BF16)
            s5p = _s5_params(s5_lambda_re[i], s5_lambda_im[i], s5_log_dt[i], s5_b_re[i], s5_b_im[i],
                             s5_c_re[i], s5_c_im[i], s5_d[i], s5_w_glu[i])
            xp, hr, hi, nk, nv = _even_prompt(xp, bsz, seq, w_in, s5p, swa_sinks[i], w_out,
                                              ln1_g[layer], ln1_b[layer])
            for lst, val in zip(p_out[:4], (hr, hi, nk, nv)):
                lst.append(val)
            xs, hr, hi, nk, nv = _even_sample(xs, state_s5_re[i], state_s5_im[i], cache_swa_k[i], cache_swa_v[i],
                                              PAST_LEN, w_in, s5p, swa_sinks[i], w_out,
                                              ln1_g[layer], ln1_b[layer])
            for lst, val in zip(s_out[:4], (hr, hi, nk, nv)):
                lst.append(val)
        else:
            w3 = _odd_weights(w_in_odd[i])
            w_out = w_out_odd[i].astype(BF16)
            ssdp = _ssd_params(ssd_conv_w[i], ssd_conv_b[i], ssd_dt_bias[i], ssd_a_log[i], ssd_d[i],
                               ssd_norm_w[i])
            xp, st, cb = _odd_prompt(xp, bsz, seq, w3, ssdp, w_out, ln1_g[layer], ln1_b[layer])
            p_out[4].append(st)
            p_out[5].append(cb)
            xs, st, cb = _odd_sample(xs, state_ssd, i, state_conv[i], w3, ssdp, w_out,
                                     ln1_g[layer], ln1_b[layer])
            s_out[4].append(st)
            s_out[5].append(cb)
        wq = peer_w_q[layer].astype(BF16)
        k1 = peer_k1[layer].astype(BF16)
        k2 = peer_k2[layer].astype(BF16)
        u = peer_u[layer].astype(BF16)
        vt = peer_v[layer].T.astype(BF16)
        xp = _peer_ln(xp, wq, k1, k2, u, vt, ln2_g[layer], ln2_b[layer])
        xs = _peer_ln(xs, wq, k1, k2, u, vt, ln2_g[layer], ln2_b[layer])
    stack = lambda lists: tuple(jnp.stack(l) for l in lists)
    return ((xp.reshape(bsz, seq, D_MODEL), xs.reshape(nsm, 1, D_MODEL)) + stack(p_out) + stack(s_out))
```

```python
import functools
import math

import jax
import jax.numpy as jnp
from jax import lax
from jax.experimental import pallas as pl
from jax.experimental.pallas import tpu as pltpu

F32 = jnp.float32
BF16 = jnp.bfloat16

D_MODEL = 1024
DEPTH = 4
PAST_LEN = 8192
S5_WIDTH = 512
S5_GROUP = 16
S5_GROUPS = 32
S5_STATE = 64
S5_FLAT = S5_GROUPS * S5_STATE
SWA_HEADS = 8
SWA_KV_HEADS = 2
SWA_REP = SWA_HEADS // SWA_KV_HEADS
SWA_HEAD_DIM = 64
SWA_WIDTH = SWA_HEADS * SWA_HEAD_DIM
SWA_KV_WIDTH = SWA_KV_HEADS * SWA_HEAD_DIM
WINDOW = 128
SSD_INNER = 2048
SSD_HEAD_DIM = 64
SSD_HEADS = 32
SSD_GROUPS = 4
SSD_HPG = SSD_HEADS // SSD_GROUPS
SSD_GW = SSD_HPG * SSD_HEAD_DIM
SSD_STATE = 128
SSD_CONV = 4
SSD_CHUNK = 128
SSD_CONV_DIM = SSD_INNER + 2 * SSD_GROUPS * SSD_STATE
PEER_HEADS = 8
PEER_KEYS = 128
PEER_EXPERTS = PEER_KEYS * PEER_KEYS
PEER_DK = 256
PEER_TOPK = 16
DN_ALPHA = (2 * DEPTH) ** 0.25
LN_EPS = 1e-5
RMS_EPS = 1e-5
NEG_INF = -1e30

LANES = 128
SUBLANES = 8
VMEM_LIMIT = 56 * 1024 * 1024


def _cp(*sem):
    return pltpu.CompilerParams(dimension_semantics=sem, vmem_limit_bytes=VMEM_LIMIT)


def _ln(x, g, b):
    mu = jnp.mean(x, axis=-1, keepdims=True)
    xc = x - mu
    var = jnp.mean(xc * xc, axis=-1, keepdims=True)
    return xc * lax.rsqrt(var + LN_EPS) * g + b


def _softplus(x):
    return jnp.maximum(x, 0.0) + jnp.log(1.0 + jnp.exp(-jnp.abs(x)))


def _silu(x):
    return x * jax.nn.sigmoid(x)


def _dot(a, b):
    return jnp.dot(a, b, preferred_element_type=F32)


def _dot_nt(a, b):
    return lax.dot_general(a, b, (((1,), (1,)), ((), ())), preferred_element_type=F32)


def _dot_tn(a, b):
    return lax.dot_general(a, b, (((0,), (0,)), ((), ())), preferred_element_type=F32)


def _split3(x):
    x1 = x.astype(BF16)
    r1 = x - x1.astype(F32)
    x2 = r1.astype(BF16)
    r2 = r1 - x2.astype(F32)
    return x1, x2, r2.astype(BF16)


def _dot_exact01(x, m01):
    x1, x2, x3 = _split3(x)
    return _dot(x1, m01) + _dot(x2, m01) + _dot(x3, m01)


def _mm_kernel(x_ref, w_ref, o_ref):
    o_ref[...] = _dot(x_ref[...].astype(BF16), w_ref[...]).astype(o_ref.dtype)


def _mm(x, w, tn=512, out_dtype=F32):
    m, k = x.shape
    n = w.shape[1]
    tm = min(m, 512)
    tn = min(n, tn)
    return pl.pallas_call(
        _mm_kernel,
        grid=(m // tm, n // tn),
        in_specs=[pl.BlockSpec((tm, k), lambda i, j: (i, 0)),
                  pl.BlockSpec((k, tn), lambda i, j: (0, j))],
        out_specs=pl.BlockSpec((tm, tn), lambda i, j: (i, j)),
        out_shape=jax.ShapeDtypeStruct((m, n), out_dtype),
        compiler_params=_cp("parallel", "arbitrary"),
        name="mm",
    )(x, w)


def _mm2_ln_kernel(xa_ref, xb_ref, w_ref, r_ref, g_ref, b_ref, o_ref):
    ka = xa_ref.shape[1]
    y = _dot(xa_ref[...], w_ref[0:ka, :]) + _dot(xb_ref[...], w_ref[ka:, :])
    o_ref[...] = _ln(DN_ALPHA * r_ref[...] + y, g_ref[...], b_ref[...])


def _mm2_ln(xa, xb, w, res, g, b):
    m, ka = xa.shape
    kb = xb.shape[1]
    n = w.shape[1]
    tm = min(m, 512)
    return pl.pallas_call(
        _mm2_ln_kernel,
        grid=(m // tm,),
        in_specs=[pl.BlockSpec((tm, ka), lambda i: (i, 0)),
                  pl.BlockSpec((tm, kb), lambda i: (i, 0)),
                  pl.BlockSpec((ka + kb, n), lambda i: (0, 0)),
                  pl.BlockSpec((tm, n), lambda i: (i, 0)),
                  pl.BlockSpec((1, n), lambda i: (0, 0)),
                  pl.BlockSpec((1, n), lambda i: (0, 0))],
        out_specs=pl.BlockSpec((tm, n), lambda i: (i, 0)),
        out_shape=jax.ShapeDtypeStruct((m, n), F32),
        compiler_params=_cp("parallel"),
        name="mm2_ln",
    )(xa, xb, w, res, g.reshape(1, n), b.reshape(1, n))


def _mm_ln_kernel(x_ref, w_ref, r_ref, g_ref, b_ref, o_ref):
    y = _dot(x_ref[...].astype(BF16), w_ref[...])
    o_ref[...] = _ln(DN_ALPHA * r_ref[...] + y, g_ref[...], b_ref[...])


def _mm_ln(x, w, res, g, b):
    m, k = x.shape
    n = w.shape[1]
    tm = min(m, 512)
    return pl.pallas_call(
        _mm_ln_kernel,
        grid=(m // tm,),
        in_specs=[pl.BlockSpec((tm, k), lambda i: (i, 0)),
                  pl.BlockSpec((k, n), lambda i: (0, 0)),
                  pl.BlockSpec((tm, n), lambda i: (i, 0)),
                  pl.BlockSpec((1, n), lambda i: (0, 0)),
                  pl.BlockSpec((1, n), lambda i: (0, 0))],
        out_specs=pl.BlockSpec((tm, n), lambda i: (i, 0)),
        out_shape=jax.ShapeDtypeStruct((m, n), F32),
        compiler_params=_cp("parallel"),
        name="mm_ln",
    )(x, w, res, g.reshape(1, n), b.reshape(1, n))


def _s5_kernel(u_ref, h0_ref, bd_ref, ar_ref, ai_ref, cbd_ref, d_ref, wg_ref,
               y_ref, hT_ref, hbuf, state, *, rows, steps):
    c = pl.program_id(0)

    @pl.when(c == 0)
    def _():
        state[...] = h0_ref[...]

    u = u_ref[...]
    hbuf[...] = _dot(u.astype(BF16), bd_ref[...])
    ar = ar_ref[...]
    ai = ai_ref[...]

    def step(t, carry):
        hr, hi = carry
        r0 = pl.multiple_of(t * rows, rows)
        br = hbuf[pl.ds(r0, rows), 0:S5_FLAT]
        bi = hbuf[pl.ds(r0, rows), S5_FLAT:2 * S5_FLAT]
        nr = ar * hr - ai * hi + br
        ni = ar * hi + ai * hr + bi
        hbuf[pl.ds(r0, rows), 0:S5_FLAT] = nr
        hbuf[pl.ds(r0, rows), S5_FLAT:2 * S5_FLAT] = ni
        return nr, ni

    hr, hi = lax.fori_loop(0, steps, step, (state[:, 0:S5_FLAT], state[:, S5_FLAT:2 * S5_FLAT]))
    state[:, 0:S5_FLAT] = hr
    state[:, S5_FLAT:2 * S5_FLAT] = hi

    y = _dot(hbuf[...].astype(BF16), cbd_ref[...]) + d_ref[...] * u
    g = jax.nn.gelu(y)
    y_ref[...] = (g * jax.nn.sigmoid(_dot(g.astype(BF16), wg_ref[...]))).astype(y_ref.dtype)

    @pl.when(c == pl.num_programs(0) - 1)
    def _():
        hT_ref[...] = state[...]


def _s5(u_tm, h0, prm, rows, steps):
    n = u_tm.shape[0]
    blk = rows * steps
    bd, ar, ai, cbd, d, wg = prm
    const = lambda shape: pl.BlockSpec(shape, lambda c: (0, 0))
    return pl.pallas_call(
        functools.partial(_s5_kernel, rows=rows, steps=steps),
        grid=(n // blk,),
        in_specs=[pl.BlockSpec((blk, S5_WIDTH), lambda c: (c, 0)),
                  const((rows, 2 * S5_FLAT)),
                  const((S5_WIDTH, 2 * S5_FLAT)),
                  const((1, S5_FLAT)), const((1, S5_FLAT)),
                  const((2 * S5_FLAT, S5_WIDTH)),
                  const((1, S5_WIDTH)),
                  const((S5_WIDTH, S5_WIDTH))],
        out_specs=[pl.BlockSpec((blk, S5_WIDTH), lambda c: (c, 0)),
                   const((rows, 2 * S5_FLAT))],
        out_shape=[jax.ShapeDtypeStruct((n, S5_WIDTH), BF16),
                   jax.ShapeDtypeStruct((rows, 2 * S5_FLAT), F32)],
        scratch_shapes=[pltpu.VMEM((blk, 2 * S5_FLAT), F32),
                        pltpu.VMEM((rows, 2 * S5_FLAT), F32)],
        compiler_params=_cp("arbitrary"),
        name="s5",
    )(u_tm, h0, bd, ar, ai, cbd, d, wg)


def _s5_params(lam_re, lam_im, log_dt, b_re, b_im, c_re, c_im, d_skip, w_glu):
    dt = jnp.exp(log_dt)[:, None]
    mag = jnp.exp(lam_re * dt)
    ab_re, ab_im = mag * jnp.cos(lam_im * dt), mag * jnp.sin(lam_im * dt)
    den = lam_re * lam_re + lam_im * lam_im
    nr, ni = ab_re - 1.0, ab_im
    coef_re = ((nr * lam_re + ni * lam_im) / den)[..., None]
    coef_im = ((ni * lam_re - nr * lam_im) / den)[..., None]
    bb_re = coef_re * b_re - coef_im * b_im
    bb_im = coef_re * b_im + coef_im * b_re
    eye = jnp.eye(S5_GROUPS, dtype=F32)
    pack_in = lambda bb: jnp.einsum('gpc,gh->gchp', bb, eye).reshape(S5_WIDTH, S5_FLAT)
    bd = jnp.concatenate([pack_in(bb_re), pack_in(bb_im)], axis=1).astype(BF16)
    pack_out = lambda cc: jnp.einsum('gcp,gh->gphc', cc, eye).reshape(S5_FLAT, S5_WIDTH)
    cbd = jnp.concatenate([pack_out(c_re), -pack_out(c_im)], axis=0).astype(BF16)
    return (bd, ab_re.reshape(1, S5_FLAT), ab_im.reshape(1, S5_FLAT), cbd,
            d_skip.reshape(1, S5_WIDTH), w_glu.astype(BF16))


def _swa_kernel(sink_ref, q_ref, kp_ref, kc_ref, vp_ref, vc_ref, o_ref):
    n = pl.program_id(1)
    q = q_ref[0].astype(BF16)
    k = jnp.concatenate([kp_ref[0], kc_ref[0]], axis=0).astype(BF16)
    v = jnp.concatenate([vp_ref[0], vc_ref[0]], axis=0).astype(BF16)
    qi = lax.broadcasted_iota(jnp.int32, (WINDOW, 2 * WINDOW), 0)
    kj = lax.broadcasted_iota(jnp.int32, (WINDOW, 2 * WINDOW), 1)
    dist = WINDOW + qi - kj
    valid = (dist >= 0) & (dist < WINDOW) & ((n > 0) | (kj >= WINDOW))
    distf = dist.astype(F32)
    outs = []
    for h in range(SWA_HEADS):
        g = h // SWA_REP
        qh = q[:, h * SWA_HEAD_DIM:(h + 1) * SWA_HEAD_DIM]
        kg = k[:, g * SWA_HEAD_DIM:(g + 1) * SWA_HEAD_DIM]
        vg = v[:, g * SWA_HEAD_DIM:(g + 1) * SWA_HEAD_DIM]
        s = _dot_nt(qh, kg) * (SWA_HEAD_DIM ** -0.5)
        s = s - (2.0 ** -(h + 1)) * distf
        s = jnp.where(valid, s, NEG_INF)
        sink = sink_ref[h]
        m = jnp.maximum(jnp.max(s, axis=-1, keepdims=True), sink)
        p = jnp.exp(s - m)
        p = p / (jnp.sum(p, axis=-1, keepdims=True) + jnp.exp(sink - m))
        outs.append(_dot(p.astype(BF16), vg))
    o_ref[0] = jnp.concatenate(outs, axis=-1).astype(o_ref.dtype)


def _swa_prompt(proj, sinks):
    bsz, seq, _ = proj.shape
    kcol = (S5_WIDTH + SWA_WIDTH) // SWA_KV_WIDTH
    prev = lambda b, n: (b, jnp.maximum(n - 1, 0), kcol)
    prev_v = lambda b, n: (b, jnp.maximum(n - 1, 0), kcol + 1)
    blk = (1, WINDOW, SWA_KV_WIDTH)
    return pl.pallas_call(
        _swa_kernel,
        grid=(bsz, seq // WINDOW),
        in_specs=[pl.BlockSpec(memory_space=pltpu.SMEM),
                  pl.BlockSpec((1, WINDOW, SWA_WIDTH), lambda b, n: (b, n, 1)),
                  pl.BlockSpec(blk, prev),
                  pl.BlockSpec(blk, lambda b, n: (b, n, kcol)),
                  pl.BlockSpec(blk, prev_v),
                  pl.BlockSpec(blk, lambda b, n: (b, n, kcol + 1))],
        out_specs=pl.BlockSpec((1, WINDOW, SWA_WIDTH), lambda b, n: (b, n, 0)),
        out_shape=jax.ShapeDtypeStruct((bsz, seq, SWA_WIDTH), BF16),
        compiler_params=_cp("parallel", "arbitrary"),
        name="swa_prompt",
    )(sinks, proj, proj, proj, proj, proj)


def _per_head_column(values):
    r = lax.broadcasted_iota(jnp.int32, (SWA_REP, 1), 0)
    col = jnp.full((SWA_REP, 1), values[-1], F32)
    for i in range(SWA_REP - 2, -1, -1):
        col = jnp.where(r == i, values[i], col)
    return col


def _swa_step_kernel(sink_ref, p_ref, kc_ref, vc_ref, o_ref, *, nseq, start):
    kj = lax.broadcasted_iota(jnp.int32, (SWA_REP, WINDOW), 1)
    dist = WINDOW - kj
    valid = (dist < WINDOW) & (start - WINDOW + kj >= 0)
    distf = dist.astype(F32)
    for i in range(nseq):
        row = p_ref[i]
        kn = row[:, S5_WIDTH + SWA_WIDTH:S5_WIDTH + SWA_WIDTH + SWA_KV_WIDTH].astype(BF16)
        vn = row[:, S5_WIDTH + SWA_WIDTH + SWA_KV_WIDTH:].astype(BF16)
        kc = kc_ref[i].astype(BF16)
        vc = vc_ref[i].astype(BF16)
        outs = []
        for g in range(SWA_KV_HEADS):
            q4 = jnp.concatenate(
                [row[:, S5_WIDTH + (g * SWA_REP + r) * SWA_HEAD_DIM:S5_WIDTH + (g * SWA_REP + r + 1) * SWA_HEAD_DIM]
                 for r in range(SWA_REP)], axis=0).astype(BF16)
            sl = slice(g * SWA_HEAD_DIM, (g + 1) * SWA_HEAD_DIM)
            slope = _per_head_column([2.0 ** -(g * SWA_REP + r + 1) for r in range(SWA_REP)])
            sink = _per_head_column([sink_ref[g * SWA_REP + r] for r in range(SWA_REP)])
            scale = SWA_HEAD_DIM ** -0.5
            sc = _dot_nt(q4, kc[:, sl]) * scale - slope * distf
            sc = jnp.where(valid, sc, NEG_INF)
            sn = jnp.sum(q4.astype(F32) * kn[:, sl].astype(F32), axis=-1, keepdims=True) * scale
            m = jnp.maximum(jnp.maximum(jnp.max(sc, axis=-1, keepdims=True), sn), sink)
            pc = jnp.exp(sc - m)
            pn = jnp.exp(sn - m)
            den = jnp.sum(pc, axis=-1, keepdims=True) + pn + jnp.exp(sink - m)
            o4 = _dot((pc / den).astype(BF16), vc[:, sl])
            o4 = o4 + (pn / den).astype(BF16).astype(F32) * vn[:, sl].astype(F32)
            outs.extend([o4[r:r + 1, :] for r in range(SWA_REP)])
        o_ref[i] = jnp.concatenate(outs, axis=-1).astype(o_ref.dtype)


def _swa_step(proj, k_cache, v_cache, sinks, start):
    n = proj.shape[0]
    nseq = min(n, SUBLANES)
    out = pl.pallas_call(
        functools.partial(_swa_step_kernel, nseq=nseq, start=start),
        grid=(n // nseq,),
        in_specs=[pl.BlockSpec(memory_space=pltpu.SMEM),
                  pl.BlockSpec((nseq, 1, proj.shape[1]), lambda i: (i, 0, 0)),
                  pl.BlockSpec((nseq, WINDOW, SWA_KV_WIDTH), lambda i: (i, 0, 0)),
                  pl.BlockSpec((nseq, WINDOW, SWA_KV_WIDTH), lambda i: (i, 0, 0))],
        out_specs=pl.BlockSpec((nseq, 1, SWA_WIDTH), lambda i: (i, 0, 0)),
        out_shape=jax.ShapeDtypeStruct((n, 1, SWA_WIDTH), BF16),
        compiler_params=_cp("parallel"),
        name="swa_step",
    )(sinks, proj.reshape(n, 1, -1), k_cache, v_cache)
    return out.reshape(n, SWA_WIDTH)


def _head_expand(width):
    r = lax.broadcasted_iota(jnp.int32, (LANES, SSD_HEADS * width), 0)
    c = lax.broadcasted_iota(jnp.int32, (LANES, SSD_HEADS * width), 1)
    return jnp.where((c >= r * width) & (c < (r + 1) * width), 1.0, 0.0).astype(BF16)


def _ssd_kernel(x_ref, wz_ref, wx_ref, wdt_ref, cw_ref, cb_ref, dtb_ref, a_ref, d_ref, nw_ref,
                wo_ref, g_ref, b_ref, o_ref, st_ref, conv_ref, tail, state, ybuf):
    c = pl.program_id(1)
    q = SSD_CHUNK

    @pl.when(c == 0)
    def _():
        tail[...] = jnp.zeros_like(tail)
        state[...] = jnp.zeros_like(state)

    xin = x_ref[0]
    xb = xin.astype(BF16)
    cur = _dot(xb, wx_ref[...])
    tl = tail[...]
    row8 = lax.broadcasted_iota(jnp.int32, (SUBLANES, SSD_CONV_DIM), 0)
    acc = None
    for tap in range(SSD_CONV):
        sh = SSD_CONV - 1 - tap
        if sh == 0:
            term = cur
        else:
            rolled = pltpu.roll(cur, sh, 0)
            head = jnp.where(row8 < sh, pltpu.roll(tl, sh, 0), rolled[0:SUBLANES])
            term = jnp.concatenate([head, rolled[SUBLANES:]], axis=0)
        term = term * cw_ref[tap:tap + 1, :]
        acc = term if acc is None else acc + term
    tail[...] = cur[q - SUBLANES:q]
    xbc = _silu(acc + cb_ref[...])
    xs = xbc[:, 0:SSD_INNER]
    bm = xbc[:, SSD_INNER:SSD_INNER + SSD_GROUPS * SSD_STATE].astype(BF16)
    cm = xbc[:, SSD_INNER + SSD_GROUPS * SSD_STATE:].astype(BF16)

    dt = _softplus(_dot(xb, wdt_ref[...]) + dtb_ref[...])
    da = dt * a_ref[...]
    ri = lax.broadcasted_iota(jnp.int32, (q, q), 0)
    ci = lax.broadcasted_iota(jnp.int32, (q, q), 1)
    causal = ri >= ci
    tri = jnp.where(causal, 1.0, 0.0).astype(BF16)
    da1, da2, da3 = _split3(da)
    acs = _dot(tri, da1) + _dot(tri, da2) + _dot(tri, da3)
    acs_t = acs.T
    e64 = _head_expand(SSD_HEAD_DIM)
    dt_x = _dot_exact01(dt, e64)
    acs_x = _dot_exact01(acs, e64)
    last_x = acs_x[q - 1:q, :]
    dx = dt_x * xs
    dsx = (jnp.exp(last_x - acs_x) * dx).astype(BF16)
    eacs_x = jnp.exp(acs_x)
    dxb = dx.astype(BF16)

    for g in range(SSD_GROUPS):
        ns = slice(g * SSD_STATE, (g + 1) * SSD_STATE)
        gs = slice(g * SSD_GW, (g + 1) * SSD_GW)
        cb = _dot_nt(cm[:, ns], bm[:, ns])
        for r in range(SSD_HPG):
            h = g * SSD_HPG + r
            seg = acs[:, h:h + 1] - acs_t[h:h + 1, :]
            decay = jnp.where(causal, jnp.exp(jnp.where(causal, seg, 0.0)), 0.0)
            hs = slice(h * SSD_HEAD_DIM, (h + 1) * SSD_HEAD_DIM)
            ybuf[:, hs] = _dot((cb * decay).astype(BF16), dxb[:, hs])
        s_in = state[g]
        y_off = _dot(cm[:, ns], s_in.astype(BF16)) * eacs_x[:, gs]
        ybuf[:, gs] = ybuf[:, gs] + y_off
        state[g] = jnp.exp(last_x[:, gs]) * s_in + _dot_tn(bm[:, ns], dsx[:, gs])

    y = ybuf[...] + d_ref[...] * xs
    y = y * _silu(_dot(xb, wz_ref[...]))
    y = y * lax.rsqrt(jnp.mean(y * y, axis=-1, keepdims=True) + RMS_EPS) * nw_ref[...]
    mix = _dot(y.astype(BF16), wo_ref[...])
    o_ref[0] = _ln(DN_ALPHA * xin + mix, g_ref[...], b_ref[...])

    @pl.when(c == pl.num_programs(1) - 1)
    def _():
        conv_ref[0] = cur[q - SUBLANES:q]
        for g in range(SSD_GROUPS):
            st_ref[0, g * SSD_GW:(g + 1) * SSD_GW, :] = state[g].T


def _ssd_prompt(x, w3, prm, w_out, g, b):
    bsz, seq, _ = x.shape
    w_z, w_x, w_dt = w3
    cw, cb, dtb, a, d_x, nw = prm
    const = lambda shape: pl.BlockSpec(shape, lambda b, c: (0, 0))
    seqblk = lambda w: pl.BlockSpec((1, SSD_CHUNK, w), lambda b, c: (b, c, 0))
    return pl.pallas_call(
        _ssd_kernel,
        grid=(bsz, seq // SSD_CHUNK),
        in_specs=[seqblk(D_MODEL),
                  const((D_MODEL, SSD_INNER)), const((D_MODEL, SSD_CONV_DIM)), const((D_MODEL, LANES)),
                  const((SSD_CONV, SSD_CONV_DIM)), const((1, SSD_CONV_DIM)),
                  const((1, LANES)), const((1, LANES)),
                  const((1, SSD_INNER)), const((1, SSD_INNER)),
                  const((SSD_INNER, D_MODEL)), const((1, D_MODEL)), const((1, D_MODEL))],
        out_specs=[seqblk(D_MODEL),
                   pl.BlockSpec((1, SSD_INNER, SSD_STATE), lambda b, c: (b, 0, 0)),
                   pl.BlockSpec((1, SUBLANES, SSD_CONV_DIM), lambda b, c: (b, 0, 0))],
        out_shape=[jax.ShapeDtypeStruct((bsz, seq, D_MODEL), F32),
                   jax.ShapeDtypeStruct((bsz, SSD_INNER, SSD_STATE), F32),
                   jax.ShapeDtypeStruct((bsz, SUBLANES, SSD_CONV_DIM), F32)],
        scratch_shapes=[pltpu.VMEM((SUBLANES, SSD_CONV_DIM), F32),
                        pltpu.VMEM((SSD_GROUPS, SSD_STATE, SSD_GW), F32),
                        pltpu.VMEM((SSD_CHUNK, SSD_INNER), F32)],
        compiler_params=_cp("parallel", "arbitrary"),
        name="ssd_prompt",
    )(x, w_z, w_x, w_dt, cw, cb, dtb, a, d_x, nw, w_out, g.reshape(1, -1), b.reshape(1, -1))


def _pad_rows(row):
    return jnp.concatenate([row, jnp.zeros((SUBLANES - 1, row.shape[1]), row.dtype)], axis=0)


def _ssd_step_kernel(z_ref, x_ref, dt_ref, buf_ref, s0_ref, cw_ref, cb_ref, dtb_ref, a_ref, d_ref, nw_ref,
                     y_ref, s1_ref, *, nseq):
    e64 = _head_expand(SSD_HEAD_DIM)
    ones3 = jnp.where(lax.broadcasted_iota(jnp.int32, (SUBLANES, SSD_STATE), 0) < 3, 1.0, 0.0).astype(BF16)
    for i in range(nseq):
        buf = buf_ref[i]
        acc = buf[0:1] * cw_ref[0:1, :]
        acc = acc + buf[1:2] * cw_ref[1:2, :]
        acc = acc + buf[2:3] * cw_ref[2:3, :]
        acc = acc + x_ref[i] * cw_ref[3:4, :]
        xbc = _silu(acc + cb_ref[...])
        xs = xbc[:, 0:SSD_INNER]
        bm = xbc[:, SSD_INNER:SSD_INNER + SSD_GROUPS * SSD_STATE].astype(BF16)
        cm = xbc[:, SSD_INNER + SSD_GROUPS * SSD_STATE:].astype(BF16)
        dt = _softplus(dt_ref[i] + dtb_ref[...])
        ed = jnp.exp(dt * a_ref[...])
        dt_x = _dot_exact01(_pad_rows(dt), e64)[0:1]
        ed_x = _dot_exact01(_pad_rows(ed), e64)[0:1]
        dx = dt_x * xs
        dxb = dx.astype(BF16)
        ys = []
        for g in range(SSD_GROUPS):
            ns = slice(g * SSD_STATE, (g + 1) * SSD_STATE)
            gs = slice(g * SSD_GW, (g + 1) * SSD_GW)
            s0 = s0_ref[i, gs, :]
            cmg = _pad_rows(cm[:, ns])
            cbv = jnp.sum(cm[:, ns].astype(F32) * bm[:, ns].astype(F32), axis=-1, keepdims=True)
            y_diag = cbv.astype(BF16).astype(F32) * dxb[:, gs].astype(F32)
            y_off = _dot_nt(cmg, s0.astype(BF16))[0:1] * ed_x[:, gs]
            ys.append(y_diag + y_off)
            e1, e2, e3 = _split3(ed_x[:, gs])
            ed_rows = jnp.concatenate([e1, e2, e3, jnp.zeros((SUBLANES - 3, SSD_GW), BF16)], axis=0)
            ed_full = _dot_tn(ed_rows, ones3)
            outer = _dot_tn(_pad_rows(dxb[:, gs]), _pad_rows(bm[:, ns]))
            s1_ref[i, gs, :] = ed_full * s0 + outer
        y = jnp.concatenate(ys, axis=-1) + d_ref[...] * xs
        y = y * _silu(z_ref[i])
        y = y * lax.rsqrt(jnp.mean(y * y, axis=-1, keepdims=True) + RMS_EPS) * nw_ref[...]
        y_ref[i] = y.astype(y_ref.dtype)


def _ssd_step(z, xbc, dtraw, conv_buf, s0_all, layer, prm):
    n = z.shape[0]
    nseq = min(n, 4)
    cw, cb, dtb, a, d_x, nw = prm
    const = lambda shape: pl.BlockSpec(shape, lambda i: (0, 0))
    tok = lambda w: pl.BlockSpec((nseq, 1, w), lambda i: (i, 0, 0))
    y, s1 = pl.pallas_call(
        functools.partial(_ssd_step_kernel, nseq=nseq),
        grid=(n // nseq,),
        in_specs=[tok(SSD_INNER), tok(SSD_CONV_DIM), tok(LANES),
                  pl.BlockSpec((nseq, SSD_CONV - 1, SSD_CONV_DIM), lambda i: (i, 0, 0)),
                  pl.BlockSpec((None, nseq, SSD_INNER, SSD_STATE), lambda i: (layer, i, 0, 0)),
                  const((SSD_CONV, SSD_CONV_DIM)), const((1, SSD_CONV_DIM)),
                  const((1, LANES)), const((1, LANES)),
                  const((1, SSD_INNER)), const((1, SSD_INNER))],
        out_specs=[tok(SSD_INNER),
                   pl.BlockSpec((nseq, SSD_INNER, SSD_STATE), lambda i: (i, 0, 0))],
        out_shape=[jax.ShapeDtypeStruct((n, 1, SSD_INNER), BF16),
                   jax.ShapeDtypeStruct((n, SSD_INNER, SSD_STATE), F32)],
        compiler_params=_cp("parallel"),
        name="ssd_step",
    )(z.reshape(n, 1, -1), xbc.reshape(n, 1, -1), dtraw.reshape(n, 1, -1), conv_buf, s0_all,
      cw, cb, dtb, a, d_x, nw)
    return y.reshape(n, SSD_INNER), s1


def _ssd_params(conv_w, conv_b, dt_bias, a_log, d_skip, norm_w):
    pad = lambda v: jnp.pad(v, (0, LANES - SSD_HEADS)).reshape(1, LANES)
    return (conv_w, conv_b.reshape(1, -1), pad(dt_bias), pad(-jnp.exp(a_log)),
            jnp.repeat(d_skip, SSD_HEAD_DIM).reshape(1, SSD_INNER), norm_w.reshape(1, SSD_INNER))


PEER_TE = 1024
PEER_ROWS = PEER_TE // PEER_KEYS


PEER_CHUNK_ROWS = 2
PEER_TOK_CHUNK = 256
PACK = 2 * SUBLANES


def _batcher_sort_pairs(n):
    def merge(lo, hi, r):
        step = r * 2
        if step < hi - lo:
            yield from merge(lo, hi, step)
            yield from merge(lo + r, hi, step)
            for i in range(lo + r, hi - r, step):
                yield (i, i + r)
        else:
            yield (lo, lo + r)

    def sort(lo, hi):
        if hi - lo >= 1:
            mid = lo + (hi - lo) // 2
            yield from sort(lo, mid)
            yield from sort(mid + 1, hi)
            yield from merge(lo, hi, 1)

    return tuple(sort(0, n - 1))


_SORT16 = _batcher_sort_pairs(PEER_TOPK)
_BITONIC16 = tuple((i, i + d) for d in (8, 4, 2, 1) for i in range(PEER_TOPK) if not i & d)


def _mx(a, b):
    return b if a is None else a if b is None else jnp.maximum(a, b)


def _mn(a, b):
    return None if a is None or b is None else jnp.minimum(a, b)


def _top16(rows):
    v = list(rows) + [None] * (PEER_TOPK - len(rows))
    for i, j in _SORT16:
        v[i], v[j] = _mx(v[i], v[j]), _mn(v[i], v[j])
    for shift in (4, 2, 1):
        w = [None if x is None else pltpu.roll(x, shift, 0) for x in v]
        v = [_mx(v[k], w[PEER_TOPK - 1 - k]) for k in range(PEER_TOPK)]
        for i, j in _BITONIC16:
            v[i], v[j] = _mx(v[i], v[j]), _mn(v[i], v[j])
    return v


def _prefix_count(test, thr):
    m8 = test(thr[7])
    m4 = test(jnp.where(m8, thr[11], thr[3]))
    m2 = test(jnp.where(m8, jnp.where(m4, thr[13], thr[9]), jnp.where(m4, thr[5], thr[1])))
    hi = jnp.where(m4, jnp.where(m2, thr[14], thr[12]), jnp.where(m2, thr[10], thr[8]))
    lo = jnp.where(m4, jnp.where(m2, thr[6], thr[4]), jnp.where(m2, thr[2], thr[0]))
    m1 = test(jnp.where(m8, hi, lo))
    return (jnp.where(m8, 8.0, 0.0) + jnp.where(m4, 4.0, 0.0)) + (jnp.where(m2, 2.0, 0.0) + jnp.where(m1, 1.0, 0.0))


def _peer_route(xb, wq_ref, k1_ref, k2_ref, c_ref, e1_ref, rk_ref, e2_ref):
    tm = xb.shape[0]
    q = _dot(xb, wq_ref[...]).astype(BF16)
    half = PEER_DK // 2
    nblk = PEER_KEYS // SUBLANES
    sub = lax.broadcasted_iota(jnp.int32, (SUBLANES, tm), 0)

    def by_sublane(vals):
        out = vals[SUBLANES - 1]
        for s in range(SUBLANES - 2, -1, -1):
            out = jnp.where(sub == s, vals[s], out)
        return out

    for h in range(PEER_HEADS):
        s1 = _dot_nt(k1_ref[h], q[:, h * PEER_DK:h * PEER_DK + half])
        s2 = _dot_nt(k2_ref[h], q[:, h * PEER_DK + half:(h + 1) * PEER_DK])
        r1 = [s1[k * SUBLANES:(k + 1) * SUBLANES] for k in range(nblk)]
        r2 = [s2[k * SUBLANES:(k + 1) * SUBLANES] for k in range(nblk)]
        a = _top16(r1)
        b = _top16(r2)
        r1 = [x - a[0] for x in r1]
        r2 = [x - b[0] for x in r2]
        a = [x - a[0] for x in a]
        b = [x - b[0] for x in b]
        b_lo, b_hi, a_hi = by_sublane(b[0:8]), by_sublane(b[8:16]), by_sublane(a[8:16])
        cand = [a[0] + b_lo, a[0] + b_hi] + [a[i] + b_lo for i in range(1, 8)] + [a_hi + b[0]]
        top = _top16(cand)
        tau = top[PEER_TOPK - 1]
        zsum = jnp.exp(top[0])
        for k in range(1, PEER_TOPK):
            zsum = zsum + jnp.exp(top[k])
        inv_z = 1.0 / zsum
        for k in range(nblk):
            x1 = r1[k]
            cnt = _prefix_count(lambda t: (x1 + t) >= tau, b[0:15])
            cnt = cnt + jnp.where((x1 + b[15]) >= tau, 1.0, 0.0)
            c_ref[h, k, :, 0:tm] = cnt
            e1_ref[h, k, :, 0:tm] = jnp.exp(x1) * inv_z
        ranks, e2s = [], []
        for k in range(nblk):
            x2 = r2[k]
            rnk = _prefix_count(lambda t: t > x2, b[0:15])
            ranks.append(jnp.where(x2 >= b[15], rnk, float(PEER_TOPK)))
            e2s.append(jnp.exp(x2))
        rk_ref[h, :, 0:tm] = jnp.concatenate(ranks, axis=0).astype(BF16)
        e2_ref[h, :, 0:tm] = jnp.concatenate(e2s, axis=0).astype(BF16)


def _gelu_tanh(x):
    c0 = math.sqrt(2.0 / math.pi)
    return (0.5 * x) * (1.0 + jnp.tanh(x * (c0 + (c0 * 0.044715) * (x * x))))


def _peer_kernel(x_ref, wq_ref, k1_ref, k2_ref, u_ref, vt_ref, g_ref, b_ref, o_ref,
                 acc, c_ref, e1_ref, rk_ref, e2_ref, act_ref):
    j = pl.program_id(1)
    tm = x_ref.shape[0]
    xb = x_ref[...].astype(BF16)

    @pl.when(j == 0)
    def _():
        acc[:, 0:tm] = jnp.zeros((D_MODEL, tm), F32)
        _peer_route(xb, wq_ref, k1_ref, k2_ref, c_ref, e1_ref, rk_ref, e2_ref)
        act_ref[1, :, 0:tm] = jnp.zeros((PEER_TE, tm), BF16)

    nb2 = PEER_KEYS // PACK
    jt = jnp.minimum(j, PEER_EXPERTS // PEER_TE - 1)
    slot = j % 2

    crows = PEER_CHUNK_ROWS
    nch = PEER_ROWS // crows
    tc = min(tm, PEER_TOK_CHUNK)
    apply_after = {nch // 2 - 1 + k: k * tc for k in range(tm // tc)}
    for ch in range(nch):
        e0 = ch * crows * PEER_KEYS
        h_t = _dot_nt(u_ref[e0:e0 + crows * PEER_KEYS, :], xb)
        if ch in apply_after:
            t0 = apply_after[ch]
            acc[:, t0:t0 + tc] += _dot(vt_ref[...], act_ref[1 - slot, :, t0:t0 + tc])
        rows = [ch * crows + ri for ri in range(crows)]
        for tl in range(tm // LANES):
            ts = slice(tl * LANES, (tl + 1) * LANES)
            gate = [[None] * nb2 for _ in range(crows)]
            for h in range(PEER_HEADS):
                cbv = [jnp.broadcast_to(c_ref[h, jt, r:r + 1, ts], (PACK, LANES)).astype(BF16) for r in rows]
                ebv = [jnp.broadcast_to(e1_ref[h, jt, r:r + 1, ts], (PACK, LANES)).astype(BF16) for r in rows]
                for b2 in range(nb2):
                    ks = slice(b2 * PACK, (b2 + 1) * PACK)
                    rk = rk_ref[h, ks, ts]
                    e2 = e2_ref[h, ks, ts]
                    for ri in range(crows):
                        w = jnp.where(rk < cbv[ri], ebv[ri] * e2, 0.0)
                        gate[ri][b2] = w if gate[ri][b2] is None else gate[ri][b2] + w
            for ri in range(crows):
                for b2 in range(nb2):
                    lo = ri * PEER_KEYS + b2 * PACK
                    hv = _gelu_tanh(h_t[lo:lo + PACK, ts].astype(BF16))
                    act_ref[slot, e0 + lo:e0 + lo + PACK, ts] = hv * gate[ri][b2]

    @pl.when(j == pl.num_programs(1) - 1)
    def _():
        o_ref[...] = _ln(DN_ALPHA * x_ref[...] + acc[:, 0:tm].T, g_ref[...], b_ref[...])


def _peer_ln(x, wq, k1, k2, u, vt, g, b):
    t = x.shape[0]
    tm = min(t, 512)
    const2 = lambda shape: pl.BlockSpec(shape, lambda i, j: (0, 0))
    const3 = lambda shape: pl.BlockSpec(shape, lambda i, j: (0, 0, 0))
    last = PEER_EXPERTS // PEER_TE - 1
    return pl.pallas_call(
        _peer_kernel,
        grid=(t // tm, last + 2),
        in_specs=[pl.BlockSpec((tm, D_MODEL), lambda i, j: (i, 0)),
                  const2((D_MODEL, PEER_HEADS * PEER_DK)),
                  const3((PEER_HEADS, PEER_KEYS, PEER_DK // 2)),
                  const3((PEER_HEADS, PEER_KEYS, PEER_DK // 2)),
                  pl.BlockSpec((PEER_TE, D_MODEL), lambda i, j: (jnp.minimum(j, last), 0)),
                  pl.BlockSpec((D_MODEL, PEER_TE), lambda i, j: (0, jnp.maximum(j - 1, 0))),
                  const2((1, D_MODEL)), const2((1, D_MODEL))],
        out_specs=pl.BlockSpec((tm, D_MODEL), lambda i, j: (i, 0)),
        out_shape=jax.ShapeDtypeStruct((t, D_MODEL), F32),
        scratch_shapes=[pltpu.VMEM((D_MODEL, tm), F32),
                        pltpu.VMEM((PEER_HEADS, PEER_KEYS // PEER_ROWS, PEER_ROWS, tm), F32),
                        pltpu.VMEM((PEER_HEADS, PEER_KEYS // PEER_ROWS, PEER_ROWS, tm), F32),
                        pltpu.VMEM((PEER_HEADS, PEER_KEYS, tm), BF16),
                        pltpu.VMEM((PEER_HEADS, PEER_KEYS, tm), BF16),
                        pltpu.VMEM((2, PEER_TE, tm), BF16)],
        compiler_params=_cp("parallel", "arbitrary"),
        name="peer",
    )(x, wq, k1, k2, u, vt, g.reshape(1, -1), b.reshape(1, -1))


def _even_prompt(x, bsz, seq, w_in, s5p, sinks, w_out, g, b):
    proj = _mm(x, w_in, tn=640)
    p3 = proj.reshape(bsz, seq, -1)
    u_tm = p3[:, :, :S5_WIDTH].transpose(1, 0, 2).reshape(seq * bsz, S5_WIDTH)
    steps = min(seq, 64)
    y_a, h_t = _s5(u_tm, jnp.zeros((bsz, 2 * S5_FLAT), F32), s5p, bsz, steps)
    y_a = y_a.reshape(seq, bsz, S5_WIDTH).transpose(1, 0, 2).reshape(bsz * seq, S5_WIDTH)
    y_b = _swa_prompt(p3, sinks).reshape(bsz * seq, SWA_WIDTH)
    x = _mm2_ln(y_a, y_b, w_out, x, g, b)
    kv = p3[:, seq - WINDOW:, S5_WIDTH + SWA_WIDTH:]
    new_k = kv[..., :SWA_KV_WIDTH].reshape(bsz, WINDOW, SWA_KV_HEADS, SWA_HEAD_DIM)
    new_v = kv[..., SWA_KV_WIDTH:].reshape(bsz, WINDOW, SWA_KV_HEADS, SWA_HEAD_DIM)
    h_re = h_t[:, :S5_FLAT].reshape(bsz, S5_GROUPS, S5_STATE)
    h_im = h_t[:, S5_FLAT:].reshape(bsz, S5_GROUPS, S5_STATE)
    return x, h_re, h_im, new_k, new_v


def _even_sample(x, h_re, h_im, k_buf, v_buf, start, w_in, s5p, sinks, w_out, g, b):
    n = x.shape[0]
    proj = _mm(x, w_in, tn=640)
    h0 = jnp.concatenate([h_re.reshape(n, S5_FLAT), h_im.reshape(n, S5_FLAT)], axis=1)
    y_a, h_t = _s5(proj[:, :S5_WIDTH], h0, s5p, n, 1)
    kc = k_buf.reshape(n, WINDOW, SWA_KV_WIDTH)
    vc = v_buf.reshape(n, WINDOW, SWA_KV_WIDTH)
    y_b = _swa_step(proj, kc, vc, sinks, start)
    x = _mm2_ln(y_a, y_b, w_out, x, g, b)
    kn = proj[:, None, S5_WIDTH + SWA_WIDTH:S5_WIDTH + SWA_WIDTH + SWA_KV_WIDTH]
    vn = proj[:, None, S5_WIDTH + SWA_WIDTH + SWA_KV_WIDTH:]
    new_k = jnp.concatenate([kc[:, 1:], kn], axis=1).reshape(k_buf.shape)
    new_v = jnp.concatenate([vc[:, 1:], vn], axis=1).reshape(v_buf.shape)
    return (x, h_t[:, :S5_FLAT].reshape(h_re.shape), h_t[:, S5_FLAT:].reshape(h_im.shape), new_k, new_v)


def _odd_weights(w_in):
    w_z = w_in[:, :SSD_INNER].astype(BF16)
    w_x = w_in[:, SSD_INNER:SSD_INNER + SSD_CONV_DIM].astype(BF16)
    w_dt = jnp.pad(w_in[:, SSD_INNER + SSD_CONV_DIM:], ((0, 0), (0, LANES - SSD_HEADS))).astype(BF16)
    return w_z, w_x, w_dt


def _odd_prompt(x, bsz, seq, w3, ssdp, w_out, g, b):
    x, st, tail = _ssd_prompt(x.reshape(bsz, seq, -1), w3, ssdp, w_out, g, b)
    return (x.reshape(bsz * seq, -1), st.reshape(bsz, SSD_HEADS, SSD_HEAD_DIM, SSD_STATE),
            tail[:, SUBLANES - (SSD_CONV - 1):])


def _odd_sample(x, ssm_all, layer, conv, w3, ssdp, w_out, g, b):
    n = x.shape[0]
    w_z, w_x, w_dt = w3
    z = _mm(x, w_z)
    xbc = _mm(x, w_x)
    dtraw = _mm(x, w_dt)
    s0_all = ssm_all.reshape(ssm_all.shape[0], n, SSD_INNER, SSD_STATE)
    y, st = _ssd_step(z, xbc, dtraw, conv, s0_all, layer, ssdp)
    x = _mm_ln(y, w_out, x, g, b)
    new_conv = jnp.concatenate([conv[:, 1:], xbc[:, None, :]], axis=1)
    return x, st.reshape(ssm_all.shape[1:]), new_conv


def kernel(x_prompt, x_sample, state_s5_re, state_s5_im, cache_swa_k, cache_swa_v, state_ssd, state_conv,
           w_in_even, s5_lambda_re, s5_lambda_im, s5_log_dt, s5_b_re, s5_b_im, s5_c_re, s5_c_im, s5_d,
           s5_w_glu, swa_sinks, w_out_even,
           w_in_odd, ssd_conv_w, ssd_conv_b, ssd_dt_bias, ssd_a_log, ssd_d, ssd_norm_w, w_out_odd,
           ln1_g, ln1_b, ln2_g, ln2_b, peer_w_q, peer_k1, peer_k2, peer_u, peer_v):
    bsz, seq, _ = x_prompt.shape
    nsm = x_sample.shape[0]
    xp = x_prompt.reshape(bsz * seq, D_MODEL)
    xs = x_sample.reshape(nsm, D_MODEL)
    p_out = [[] for _ in range(6)]
    s_out = [[] for _ in range(6)]
    for layer in range(DEPTH):
        i = layer // 2
        if layer % 2 == 0:
            w_in = w_in_even[i].astype(BF16)
            w_out = w_out_even[i].astype(BF16)
            s5p = _s5_params(s5_lambda_re[i], s5_lambda_im[i], s5_log_dt[i], s5_b_re[i], s5_b_im[i],
                             s5_c_re[i], s5_c_im[i], s5_d[i], s5_w_glu[i])
            xp, hr, hi, nk, nv = _even_prompt(xp, bsz, seq, w_in, s5p, swa_sinks[i], w_out,
                                              ln1_g[layer], ln1_b[layer])
            for lst, val in zip(p_out[:4], (hr, hi, nk, nv)):
                lst.append(val)
            xs, hr, hi, nk, nv = _even_sample(xs, state_s5_re[i], state_s5_im[i], cache_swa_k[i], cache_swa_v[i],
                                              PAST_LEN, w_in, s5p, swa_sinks[i], w_out,
                                              ln1_g[layer], ln1_b[layer])
            for lst, val in zip(s_out[:4], (hr, hi, nk, nv)):
                lst.append(val)
        else:
            w3 = _odd_weights(w_in_odd[i])
            w_out = w_out_odd[i].astype(BF16)
            ssdp = _ssd_params(ssd_conv_w[i], ssd_conv_b[i], ssd_dt_bias[i], ssd_a_log[i], ssd_d[i],
                               ssd_norm_w[i])
            xp, st, cb = _odd_prompt(xp, bsz, seq, w3, ssdp, w_out, ln1_g[layer], ln1_b[layer])
            p_out[4].append(st)
            p_out[5].append(cb)
            xs, st, cb = _odd_sample(xs, state_ssd, i, state_conv[i], w3, ssdp, w_out,
                                     ln1_g[layer], ln1_b[layer])
            s_out[4].append(st)
            s_out[5].append(cb)
        wq = peer_w_q[layer].astype(BF16)
        k1 = peer_k1[layer].astype(BF16)
        k2 = peer_k2[layer].astype(BF16)
        u = peer_u[layer].astype(BF16)
        vt = peer_v[layer].astype(BF16).T
        xp = _peer_ln(xp, wq, k1, k2, u, vt, ln2_g[layer], ln2_b[layer])
        xs = _peer_ln(xs, wq, k1, k2, u, vt, ln2_g[layer], ln2_b[layer])
    stack = lambda lists: tuple(jnp.stack(l) for l in lists)
    return ((xp.reshape(bsz, seq, D_MODEL), xs.reshape(nsm, 1, D_MODEL)) + stack(p_out) + stack(s_out))
```

```python
import functools
import math

import jax
import jax.numpy as jnp
from jax import lax
from jax.experimental import pallas as pl
from jax.experimental.pallas import tpu as pltpu

F32 = jnp.float32
BF16 = jnp.bfloat16

D_MODEL = 1024
DEPTH = 4
PAST_LEN = 8192
S5_WIDTH = 512
S5_GROUP = 16
S5_GROUPS = 32
S5_STATE = 64
S5_FLAT = S5_GROUPS * S5_STATE
SWA_HEADS = 8
SWA_KV_HEADS = 2
SWA_REP = SWA_HEADS // SWA_KV_HEADS
SWA_HEAD_DIM = 64
SWA_WIDTH = SWA_HEADS * SWA_HEAD_DIM
SWA_KV_WIDTH = SWA_KV_HEADS * SWA_HEAD_DIM
WINDOW = 128
SSD_INNER = 2048
SSD_HEAD_DIM = 64
SSD_HEADS = 32
SSD_GROUPS = 4
SSD_HPG = SSD_HEADS // SSD_GROUPS
SSD_GW = SSD_HPG * SSD_HEAD_DIM
SSD_STATE = 128
SSD_CONV = 4
SSD_CHUNK = 128
SSD_CONV_DIM = SSD_INNER + 2 * SSD_GROUPS * SSD_STATE
PEER_HEADS = 8
PEER_KEYS = 128
PEER_EXPERTS = PEER_KEYS * PEER_KEYS
PEER_DK = 256
PEER_TOPK = 16
DN_ALPHA = (2 * DEPTH) ** 0.25
LN_EPS = 1e-5
RMS_EPS = 1e-5
NEG_INF = -1e30

LANES = 128
SUBLANES = 8
MXU_TILE = 256
VMEM_LIMIT = 56 * 1024 * 1024


def _cp(*sem):
    return pltpu.CompilerParams(dimension_semantics=sem, vmem_limit_bytes=VMEM_LIMIT)


def _ln(x, g, b):
    mu = jnp.mean(x, axis=-1, keepdims=True)
    xc = x - mu
    var = jnp.mean(xc * xc, axis=-1, keepdims=True)
    return xc * lax.rsqrt(var + LN_EPS) * g + b


def _softplus(x):
    return jnp.maximum(x, 0.0) + jnp.log(1.0 + jnp.exp(-jnp.abs(x)))


def _silu(x):
    return x * jax.nn.sigmoid(x)


def _dot(a, b):
    return jnp.dot(a, b, preferred_element_type=F32)


def _dot_nt(a, b):
    return lax.dot_general(a, b, (((1,), (1,)), ((), ())), preferred_element_type=F32)


def _dot_tn(a, b):
    return lax.dot_general(a, b, (((0,), (0,)), ((), ())), preferred_element_type=F32)


def _split3(x):
    x1 = x.astype(BF16)
    r1 = x - x1.astype(F32)
    x2 = r1.astype(BF16)
    r2 = r1 - x2.astype(F32)
    return x1, x2, r2.astype(BF16)


def _dot_exact01(x, m01):
    x1, x2, x3 = _split3(x)
    return _dot(x1, m01) + _dot(x2, m01) + _dot(x3, m01)


def _mm_kernel(x_ref, w_ref, o_ref):
    o_ref[...] = _dot(x_ref[...].astype(BF16), w_ref[...]).astype(o_ref.dtype)


def _mm(x, w, tn=512, out_dtype=F32):
    m, k = x.shape
    n = w.shape[1]
    tm = min(m, 512)
    tn = min(n, tn)
    return pl.pallas_call(
        _mm_kernel,
        grid=(m // tm, n // tn),
        in_specs=[pl.BlockSpec((tm, k), lambda i, j: (i, 0)),
                  pl.BlockSpec((k, tn), lambda i, j: (0, j))],
        out_specs=pl.BlockSpec((tm, tn), lambda i, j: (i, j)),
        out_shape=jax.ShapeDtypeStruct((m, n), out_dtype),
        compiler_params=_cp("parallel", "arbitrary"),
        name="mm",
    )(x, w)


def _mm2_ln_kernel(xa_ref, xb_ref, w_ref, r_ref, g_ref, b_ref, o_ref):
    ka = xa_ref.shape[1]
    y = _dot(xa_ref[...], w_ref[0:ka, :]) + _dot(xb_ref[...], w_ref[ka:, :])
    o_ref[...] = _ln(DN_ALPHA * r_ref[...] + y, g_ref[...], b_ref[...])


def _mm2_ln(xa, xb, w, res, g, b):
    m, ka = xa.shape
    kb = xb.shape[1]
    n = w.shape[1]
    tm = min(m, 512)
    return pl.pallas_call(
        _mm2_ln_kernel,
        grid=(m // tm,),
        in_specs=[pl.BlockSpec((tm, ka), lambda i: (i, 0)),
                  pl.BlockSpec((tm, kb), lambda i: (i, 0)),
                  pl.BlockSpec((ka + kb, n), lambda i: (0, 0)),
                  pl.BlockSpec((tm, n), lambda i: (i, 0)),
                  pl.BlockSpec((1, n), lambda i: (0, 0)),
                  pl.BlockSpec((1, n), lambda i: (0, 0))],
        out_specs=pl.BlockSpec((tm, n), lambda i: (i, 0)),
        out_shape=jax.ShapeDtypeStruct((m, n), F32),
        compiler_params=_cp("parallel"),
        name="mm2_ln",
    )(xa, xb, w, res, g.reshape(1, n), b.reshape(1, n))


def _mm_ln_kernel(x_ref, w_ref, r_ref, g_ref, b_ref, o_ref):
    y = _dot(x_ref[...].astype(BF16), w_ref[...])
    o_ref[...] = _ln(DN_ALPHA * r_ref[...] + y, g_ref[...], b_ref[...])


def _mm_ln(x, w, res, g, b):
    m, k = x.shape
    n = w.shape[1]
    tm = min(m, 512)
    return pl.pallas_call(
        _mm_ln_kernel,
        grid=(m // tm,),
        in_specs=[pl.BlockSpec((tm, k), lambda i: (i, 0)),
                  pl.BlockSpec((k, n), lambda i: (0, 0)),
                  pl.BlockSpec((tm, n), lambda i: (i, 0)),
                  pl.BlockSpec((1, n), lambda i: (0, 0)),
                  pl.BlockSpec((1, n), lambda i: (0, 0))],
        out_specs=pl.BlockSpec((tm, n), lambda i: (i, 0)),
        out_shape=jax.ShapeDtypeStruct((m, n), F32),
        compiler_params=_cp("parallel"),
        name="mm_ln",
    )(x, w, res, g.reshape(1, n), b.reshape(1, n))


def _s5_kernel(u_ref, h0_ref, bd_ref, ar_ref, ai_ref, cbd_ref, d_ref, wg_ref,
               y_ref, hT_ref, hbuf, state, *, rows, steps):
    c = pl.program_id(0)

    @pl.when(c == 0)
    def _():
        state[...] = h0_ref[...]

    u = u_ref[...]
    ub = u.astype(BF16)
    for part in range(2):
        for jt in range(S5_FLAT // MXU_TILE):
            c0 = part * S5_FLAT + jt * MXU_TILE
            k0 = (jt * MXU_TILE // S5_STATE * S5_GROUP) // LANES * LANES
            hbuf[:, c0:c0 + MXU_TILE] = _dot(ub[:, k0:k0 + LANES], bd_ref[k0:k0 + LANES, c0:c0 + MXU_TILE])
    ar = ar_ref[...]
    ai = ai_ref[...]

    def step(t, carry):
        hr, hi = carry
        r0 = pl.multiple_of(t * rows, rows)
        br = hbuf[pl.ds(r0, rows), 0:S5_FLAT]
        bi = hbuf[pl.ds(r0, rows), S5_FLAT:2 * S5_FLAT]
        nr = ar * hr - ai * hi + br
        ni = ar * hi + ai * hr + bi
        hbuf[pl.ds(r0, rows), 0:S5_FLAT] = nr
        hbuf[pl.ds(r0, rows), S5_FLAT:2 * S5_FLAT] = ni
        return nr, ni

    hr, hi = lax.fori_loop(0, steps, step, (state[:, 0:S5_FLAT], state[:, S5_FLAT:2 * S5_FLAT]))
    state[:, 0:S5_FLAT] = hr
    state[:, S5_FLAT:2 * S5_FLAT] = hi

    ys = []
    for nt in range(S5_WIDTH // MXU_TILE):
        n0 = nt * MXU_TILE
        kw = MXU_TILE // S5_GROUP * S5_STATE
        acc = None
        for part in range(2):
            k0 = part * S5_FLAT + nt * kw
            term = _dot(hbuf[:, k0:k0 + kw].astype(BF16), cbd_ref[k0:k0 + kw, n0:n0 + MXU_TILE])
            acc = term if acc is None else acc + term
        ys.append(acc)
    y = jnp.concatenate(ys, axis=-1) + d_ref[...] * u
    g = jax.nn.gelu(y)
    y_ref[...] = (g * jax.nn.sigmoid(_dot(g.astype(BF16), wg_ref[...]))).astype(y_ref.dtype)

    @pl.when(c == pl.num_programs(0) - 1)
    def _():
        hT_ref[...] = state[...]


def _s5(u_tm, h0, prm, rows, steps):
    n = u_tm.shape[0]
    blk = rows * steps
    bd, ar, ai, cbd, d, wg = prm
    const = lambda shape: pl.BlockSpec(shape, lambda c: (0, 0))
    return pl.pallas_call(
        functools.partial(_s5_kernel, rows=rows, steps=steps),
        grid=(n // blk,),
        in_specs=[pl.BlockSpec((blk, S5_WIDTH), lambda c: (c, 0)),
                  const((rows, 2 * S5_FLAT)),
                  const((S5_WIDTH, 2 * S5_FLAT)),
                  const((1, S5_FLAT)), const((1, S5_FLAT)),
                  const((2 * S5_FLAT, S5_WIDTH)),
                  const((1, S5_WIDTH)),
                  const((S5_WIDTH, S5_WIDTH))],
        out_specs=[pl.BlockSpec((blk, S5_WIDTH), lambda c: (c, 0)),
                   const((rows, 2 * S5_FLAT))],
        out_shape=[jax.ShapeDtypeStruct((n, S5_WIDTH), BF16),
                   jax.ShapeDtypeStruct((rows, 2 * S5_FLAT), F32)],
        scratch_shapes=[pltpu.VMEM((blk, 2 * S5_FLAT), F32),
                        pltpu.VMEM((rows, 2 * S5_FLAT), F32)],
        compiler_params=_cp("arbitrary"),
        name="s5",
    )(u_tm, h0, bd, ar, ai, cbd, d, wg)


def _s5_params(lam_re, lam_im, log_dt, b_re, b_im, c_re, c_im, d_skip, w_glu):
    dt = jnp.exp(log_dt)[:, None]
    mag = jnp.exp(lam_re * dt)
    ab_re, ab_im = mag * jnp.cos(lam_im * dt), mag * jnp.sin(lam_im * dt)
    den = lam_re * lam_re + lam_im * lam_im
    nr, ni = ab_re - 1.0, ab_im
    coef_re = ((nr * lam_re + ni * lam_im) / den)[..., None]
    coef_im = ((ni * lam_re - nr * lam_im) / den)[..., None]
    bb_re = coef_re * b_re - coef_im * b_im
    bb_im = coef_re * b_im + coef_im * b_re
    eye = jnp.eye(S5_GROUPS, dtype=F32)
    pack_in = lambda bb: jnp.einsum('gpc,gh->gchp', bb, eye).reshape(S5_WIDTH, S5_FLAT)
    bd = jnp.concatenate([pack_in(bb_re), pack_in(bb_im)], axis=1).astype(BF16)
    pack_out = lambda cc: jnp.einsum('gcp,gh->gphc', cc, eye).reshape(S5_FLAT, S5_WIDTH)
    cbd = jnp.concatenate([pack_out(c_re), -pack_out(c_im)], axis=0).astype(BF16)
    return (bd, ab_re.reshape(1, S5_FLAT), ab_im.reshape(1, S5_FLAT), cbd,
            d_skip.reshape(1, S5_WIDTH), w_glu.astype(BF16))


def _swa_kernel(sink_ref, q_ref, kp_ref, kc_ref, vp_ref, vc_ref, o_ref):
    n = pl.program_id(1)
    q = q_ref[0].astype(BF16)
    k = jnp.concatenate([kp_ref[0], kc_ref[0]], axis=0).astype(BF16)
    v = jnp.concatenate([vp_ref[0], vc_ref[0]], axis=0).astype(BF16)
    qi = lax.broadcasted_iota(jnp.int32, (WINDOW, 2 * WINDOW), 0)
    kj = lax.broadcasted_iota(jnp.int32, (WINDOW, 2 * WINDOW), 1)
    dist = WINDOW + qi - kj
    valid = (dist >= 0) & (dist < WINDOW) & ((n > 0) | (kj >= WINDOW))
    distf = dist.astype(F32)
    outs = []
    for h in range(SWA_HEADS):
        g = h // SWA_REP
        qh = q[:, h * SWA_HEAD_DIM:(h + 1) * SWA_HEAD_DIM]
        kg = k[:, g * SWA_HEAD_DIM:(g + 1) * SWA_HEAD_DIM]
        vg = v[:, g * SWA_HEAD_DIM:(g + 1) * SWA_HEAD_DIM]
        s = _dot_nt(qh, kg) * (SWA_HEAD_DIM ** -0.5)
        s = s - (2.0 ** -(h + 1)) * distf
        s = jnp.where(valid, s, NEG_INF)
        sink = sink_ref[h]
        m = jnp.maximum(jnp.max(s, axis=-1, keepdims=True), sink)
        p = jnp.exp(s - m)
        p = p / (jnp.sum(p, axis=-1, keepdims=True) + jnp.exp(sink - m))
        outs.append(_dot(p.astype(BF16), vg))
    o_ref[0] = jnp.concatenate(outs, axis=-1).astype(o_ref.dtype)


def _swa_prompt(proj, sinks):
    bsz, seq, _ = proj.shape
    kcol = (S5_WIDTH + SWA_WIDTH) // SWA_KV_WIDTH
    prev = lambda b, n: (b, jnp.maximum(n - 1, 0), kcol)
    prev_v = lambda b, n: (b, jnp.maximum(n - 1, 0), kcol + 1)
    blk = (1, WINDOW, SWA_KV_WIDTH)
    return pl.pallas_call(
        _swa_kernel,
        grid=(bsz, seq // WINDOW),
        in_specs=[pl.BlockSpec(memory_space=pltpu.SMEM),
                  pl.BlockSpec((1, WINDOW, SWA_WIDTH), lambda b, n: (b, n, 1)),
                  pl.BlockSpec(blk, prev),
                  pl.BlockSpec(blk, lambda b, n: (b, n, kcol)),
                  pl.BlockSpec(blk, prev_v),
                  pl.BlockSpec(blk, lambda b, n: (b, n, kcol + 1))],
        out_specs=pl.BlockSpec((1, WINDOW, SWA_WIDTH), lambda b, n: (b, n, 0)),
        out_shape=jax.ShapeDtypeStruct((bsz, seq, SWA_WIDTH), BF16),
        compiler_params=_cp("parallel", "arbitrary"),
        name="swa_prompt",
    )(sinks, proj, proj, proj, proj, proj)


def _per_head_column(values):
    r = lax.broadcasted_iota(jnp.int32, (SWA_REP, 1), 0)
    col = jnp.full((SWA_REP, 1), values[-1], F32)
    for i in range(SWA_REP - 2, -1, -1):
        col = jnp.where(r == i, values[i], col)
    return col


def _swa_step_kernel(sink_ref, p_ref, kc_ref, vc_ref, o_ref, *, nseq, start):
    kj = lax.broadcasted_iota(jnp.int32, (SWA_REP, WINDOW), 1)
    dist = WINDOW - kj
    valid = (dist < WINDOW) & (start - WINDOW + kj >= 0)
    distf = dist.astype(F32)
    for i in range(nseq):
        row = p_ref[i]
        kn = row[:, S5_WIDTH + SWA_WIDTH:S5_WIDTH + SWA_WIDTH + SWA_KV_WIDTH].astype(BF16)
        vn = row[:, S5_WIDTH + SWA_WIDTH + SWA_KV_WIDTH:].astype(BF16)
        kc = kc_ref[i].astype(BF16)
        vc = vc_ref[i].astype(BF16)
        outs = []
        for g in range(SWA_KV_HEADS):
            q4 = jnp.concatenate(
                [row[:, S5_WIDTH + (g * SWA_REP + r) * SWA_HEAD_DIM:S5_WIDTH + (g * SWA_REP + r + 1) * SWA_HEAD_DIM]
                 for r in range(SWA_REP)], axis=0).astype(BF16)
            sl = slice(g * SWA_HEAD_DIM, (g + 1) * SWA_HEAD_DIM)
            slope = _per_head_column([2.0 ** -(g * SWA_REP + r + 1) for r in range(SWA_REP)])
            sink = _per_head_column([sink_ref[g * SWA_REP + r] for r in range(SWA_REP)])
            scale = SWA_HEAD_DIM ** -0.5
            sc = _dot_nt(q4, kc[:, sl]) * scale - slope * distf
            sc = jnp.where(valid, sc, NEG_INF)
            sn = jnp.sum(q4.astype(F32) * kn[:, sl].astype(F32), axis=-1, keepdims=True) * scale
            m = jnp.maximum(jnp.maximum(jnp.max(sc, axis=-1, keepdims=True), sn), sink)
            pc = jnp.exp(sc - m)
            pn = jnp.exp(sn - m)
            den = jnp.sum(pc, axis=-1, keepdims=True) + pn + jnp.exp(sink - m)
            o4 = _dot((pc / den).astype(BF16), vc[:, sl])
            o4 = o4 + (pn / den).astype(BF16).astype(F32) * vn[:, sl].astype(F32)
            outs.extend([o4[r:r + 1, :] for r in range(SWA_REP)])
        o_ref[i] = jnp.concatenate(outs, axis=-1).astype(o_ref.dtype)


def _swa_step(proj, k_cache, v_cache, sinks, start):
    n = proj.shape[0]
    nseq = min(n, SUBLANES)
    out = pl.pallas_call(
        functools.partial(_swa_step_kernel, nseq=nseq, start=start),
        grid=(n // nseq,),
        in_specs=[pl.BlockSpec(memory_space=pltpu.SMEM),
                  pl.BlockSpec((nseq, 1, proj.shape[1]), lambda i: (i, 0, 0)),
                  pl.BlockSpec((nseq, WINDOW, SWA_KV_WIDTH), lambda i: (i, 0, 0)),
                  pl.BlockSpec((nseq, WINDOW, SWA_KV_WIDTH), lambda i: (i, 0, 0))],
        out_specs=pl.BlockSpec((nseq, 1, SWA_WIDTH), lambda i: (i, 0, 0)),
        out_shape=jax.ShapeDtypeStruct((n, 1, SWA_WIDTH), BF16),
        compiler_params=_cp("parallel"),
        name="swa_step",
    )(sinks, proj.reshape(n, 1, -1), k_cache, v_cache)
    return out.reshape(n, SWA_WIDTH)


def _head_expand(width):
    r = lax.broadcasted_iota(jnp.int32, (LANES, SSD_HEADS * width), 0)
    c = lax.broadcasted_iota(jnp.int32, (LANES, SSD_HEADS * width), 1)
    return jnp.where((c >= r * width) & (c < (r + 1) * width), 1.0, 0.0).astype(BF16)


def _ssd_kernel(x_ref, wz_ref, wx_ref, wdt_ref, cw_ref, cb_ref, dtb_ref, a_ref, d_ref, nw_ref,
                wo_ref, g_ref, b_ref, o_ref, st_ref, conv_ref, tail, state, ybuf):
    c = pl.program_id(1)
    q = SSD_CHUNK

    @pl.when(c == 0)
    def _():
        tail[...] = jnp.zeros_like(tail)
        state[...] = jnp.zeros_like(state)

    xin = x_ref[0]
    xb = xin.astype(BF16)
    cur = _dot(xb, wx_ref[...])
    tl = tail[...]
    row8 = lax.broadcasted_iota(jnp.int32, (SUBLANES, SSD_CONV_DIM), 0)
    acc = None
    for tap in range(SSD_CONV):
        sh = SSD_CONV - 1 - tap
        if sh == 0:
            term = cur
        else:
            rolled = pltpu.roll(cur, sh, 0)
            head = jnp.where(row8 < sh, pltpu.roll(tl, sh, 0), rolled[0:SUBLANES])
            term = jnp.concatenate([head, rolled[SUBLANES:]], axis=0)
        term = term * cw_ref[tap:tap + 1, :]
        acc = term if acc is None else acc + term
    tail[...] = cur[q - SUBLANES:q]
    xbc = _silu(acc + cb_ref[...])
    xs = xbc[:, 0:SSD_INNER]
    bm = xbc[:, SSD_INNER:SSD_INNER + SSD_GROUPS * SSD_STATE].astype(BF16)
    cm = xbc[:, SSD_INNER + SSD_GROUPS * SSD_STATE:].astype(BF16)

    dt = _softplus(_dot(xb, wdt_ref[...]) + dtb_ref[...])
    da = dt * a_ref[...]
    ri = lax.broadcasted_iota(jnp.int32, (q, q), 0)
    ci = lax.broadcasted_iota(jnp.int32, (q, q), 1)
    causal = ri >= ci
    tri = jnp.where(causal, 1.0, 0.0).astype(BF16)
    da1, da2, da3 = _split3(da)
    acs = _dot(tri, da1) + _dot(tri, da2) + _dot(tri, da3)
    acs_t = acs.T
    e64 = _head_expand(SSD_HEAD_DIM)
    dt_x = _dot_exact01(dt, e64)
    acs_x = _dot_exact01(acs, e64)
    last_x = acs_x[q - 1:q, :]
    dx = dt_x * xs
    dsx = (jnp.exp(last_x - acs_x) * dx).astype(BF16)
    eacs_x = jnp.exp(acs_x)
    dxb = dx.astype(BF16)

    for g in range(SSD_GROUPS):
        ns = slice(g * SSD_STATE, (g + 1) * SSD_STATE)
        gs = slice(g * SSD_GW, (g + 1) * SSD_GW)
        cb = _dot_nt(cm[:, ns], bm[:, ns])
        for r in range(SSD_HPG):
            h = g * SSD_HPG + r
            seg = acs[:, h:h + 1] - acs_t[h:h + 1, :]
            decay = jnp.where(causal, jnp.exp(jnp.where(causal, seg, 0.0)), 0.0)
            hs = slice(h * SSD_HEAD_DIM, (h + 1) * SSD_HEAD_DIM)
            ybuf[:, hs] = _dot((cb * decay).astype(BF16), dxb[:, hs])
        s_in = state[g]
        y_off = _dot(cm[:, ns], s_in.astype(BF16)) * eacs_x[:, gs]
        ybuf[:, gs] = ybuf[:, gs] + y_off
        state[g] = jnp.exp(last_x[:, gs]) * s_in + _dot_tn(bm[:, ns], dsx[:, gs])

    y = ybuf[...] + d_ref[...] * xs
    y = y * _silu(_dot(xb, wz_ref[...]))
    y = y * lax.rsqrt(jnp.mean(y * y, axis=-1, keepdims=True) + RMS_EPS) * nw_ref[...]
    mix = _dot(y.astype(BF16), wo_ref[...])
    o_ref[0] = _ln(DN_ALPHA * xin + mix, g_ref[...], b_ref[...])

    @pl.when(c == pl.num_programs(1) - 1)
    def _():
        conv_ref[0] = cur[q - SUBLANES:q]
        for g in range(SSD_GROUPS):
            st_ref[0, g * SSD_GW:(g + 1) * SSD_GW, :] = state[g].T


def _ssd_prompt(x, w3, prm, w_out, g, b):
    bsz, seq, _ = x.shape
    w_z, w_x, w_dt = w3
    cw, cb, dtb, a, d_x, nw = prm
    const = lambda shape: pl.BlockSpec(shape, lambda b, c: (0, 0))
    seqblk = lambda w: pl.BlockSpec((1, SSD_CHUNK, w), lambda b, c: (b, c, 0))
    return pl.pallas_call(
        _ssd_kernel,
        grid=(bsz, seq // SSD_CHUNK),
        in_specs=[seqblk(D_MODEL),
                  const((D_MODEL, SSD_INNER)), const((D_MODEL, SSD_CONV_DIM)), const((D_MODEL, LANES)),
                  const((SSD_CONV, SSD_CONV_DIM)), const((1, SSD_CONV_DIM)),
                  const((1, LANES)), const((1, LANES)),
                  const((1, SSD_INNER)), const((1, SSD_INNER)),
                  const((SSD_INNER, D_MODEL)), const((1, D_MODEL)), const((1, D_MODEL))],
        out_specs=[seqblk(D_MODEL),
                   pl.BlockSpec((1, SSD_INNER, SSD_STATE), lambda b, c: (b, 0, 0)),
                   pl.BlockSpec((1, SUBLANES, SSD_CONV_DIM), lambda b, c: (b, 0, 0))],
        out_shape=[jax.ShapeDtypeStruct((bsz, seq, D_MODEL), F32),
                   jax.ShapeDtypeStruct((bsz, SSD_INNER, SSD_STATE), F32),
                   jax.ShapeDtypeStruct((bsz, SUBLANES, SSD_CONV_DIM), F32)],
        scratch_shapes=[pltpu.VMEM((SUBLANES, SSD_CONV_DIM), F32),
                        pltpu.VMEM((SSD_GROUPS, SSD_STATE, SSD_GW), F32),
                        pltpu.VMEM((SSD_CHUNK, SSD_INNER), F32)],
        compiler_params=_cp("parallel", "arbitrary"),
        name="ssd_prompt",
    )(x, w_z, w_x, w_dt, cw, cb, dtb, a, d_x, nw, w_out, g.reshape(1, -1), b.reshape(1, -1))


def _pad_rows(row):
    return jnp.concatenate([row, jnp.zeros((SUBLANES - 1, row.shape[1]), row.dtype)], axis=0)


def _ssd_step_kernel(z_ref, x_ref, dt_ref, buf_ref, s0_ref, cw_ref, cb_ref, dtb_ref, a_ref, d_ref, nw_ref,
                     y_ref, s1_ref, *, nseq):
    e64 = _head_expand(SSD_HEAD_DIM)
    ones3 = jnp.where(lax.broadcasted_iota(jnp.int32, (SUBLANES, SSD_STATE), 0) < 3, 1.0, 0.0).astype(BF16)
    for i in range(nseq):
        buf = buf_ref[i]
        acc = buf[0:1] * cw_ref[0:1, :]
        acc = acc + buf[1:2] * cw_ref[1:2, :]
        acc = acc + buf[2:3] * cw_ref[2:3, :]
        acc = acc + x_ref[i] * cw_ref[3:4, :]
        xbc = _silu(acc + cb_ref[...])
        xs = xbc[:, 0:SSD_INNER]
        bm = xbc[:, SSD_INNER:SSD_INNER + SSD_GROUPS * SSD_STATE].astype(BF16)
        cm = xbc[:, SSD_INNER + SSD_GROUPS * SSD_STATE:].astype(BF16)
        dt = _softplus(dt_ref[i] + dtb_ref[...])
        ed = jnp.exp(dt * a_ref[...])
        dt_x = _dot_exact01(_pad_rows(dt), e64)[0:1]
        ed_x = _dot_exact01(_pad_rows(ed), e64)[0:1]
        dx = dt_x * xs
        dxb = dx.astype(BF16)
        ys = []
        for g in range(SSD_GROUPS):
            ns = slice(g * SSD_STATE, (g + 1) * SSD_STATE)
            gs = slice(g * SSD_GW, (g + 1) * SSD_GW)
            s0 = s0_ref[i, gs, :]
            cmg = _pad_rows(cm[:, ns])
            cbv = jnp.sum(cm[:, ns].astype(F32) * bm[:, ns].astype(F32), axis=-1, keepdims=True)
            y_diag = cbv.astype(BF16).astype(F32) * dxb[:, gs].astype(F32)
            y_off = _dot_nt(cmg, s0.astype(BF16))[0:1] * ed_x[:, gs]
            ys.append(y_diag + y_off)
            e1, e2, e3 = _split3(ed_x[:, gs])
            ed_rows = jnp.concatenate([e1, e2, e3, jnp.zeros((SUBLANES - 3, SSD_GW), BF16)], axis=0)
            ed_full = _dot_tn(ed_rows, ones3)
            outer = _dot_tn(_pad_rows(dxb[:, gs]), _pad_rows(bm[:, ns]))
            s1_ref[i, gs, :] = ed_full * s0 + outer
        y = jnp.concatenate(ys, axis=-1) + d_ref[...] * xs
        y = y * _silu(z_ref[i])
        y = y * lax.rsqrt(jnp.mean(y * y, axis=-1, keepdims=True) + RMS_EPS) * nw_ref[...]
        y_ref[i] = y.astype(y_ref.dtype)


def _ssd_step(z, xbc, dtraw, conv_buf, s0_all, layer, prm):
    n = z.shape[0]
    nseq = min(n, 4)
    cw, cb, dtb, a, d_x, nw = prm
    const = lambda shape: pl.BlockSpec(shape, lambda i: (0, 0))
    tok = lambda w: pl.BlockSpec((nseq, 1, w), lambda i: (i, 0, 0))
    y, s1 = pl.pallas_call(
        functools.partial(_ssd_step_kernel, nseq=nseq),
        grid=(n // nseq,),
        in_specs=[tok(SSD_INNER), tok(SSD_CONV_DIM), tok(LANES),
                  pl.BlockSpec((nseq, SSD_CONV - 1, SSD_CONV_DIM), lambda i: (i, 0, 0)),
                  pl.BlockSpec((None, nseq, SSD_INNER, SSD_STATE), lambda i: (layer, i, 0, 0)),
                  const((SSD_CONV, SSD_CONV_DIM)), const((1, SSD_CONV_DIM)),
                  const((1, LANES)), const((1, LANES)),
                  const((1, SSD_INNER)), const((1, SSD_INNER))],
        out_specs=[tok(SSD_INNER),
                   pl.BlockSpec((nseq, SSD_INNER, SSD_STATE), lambda i: (i, 0, 0))],
        out_shape=[jax.ShapeDtypeStruct((n, 1, SSD_INNER), BF16),
                   jax.ShapeDtypeStruct((n, SSD_INNER, SSD_STATE), F32)],
        compiler_params=_cp("parallel"),
        name="ssd_step",
    )(z.reshape(n, 1, -1), xbc.reshape(n, 1, -1), dtraw.reshape(n, 1, -1), conv_buf, s0_all,
      cw, cb, dtb, a, d_x, nw)
    return y.reshape(n, SSD_INNER), s1


def _ssd_params(conv_w, conv_b, dt_bias, a_log, d_skip, norm_w):
    pad = lambda v: jnp.pad(v, (0, LANES - SSD_HEADS)).reshape(1, LANES)
    return (conv_w, conv_b.reshape(1, -1), pad(dt_bias), pad(-jnp.exp(a_log)),
            jnp.repeat(d_skip, SSD_HEAD_DIM).reshape(1, SSD_INNER), norm_w.reshape(1, SSD_INNER))


PEER_TE = 1024
PEER_ROWS = PEER_TE // PEER_KEYS


PEER_CHUNK_ROWS = 2
PEER_TOK_CHUNK = 256
PACK = 2 * SUBLANES


def _batcher_sort_pairs(n):
    def merge(lo, hi, r):
        step = r * 2
        if step < hi - lo:
            yield from merge(lo, hi, step)
            yield from merge(lo + r, hi, step)
            for i in range(lo + r, hi - r, step):
                yield (i, i + r)
        else:
            yield (lo, lo + r)

    def sort(lo, hi):
        if hi - lo >= 1:
            mid = lo + (hi - lo) // 2
            yield from sort(lo, mid)
            yield from sort(mid + 1, hi)
            yield from merge(lo, hi, 1)

    return tuple(sort(0, n - 1))


_SORT16 = _batcher_sort_pairs(PEER_TOPK)
_BITONIC16 = tuple((i, i + d) for d in (8, 4, 2, 1) for i in range(PEER_TOPK) if not i & d)


def _mx(a, b):
    return b if a is None else a if b is None else jnp.maximum(a, b)


def _mn(a, b):
    return None if a is None or b is None else jnp.minimum(a, b)


def _top16(rows):
    v = list(rows) + [None] * (PEER_TOPK - len(rows))
    for i, j in _SORT16:
        v[i], v[j] = _mx(v[i], v[j]), _mn(v[i], v[j])
    for shift in (4, 2, 1):
        w = [None if x is None else pltpu.roll(x, shift, 0) for x in v]
        v = [_mx(v[k], w[PEER_TOPK - 1 - k]) for k in range(PEER_TOPK)]
        for i, j in _BITONIC16:
            v[i], v[j] = _mx(v[i], v[j]), _mn(v[i], v[j])
    return v


def _prefix_count(test, thr):
    m8 = test(thr[7])
    m4 = test(jnp.where(m8, thr[11], thr[3]))
    m2 = test(jnp.where(m8, jnp.where(m4, thr[13], thr[9]), jnp.where(m4, thr[5], thr[1])))
    hi = jnp.where(m4, jnp.where(m2, thr[14], thr[12]), jnp.where(m2, thr[10], thr[8]))
    lo = jnp.where(m4, jnp.where(m2, thr[6], thr[4]), jnp.where(m2, thr[2], thr[0]))
    m1 = test(jnp.where(m8, hi, lo))
    return (jnp.where(m8, 8.0, 0.0) + jnp.where(m4, 4.0, 0.0)) + (jnp.where(m2, 2.0, 0.0) + jnp.where(m1, 1.0, 0.0))


def _peer_route(xb, wq_ref, k1_ref, k2_ref, c_ref, e1_ref, rk_ref, e2_ref):
    tm = xb.shape[0]
    q = _dot(xb, wq_ref[...]).astype(BF16)
    half = PEER_DK // 2
    nblk = PEER_KEYS // SUBLANES
    sub = lax.broadcasted_iota(jnp.int32, (SUBLANES, tm), 0)

    def by_sublane(vals):
        out = vals[SUBLANES - 1]
        for s in range(SUBLANES - 2, -1, -1):
            out = jnp.where(sub == s, vals[s], out)
        return out

    for h in range(PEER_HEADS):
        s1 = _dot_nt(k1_ref[h], q[:, h * PEER_DK:h * PEER_DK + half])
        s2 = _dot_nt(k2_ref[h], q[:, h * PEER_DK + half:(h + 1) * PEER_DK])
        r1 = [s1[k * SUBLANES:(k + 1) * SUBLANES] for k in range(nblk)]
        r2 = [s2[k * SUBLANES:(k + 1) * SUBLANES] for k in range(nblk)]
        a = _top16(r1)
        b = _top16(r2)
        r1 = [x - a[0] for x in r1]
        r2 = [x - b[0] for x in r2]
        a = [x - a[0] for x in a]
        b = [x - b[0] for x in b]
        b_lo, b_hi, a_hi = by_sublane(b[0:8]), by_sublane(b[8:16]), by_sublane(a[8:16])
        cand = [a[0] + b_lo, a[0] + b_hi] + [a[i] + b_lo for i in range(1, 8)] + [a_hi + b[0]]
        top = _top16(cand)
        tau = top[PEER_TOPK - 1]
        zsum = jnp.exp(top[0])
        for k in range(1, PEER_TOPK):
            zsum = zsum + jnp.exp(top[k])
        inv_z = 1.0 / zsum
        for k in range(nblk):
            x1 = r1[k]
            cnt = _prefix_count(lambda t: (x1 + t) >= tau, b[0:15])
            cnt = cnt + jnp.where((x1 + b[15]) >= tau, 1.0, 0.0)
            c_ref[h, k, :, 0:tm] = cnt
            e1_ref[h, k, :, 0:tm] = jnp.exp(x1) * inv_z
        ranks, e2s = [], []
        for k in range(nblk):
            x2 = r2[k]
            rnk = _prefix_count(lambda t: t > x2, b[0:15])
            ranks.append(jnp.where(x2 >= b[15], rnk, float(PEER_TOPK)))
            e2s.append(jnp.exp(x2))
        rk_ref[h, :, 0:tm] = jnp.concatenate(ranks, axis=0).astype(BF16)
        e2_ref[h, :, 0:tm] = jnp.concatenate(e2s, axis=0).astype(BF16)


def _gelu_tanh(x):
    c0 = math.sqrt(2.0 / math.pi)
    return (0.5 * x) * (1.0 + jnp.tanh(x * (c0 + (c0 * 0.044715) * (x * x))))


def _peer_kernel(x_ref, xp_ref, wq_ref, k1_ref, k2_ref, u_ref, vt_ref, g_ref, b_ref, o_ref,
                 acc, c_ref, e1_ref, rk_ref, e2_ref, act_ref):
    i = pl.program_id(0)
    j = pl.program_id(1)
    nblk = pl.num_programs(0) - 1
    tm = x_ref.shape[0]
    xb = x_ref[...].astype(BF16)

    @pl.when((i == 0) & (j == 0))
    def _():
        acc[...] = jnp.zeros_like(acc)
        act_ref[1] = jnp.zeros((PEER_TE, tm), BF16)

    @pl.when((j == 0) & (i < nblk))
    def _():
        _peer_route(xb, wq_ref, k1_ref, k2_ref, c_ref, e1_ref, rk_ref, e2_ref)

    nb2 = PEER_KEYS // PACK
    slot = j % 2
    crows = PEER_CHUNK_ROWS
    nch = PEER_ROWS // crows
    tc = min(tm, PEER_TOK_CHUNK)
    apply_after = {nch // 2 - 1 + k: k * tc for k in range(tm // tc)}

    @pl.when((i < nblk) | (j == 0))
    def _():
        for ch in range(nch):
            e0 = ch * crows * PEER_KEYS
            h_t = _dot_nt(u_ref[e0:e0 + crows * PEER_KEYS, :], xb)
            if ch in apply_after:
                t0 = apply_after[ch]
                acc[:, t0:t0 + tc] += _dot(vt_ref[...], act_ref[1 - slot, :, t0:t0 + tc])
            rows = [ch * crows + ri for ri in range(crows)]
            for tl in range(tm // LANES):
                ts = slice(tl * LANES, (tl + 1) * LANES)
                gate = [[None] * nb2 for _ in range(crows)]
                for h in range(PEER_HEADS):
                    cbv = [jnp.broadcast_to(c_ref[h, j, r:r + 1, ts], (PACK, LANES)).astype(BF16) for r in rows]
                    ebv = [jnp.broadcast_to(e1_ref[h, j, r:r + 1, ts], (PACK, LANES)).astype(BF16) for r in rows]
                    for b2 in range(nb2):
                        ks = slice(b2 * PACK, (b2 + 1) * PACK)
                        rk = rk_ref[h, ks, ts]
                        e2 = e2_ref[h, ks, ts]
                        for ri in range(crows):
                            w = jnp.where(rk < cbv[ri], ebv[ri] * e2, 0.0)
                            gate[ri][b2] = w if gate[ri][b2] is None else gate[ri][b2] + w
                for ri in range(crows):
                    for b2 in range(nb2):
                        lo = ri * PEER_KEYS + b2 * PACK
                        hv = _gelu_tanh(h_t[lo:lo + PACK, ts].astype(BF16))
                        act_ref[slot, e0 + lo:e0 + lo + PACK, ts] = hv * gate[ri][b2]

    @pl.when(j == 0)
    def _():
        o_ref[...] = _ln(DN_ALPHA * xp_ref[...] + acc[...].T, g_ref[...], b_ref[...])
        acc[...] = jnp.zeros_like(acc)


def _peer_ln(x, wq, k1, k2, u, vt, g, b):
    t = x.shape[0]
    tm = min(t, 512)
    nblk = t // tm
    ntile = PEER_EXPERTS // PEER_TE
    const2 = lambda shape: pl.BlockSpec(shape, lambda i, j: (0, 0))
    const3 = lambda shape: pl.BlockSpec(shape, lambda i, j: (0, 0, 0))
    cur = lambda i, j: (jnp.minimum(i, nblk - 1), 0)
    prev = lambda i, j: (jnp.maximum(i - 1, 0), 0)
    return pl.pallas_call(
        _peer_kernel,
        grid=(nblk + 1, ntile),
        in_specs=[pl.BlockSpec((tm, D_MODEL), cur),
                  pl.BlockSpec((tm, D_MODEL), prev),
                  const2((D_MODEL, PEER_HEADS * PEER_DK)),
                  const3((PEER_HEADS, PEER_KEYS, PEER_DK // 2)),
                  const3((PEER_HEADS, PEER_KEYS, PEER_DK // 2)),
                  pl.BlockSpec((PEER_TE, D_MODEL), lambda i, j: (jnp.where(i < nblk, j, 0), 0)),
                  pl.BlockSpec((D_MODEL, PEER_TE),
                               lambda i, j: (0, jnp.where(i < nblk, (j + ntile - 1) % ntile, ntile - 1))),
                  const2((1, D_MODEL)), const2((1, D_MODEL))],
        out_specs=pl.BlockSpec((tm, D_MODEL), lambda i, j: (jnp.where(j == 0, prev(i, j)[0], cur(i, j)[0]), 0)),
        out_shape=jax.ShapeDtypeStruct((t, D_MODEL), F32),
        scratch_shapes=[pltpu.VMEM((D_MODEL, tm), F32),
                        pltpu.VMEM((PEER_HEADS, PEER_KEYS // PEER_ROWS, PEER_ROWS, tm), F32),
                        pltpu.VMEM((PEER_HEADS, PEER_KEYS // PEER_ROWS, PEER_ROWS, tm), F32),
                        pltpu.VMEM((PEER_HEADS, PEER_KEYS, tm), BF16),
                        pltpu.VMEM((PEER_HEADS, PEER_KEYS, tm), BF16),
                        pltpu.VMEM((2, PEER_TE, tm), BF16)],
        compiler_params=_cp("arbitrary", "arbitrary"),
        name="peer",
    )(x, x, wq, k1, k2, u, vt, g.reshape(1, -1), b.reshape(1, -1))


def _even_prompt(x, bsz, seq, w_in, s5p, sinks, w_out, g, b):
    proj = _mm(x, w_in, tn=640)
    p3 = proj.reshape(bsz, seq, -1)
    u_tm = p3[:, :, :S5_WIDTH].transpose(1, 0, 2).reshape(seq * bsz, S5_WIDTH)
    steps = min(seq, 64)
    y_a, h_t = _s5(u_tm, jnp.zeros((bsz, 2 * S5_FLAT), F32), s5p, bsz, steps)
    y_a = y_a.reshape(seq, bsz, S5_WIDTH).transpose(1, 0, 2).reshape(bsz * seq, S5_WIDTH)
    y_b = _swa_prompt(p3, sinks).reshape(bsz * seq, SWA_WIDTH)
    x = _mm2_ln(y_a, y_b, w_out, x, g, b)
    kv = p3[:, seq - WINDOW:, S5_WIDTH + SWA_WIDTH:]
    new_k = kv[..., :SWA_KV_WIDTH].reshape(bsz, WINDOW, SWA_KV_HEADS, SWA_HEAD_DIM)
    new_v = kv[..., SWA_KV_WIDTH:].reshape(bsz, WINDOW, SWA_KV_HEADS, SWA_HEAD_DIM)
    h_re = h_t[:, :S5_FLAT].reshape(bsz, S5_GROUPS, S5_STATE)
    h_im = h_t[:, S5_FLAT:].reshape(bsz, S5_GROUPS, S5_STATE)
    return x, h_re, h_im, new_k, new_v


def _even_sample(x, h_re, h_im, k_buf, v_buf, start, w_in, s5p, sinks, w_out, g, b):
    n = x.shape[0]
    proj = _mm(x, w_in, tn=640)
    h0 = jnp.concatenate([h_re.reshape(n, S5_FLAT), h_im.reshape(n, S5_FLAT)], axis=1)
    y_a, h_t = _s5(proj[:, :S5_WIDTH], h0, s5p, n, 1)
    kc = k_buf.reshape(n, WINDOW, SWA_KV_WIDTH)
    vc = v_buf.reshape(n, WINDOW, SWA_KV_WIDTH)
    y_b = _swa_step(proj, kc, vc, sinks, start)
    x = _mm2_ln(y_a, y_b, w_out, x, g, b)
    kn = proj[:, None, S5_WIDTH + SWA_WIDTH:S5_WIDTH + SWA_WIDTH + SWA_KV_WIDTH]
    vn = proj[:, None, S5_WIDTH + SWA_WIDTH + SWA_KV_WIDTH:]
    new_k = jnp.concatenate([kc[:, 1:], kn], axis=1).reshape(k_buf.shape)
    new_v = jnp.concatenate([vc[:, 1:], vn], axis=1).reshape(v_buf.shape)
    return (x, h_t[:, :S5_FLAT].reshape(h_re.shape), h_t[:, S5_FLAT:].reshape(h_im.shape), new_k, new_v)


def _odd_weights(w_in):
    w_z = w_in[:, :SSD_INNER].astype(BF16)
    w_x = w_in[:, SSD_INNER:SSD_INNER + SSD_CONV_DIM].astype(BF16)
    w_dt = jnp.pad(w_in[:, SSD_INNER + SSD_CONV_DIM:], ((0, 0), (0, LANES - SSD_HEADS))).astype(BF16)
    return w_z, w_x, w_dt


def _odd_prompt(x, bsz, seq, w3, ssdp, w_out, g, b):
    x, st, tail = _ssd_prompt(x.reshape(bsz, seq, -1), w3, ssdp, w_out, g, b)
    return (x.reshape(bsz * seq, -1), st.reshape(bsz, SSD_HEADS, SSD_HEAD_DIM, SSD_STATE),
            tail[:, SUBLANES - (SSD_CONV - 1):])


def _odd_sample(x, ssm_all, layer, conv, w3, ssdp, w_out, g, b):
    n = x.shape[0]
    w_z, w_x, w_dt = w3
    z = _mm(x, w_z)
    xbc = _mm(x, w_x)
    dtraw = _mm(x, w_dt)
    s0_all = ssm_all.reshape(ssm_all.shape[0], n, SSD_INNER, SSD_STATE)
    y, st = _ssd_step(z, xbc, dtraw, conv, s0_all, layer, ssdp)
    x = _mm_ln(y, w_out, x, g, b)
    new_conv = jnp.concatenate([conv[:, 1:], xbc[:, None, :]], axis=1)
    return x, st.reshape(ssm_all.shape[1:]), new_conv


def kernel(x_prompt, x_sample, state_s5_re, state_s5_im, cache_swa_k, cache_swa_v, state_ssd, state_conv,
           w_in_even, s5_lambda_re, s5_lambda_im, s5_log_dt, s5_b_re, s5_b_im, s5_c_re, s5_c_im, s5_d,
           s5_w_glu, swa_sinks, w_out_even,
           w_in_odd, ssd_conv_w, ssd_conv_b, ssd_dt_bias, ssd_a_log, ssd_d, ssd_norm_w, w_out_odd,
           ln1_g, ln1_b, ln2_g, ln2_b, peer_w_q, peer_k1, peer_k2, peer_u, peer_v):
    bsz, seq, _ = x_prompt.shape
    nsm = x_sample.shape[0]
    xp = x_prompt.reshape(bsz * seq, D_MODEL)
    xs = x_sample.reshape(nsm, D_MODEL)
    p_out = [[] for _ in range(6)]
    s_out = [[] for _ in range(6)]
    for layer in range(DEPTH):
        i = layer // 2
        if layer % 2 == 0:
            w_in = w_in_even[i].astype(BF16)
            w_out = w_out_even[i].astype(BF16)
            s5p = _s5_params(s5_lambda_re[i], s5_lambda_im[i], s5_log_dt[i], s5_b_re[i], s5_b_im[i],
                             s5_c_re[i], s5_c_im[i], s5_d[i], s5_w_glu[i])
            xp, hr, hi, nk, nv = _even_prompt(xp, bsz, seq, w_in, s5p, swa_sinks[i], w_out,
                                              ln1_g[layer], ln1_b[layer])
            for lst, val in zip(p_out[:4], (hr, hi, nk, nv)):
                lst.append(val)
            xs, hr, hi, nk, nv = _even_sample(xs, state_s5_re[i], state_s5_im[i], cache_swa_k[i], cache_swa_v[i],
                                              PAST_LEN, w_in, s5p, swa_sinks[i], w_out,
                                              ln1_g[layer], ln1_b[layer])
            for lst, val in zip(s_out[:4], (hr, hi, nk, nv)):
                lst.append(val)
        else:
            w3 = _odd_weights(w_in_odd[i])
            w_out = w_out_odd[i].astype(BF16)
            ssdp = _ssd_params(ssd_conv_w[i], ssd_conv_b[i], ssd_dt_bias[i], ssd_a_log[i], ssd_d[i],
                               ssd_norm_w[i])
            xp, st, cb = _odd_prompt(xp, bsz, seq, w3, ssdp, w_out, ln1_g[layer], ln1_b[layer])
            p_out[4].append(st)
            p_out[5].append(cb)
            xs, st, cb = _odd_sample(xs, state_ssd, i, state_conv[i], w3, ssdp, w_out,
                                     ln1_g[layer], ln1_b[layer])
            s_out[4].append(st)
            s_out[5].append(cb)
        wq = peer_w_q[layer].astype(BF16)
        k1 = peer_k1[layer].astype(BF16)
        k2 = peer_k2[layer].astype(BF16)
        u = peer_u[layer].astype(BF16)
        vt = peer_v[layer].astype(BF16).T
        xp = _peer_ln(xp, wq, k1, k2, u, vt, ln2_g[layer], ln2_b[layer])
        xs = _peer_ln(xs, wq, k1, k2, u, vt, ln2_g[layer], ln2_b[layer])
    stack = lambda lists: tuple(jnp.stack(l) for l in lists)
    return ((xp.reshape(bsz, seq, D_MODEL), xs.reshape(nsm, 1, D_MODEL)) + stack(p_out) + stack(s_out))
```

```python
import functools
import math

import jax
import jax.numpy as jnp
from jax import lax
from jax.experimental import pallas as pl
from jax.experimental.pallas import tpu as pltpu

F32 = jnp.float32
BF16 = jnp.bfloat16

D_MODEL = 1024
DEPTH = 4
PAST_LEN = 8192
S5_WIDTH = 512
S5_GROUP = 16
S5_GROUPS = 32
S5_STATE = 64
S5_FLAT = S5_GROUPS * S5_STATE
SWA_HEADS = 8
SWA_KV_HEADS = 2
SWA_REP = SWA_HEADS // SWA_KV_HEADS
SWA_HEAD_DIM = 64
SWA_WIDTH = SWA_HEADS * SWA_HEAD_DIM
SWA_KV_WIDTH = SWA_KV_HEADS * SWA_HEAD_DIM
WINDOW = 128
SSD_INNER = 2048
SSD_HEAD_DIM = 64
SSD_HEADS = 32
SSD_GROUPS = 4
SSD_HPG = SSD_HEADS // SSD_GROUPS
SSD_GW = SSD_HPG * SSD_HEAD_DIM
SSD_STATE = 128
SSD_CONV = 4
SSD_CHUNK = 128
SSD_CONV_DIM = SSD_INNER + 2 * SSD_GROUPS * SSD_STATE
PEER_HEADS = 8
PEER_KEYS = 128
PEER_EXPERTS = PEER_KEYS * PEER_KEYS
PEER_DK = 256
PEER_TOPK = 16
DN_ALPHA = (2 * DEPTH) ** 0.25
LN_EPS = 1e-5
RMS_EPS = 1e-5
NEG_INF = -1e30

LANES = 128
SUBLANES = 8
MXU_TILE = 256
VMEM_LIMIT = 56 * 1024 * 1024


def _cp(*sem):
    return pltpu.CompilerParams(dimension_semantics=sem, vmem_limit_bytes=VMEM_LIMIT)


def _ln(x, g, b):
    mu = jnp.mean(x, axis=-1, keepdims=True)
    xc = x - mu
    var = jnp.mean(xc * xc, axis=-1, keepdims=True)
    return xc * lax.rsqrt(var + LN_EPS) * g + b


def _softplus(x):
    return jnp.maximum(x, 0.0) + jnp.log(1.0 + jnp.exp(-jnp.abs(x)))


def _silu(x):
    return x * jax.nn.sigmoid(x)


def _dot(a, b):
    return jnp.dot(a, b, preferred_element_type=F32)


def _dot_nt(a, b):
    return lax.dot_general(a, b, (((1,), (1,)), ((), ())), preferred_element_type=F32)


def _dot_tn(a, b):
    return lax.dot_general(a, b, (((0,), (0,)), ((), ())), preferred_element_type=F32)


def _split3(x):
    x1 = x.astype(BF16)
    r1 = x - x1.astype(F32)
    x2 = r1.astype(BF16)
    r2 = r1 - x2.astype(F32)
    return x1, x2, r2.astype(BF16)


def _dot_exact01(x, m01):
    x1, x2, x3 = _split3(x)
    return _dot(x1, m01) + _dot(x2, m01) + _dot(x3, m01)


def _mm_kernel(x_ref, w_ref, o_ref):
    o_ref[...] = _dot(x_ref[...].astype(BF16), w_ref[...]).astype(o_ref.dtype)


def _mm(x, w, tn=512, out_dtype=F32):
    m, k = x.shape
    n = w.shape[1]
    tm = min(m, 512)
    tn = min(n, tn)
    return pl.pallas_call(
        _mm_kernel,
        grid=(m // tm, n // tn),
        in_specs=[pl.BlockSpec((tm, k), lambda i, j: (i, 0)),
                  pl.BlockSpec((k, tn), lambda i, j: (0, j))],
        out_specs=pl.BlockSpec((tm, tn), lambda i, j: (i, j)),
        out_shape=jax.ShapeDtypeStruct((m, n), out_dtype),
        compiler_params=_cp("parallel", "arbitrary"),
        name="mm",
    )(x, w)


def _mm2_ln_kernel(xa_ref, xb_ref, w_ref, r_ref, g_ref, b_ref, o_ref):
    ka = xa_ref.shape[1]
    y = _dot(xa_ref[...], w_ref[0:ka, :]) + _dot(xb_ref[...], w_ref[ka:, :])
    o_ref[...] = _ln(DN_ALPHA * r_ref[...] + y, g_ref[...], b_ref[...])


def _mm2_ln(xa, xb, w, res, g, b):
    m, ka = xa.shape
    kb = xb.shape[1]
    n = w.shape[1]
    tm = min(m, 512)
    return pl.pallas_call(
        _mm2_ln_kernel,
        grid=(m // tm,),
        in_specs=[pl.BlockSpec((tm, ka), lambda i: (i, 0)),
                  pl.BlockSpec((tm, kb), lambda i: (i, 0)),
                  pl.BlockSpec((ka + kb, n), lambda i: (0, 0)),
                  pl.BlockSpec((tm, n), lambda i: (i, 0)),
                  pl.BlockSpec((1, n), lambda i: (0, 0)),
                  pl.BlockSpec((1, n), lambda i: (0, 0))],
        out_specs=pl.BlockSpec((tm, n), lambda i: (i, 0)),
        out_shape=jax.ShapeDtypeStruct((m, n), F32),
        compiler_params=_cp("parallel"),
        name="mm2_ln",
    )(xa, xb, w, res, g.reshape(1, n), b.reshape(1, n))


def _mm_ln_kernel(x_ref, w_ref, r_ref, g_ref, b_ref, o_ref):
    y = _dot(x_ref[...].astype(BF16), w_ref[...])
    o_ref[...] = _ln(DN_ALPHA * r_ref[...] + y, g_ref[...], b_ref[...])


def _mm_ln(x, w, res, g, b):
    m, k = x.shape
    n = w.shape[1]
    tm = min(m, 512)
    return pl.pallas_call(
        _mm_ln_kernel,
        grid=(m // tm,),
        in_specs=[pl.BlockSpec((tm, k), lambda i: (i, 0)),
                  pl.BlockSpec((k, n), lambda i: (0, 0)),
                  pl.BlockSpec((tm, n), lambda i: (i, 0)),
                  pl.BlockSpec((1, n), lambda i: (0, 0)),
                  pl.BlockSpec((1, n), lambda i: (0, 0))],
        out_specs=pl.BlockSpec((tm, n), lambda i: (i, 0)),
        out_shape=jax.ShapeDtypeStruct((m, n), F32),
        compiler_params=_cp("parallel"),
        name="mm_ln",
    )(x, w, res, g.reshape(1, n), b.reshape(1, n))


def _s5_kernel(u_ref, h0_ref, bd_ref, ar_ref, ai_ref, cbd_ref, d_ref, wg_ref,
               y_ref, hT_ref, hbuf, state, *tm_bufs, rows, steps):
    c = pl.program_id(0)

    @pl.when(c == 0)
    def _():
        state[...] = h0_ref[...]

    if tm_bufs:
        ubm, utm, ytm = tm_bufs
        nlb = S5_WIDTH // LANES
        for s in range(rows):
            us = u_ref[s]
            for kb in range(nlb):
                ubm[kb, s * steps:(s + 1) * steps, :] = us[:, kb * LANES:(kb + 1) * LANES]
        for t in range(steps):
            for kb in range(nlb):
                utm[t * rows:(t + 1) * rows, kb * LANES:(kb + 1) * LANES] = ubm[kb, pl.ds(t, rows, stride=steps), :]
        u = utm[...]
    else:
        u = u_ref[...]
    ub = u.astype(BF16)
    for part in range(2):
        for jt in range(S5_FLAT // MXU_TILE):
            c0 = part * S5_FLAT + jt * MXU_TILE
            k0 = (jt * MXU_TILE // S5_STATE * S5_GROUP) // LANES * LANES
            hbuf[:, c0:c0 + MXU_TILE] = _dot(ub[:, k0:k0 + LANES], bd_ref[k0:k0 + LANES, c0:c0 + MXU_TILE])
    ar = ar_ref[...]
    ai = ai_ref[...]

    def step(t, carry):
        hr, hi = carry
        r0 = pl.multiple_of(t * rows, rows)
        br = hbuf[pl.ds(r0, rows), 0:S5_FLAT]
        bi = hbuf[pl.ds(r0, rows), S5_FLAT:2 * S5_FLAT]
        nr = ar * hr - ai * hi + br
        ni = ar * hi + ai * hr + bi
        hbuf[pl.ds(r0, rows), 0:S5_FLAT] = nr
        hbuf[pl.ds(r0, rows), S5_FLAT:2 * S5_FLAT] = ni
        return nr, ni

    hr, hi = lax.fori_loop(0, steps, step, (state[:, 0:S5_FLAT], state[:, S5_FLAT:2 * S5_FLAT]))
    state[:, 0:S5_FLAT] = hr
    state[:, S5_FLAT:2 * S5_FLAT] = hi

    ys = []
    for nt in range(S5_WIDTH // MXU_TILE):
        n0 = nt * MXU_TILE
        kw = MXU_TILE // S5_GROUP * S5_STATE
        acc = None
        for part in range(2):
            k0 = part * S5_FLAT + nt * kw
            term = _dot(hbuf[:, k0:k0 + kw].astype(BF16), cbd_ref[k0:k0 + kw, n0:n0 + MXU_TILE])
            acc = term if acc is None else acc + term
        ys.append(acc)
    y = jnp.concatenate(ys, axis=-1) + d_ref[...] * u
    g = jax.nn.gelu(y)
    out = g * jax.nn.sigmoid(_dot(g.astype(BF16), wg_ref[...]))
    if tm_bufs:
        for kb in range(nlb):
            ytm[kb] = out[:, kb * LANES:(kb + 1) * LANES]
        for s in range(rows):
            y_ref[s] = jnp.concatenate([ytm[kb, pl.ds(s, steps, stride=rows), :] for kb in range(nlb)],
                                       axis=-1).astype(y_ref.dtype)
    else:
        y_ref[...] = out.astype(y_ref.dtype)

    @pl.when(c == pl.num_programs(0) - 1)
    def _():
        hT_ref[...] = state[...]


def _s5(u_tm, h0, prm, rows, steps):
    n = u_tm.shape[0]
    blk = rows * steps
    bd, ar, ai, cbd, d, wg = prm
    const = lambda shape: pl.BlockSpec(shape, lambda c: (0, 0))
    return pl.pallas_call(
        functools.partial(_s5_kernel, rows=rows, steps=steps),
        grid=(n // blk,),
        in_specs=[pl.BlockSpec((blk, S5_WIDTH), lambda c: (c, 0)),
                  const((rows, 2 * S5_FLAT)),
                  const((S5_WIDTH, 2 * S5_FLAT)),
                  const((1, S5_FLAT)), const((1, S5_FLAT)),
                  const((2 * S5_FLAT, S5_WIDTH)),
                  const((1, S5_WIDTH)),
                  const((S5_WIDTH, S5_WIDTH))],
        out_specs=[pl.BlockSpec((blk, S5_WIDTH), lambda c: (c, 0)),
                   const((rows, 2 * S5_FLAT))],
        out_shape=[jax.ShapeDtypeStruct((n, S5_WIDTH), BF16),
                   jax.ShapeDtypeStruct((rows, 2 * S5_FLAT), F32)],
        scratch_shapes=[pltpu.VMEM((blk, 2 * S5_FLAT), F32),
                        pltpu.VMEM((rows, 2 * S5_FLAT), F32)],
        compiler_params=_cp("arbitrary"),
        name="s5",
    )(u_tm, h0, bd, ar, ai, cbd, d, wg)


def _s5_prompt(proj, h0, prm, steps):
    bsz, seq, _ = proj.shape
    blk = bsz * steps
    bd, ar, ai, cbd, d, wg = prm
    const = lambda shape: pl.BlockSpec(shape, lambda c: (0, 0))
    seqblk = pl.BlockSpec((bsz, steps, S5_WIDTH), lambda c: (0, c, 0))
    return pl.pallas_call(
        functools.partial(_s5_kernel, rows=bsz, steps=steps),
        grid=(seq // steps,),
        in_specs=[seqblk,
                  const((bsz, 2 * S5_FLAT)),
                  const((S5_WIDTH, 2 * S5_FLAT)),
                  const((1, S5_FLAT)), const((1, S5_FLAT)),
                  const((2 * S5_FLAT, S5_WIDTH)),
                  const((1, S5_WIDTH)),
                  const((S5_WIDTH, S5_WIDTH))],
        out_specs=[seqblk, const((bsz, 2 * S5_FLAT))],
        out_shape=[jax.ShapeDtypeStruct((bsz, seq, S5_WIDTH), BF16),
                   jax.ShapeDtypeStruct((bsz, 2 * S5_FLAT), F32)],
        scratch_shapes=[pltpu.VMEM((blk, 2 * S5_FLAT), F32),
                        pltpu.VMEM((bsz, 2 * S5_FLAT), F32),
                        pltpu.VMEM((S5_WIDTH // LANES, blk, LANES), F32),
                        pltpu.VMEM((blk, S5_WIDTH), F32),
                        pltpu.VMEM((S5_WIDTH // LANES, blk, LANES), F32)],
        compiler_params=_cp("arbitrary"),
        name="s5_prompt",
    )(proj, h0, bd, ar, ai, cbd, d, wg)


def _s5_params(lam_re, lam_im, log_dt, b_re, b_im, c_re, c_im, d_skip, w_glu):
    dt = jnp.exp(log_dt)[:, None]
    mag = jnp.exp(lam_re * dt)
    ab_re, ab_im = mag * jnp.cos(lam_im * dt), mag * jnp.sin(lam_im * dt)
    den = lam_re * lam_re + lam_im * lam_im
    nr, ni = ab_re - 1.0, ab_im
    coef_re = ((nr * lam_re + ni * lam_im) / den)[..., None]
    coef_im = ((ni * lam_re - nr * lam_im) / den)[..., None]
    bb_re = coef_re * b_re - coef_im * b_im
    bb_im = coef_re * b_im + coef_im * b_re
    eye = jnp.eye(S5_GROUPS, dtype=F32)
    pack_in = lambda bb: jnp.einsum('gpc,gh->gchp', bb, eye).reshape(S5_WIDTH, S5_FLAT)
    bd = jnp.concatenate([pack_in(bb_re), pack_in(bb_im)], axis=1).astype(BF16)
    pack_out = lambda cc: jnp.einsum('gcp,gh->gphc', cc, eye).reshape(S5_FLAT, S5_WIDTH)
    cbd = jnp.concatenate([pack_out(c_re), -pack_out(c_im)], axis=0).astype(BF16)
    return (bd, ab_re.reshape(1, S5_FLAT), ab_im.reshape(1, S5_FLAT), cbd,
            d_skip.reshape(1, S5_WIDTH), w_glu.astype(BF16))


def _swa_kernel(sink_ref, q_ref, kp_ref, kc_ref, vp_ref, vc_ref, o_ref):
    n = pl.program_id(1)
    q = q_ref[0].astype(BF16)
    k = jnp.concatenate([kp_ref[0], kc_ref[0]], axis=0).astype(BF16)
    v = jnp.concatenate([vp_ref[0], vc_ref[0]], axis=0).astype(BF16)
    qi = lax.broadcasted_iota(jnp.int32, (WINDOW, 2 * WINDOW), 0)
    kj = lax.broadcasted_iota(jnp.int32, (WINDOW, 2 * WINDOW), 1)
    dist = WINDOW + qi - kj
    valid = (dist >= 0) & (dist < WINDOW) & ((n > 0) | (kj >= WINDOW))
    distf = dist.astype(F32)
    outs = []
    for h in range(SWA_HEADS):
        g = h // SWA_REP
        qh = q[:, h * SWA_HEAD_DIM:(h + 1) * SWA_HEAD_DIM]
        kg = k[:, g * SWA_HEAD_DIM:(g + 1) * SWA_HEAD_DIM]
        vg = v[:, g * SWA_HEAD_DIM:(g + 1) * SWA_HEAD_DIM]
        s = _dot_nt(qh, kg) * (SWA_HEAD_DIM ** -0.5)
        s = s - (2.0 ** -(h + 1)) * distf
        s = jnp.where(valid, s, NEG_INF)
        sink = sink_ref[h]
        m = jnp.maximum(jnp.max(s, axis=-1, keepdims=True), sink)
        p = jnp.exp(s - m)
        p = p / (jnp.sum(p, axis=-1, keepdims=True) + jnp.exp(sink - m))
        outs.append(_dot(p.astype(BF16), vg))
    o_ref[0] = jnp.concatenate(outs, axis=-1).astype(o_ref.dtype)


def _swa_prompt(proj, sinks):
    bsz, seq, _ = proj.shape
    kcol = (S5_WIDTH + SWA_WIDTH) // SWA_KV_WIDTH
    prev = lambda b, n: (b, jnp.maximum(n - 1, 0), kcol)
    prev_v = lambda b, n: (b, jnp.maximum(n - 1, 0), kcol + 1)
    blk = (1, WINDOW, SWA_KV_WIDTH)
    return pl.pallas_call(
        _swa_kernel,
        grid=(bsz, seq // WINDOW),
        in_specs=[pl.BlockSpec(memory_space=pltpu.SMEM),
                  pl.BlockSpec((1, WINDOW, SWA_WIDTH), lambda b, n: (b, n, 1)),
                  pl.BlockSpec(blk, prev),
                  pl.BlockSpec(blk, lambda b, n: (b, n, kcol)),
                  pl.BlockSpec(blk, prev_v),
                  pl.BlockSpec(blk, lambda b, n: (b, n, kcol + 1))],
        out_specs=pl.BlockSpec((1, WINDOW, SWA_WIDTH), lambda b, n: (b, n, 0)),
        out_shape=jax.ShapeDtypeStruct((bsz, seq, SWA_WIDTH), BF16),
        compiler_params=_cp("parallel", "arbitrary"),
        name="swa_prompt",
    )(sinks, proj, proj, proj, proj, proj)


def _per_head_column(values):
    r = lax.broadcasted_iota(jnp.int32, (SWA_REP, 1), 0)
    col = jnp.full((SWA_REP, 1), values[-1], F32)
    for i in range(SWA_REP - 2, -1, -1):
        col = jnp.where(r == i, values[i], col)
    return col


def _swa_step_kernel(sink_ref, p_ref, kc_ref, vc_ref, o_ref, *, nseq, start):
    kj = lax.broadcasted_iota(jnp.int32, (SWA_REP, WINDOW), 1)
    dist = WINDOW - kj
    valid = (dist < WINDOW) & (start - WINDOW + kj >= 0)
    distf = dist.astype(F32)
    for i in range(nseq):
        row = p_ref[i]
        kn = row[:, S5_WIDTH + SWA_WIDTH:S5_WIDTH + SWA_WIDTH + SWA_KV_WIDTH].astype(BF16)
        vn = row[:, S5_WIDTH + SWA_WIDTH + SWA_KV_WIDTH:].astype(BF16)
        kc = kc_ref[i].astype(BF16)
        vc = vc_ref[i].astype(BF16)
        outs = []
        for g in range(SWA_KV_HEADS):
            q4 = jnp.concatenate(
                [row[:, S5_WIDTH + (g * SWA_REP + r) * SWA_HEAD_DIM:S5_WIDTH + (g * SWA_REP + r + 1) * SWA_HEAD_DIM]
                 for r in range(SWA_REP)], axis=0).astype(BF16)
            sl = slice(g * SWA_HEAD_DIM, (g + 1) * SWA_HEAD_DIM)
            slope = _per_head_column([2.0 ** -(g * SWA_REP + r + 1) for r in range(SWA_REP)])
            sink = _per_head_column([sink_ref[g * SWA_REP + r] for r in range(SWA_REP)])
            scale = SWA_HEAD_DIM ** -0.5
            sc = _dot_nt(q4, kc[:, sl]) * scale - slope * distf
            sc = jnp.where(valid, sc, NEG_INF)
            sn = jnp.sum(q4.astype(F32) * kn[:, sl].astype(F32), axis=-1, keepdims=True) * scale
            m = jnp.maximum(jnp.maximum(jnp.max(sc, axis=-1, keepdims=True), sn), sink)
            pc = jnp.exp(sc - m)
            pn = jnp.exp(sn - m)
            den = jnp.sum(pc, axis=-1, keepdims=True) + pn + jnp.exp(sink - m)
            o4 = _dot((pc / den).astype(BF16), vc[:, sl])
            o4 = o4 + (pn / den).astype(BF16).astype(F32) * vn[:, sl].astype(F32)
            outs.extend([o4[r:r + 1, :] for r in range(SWA_REP)])
        o_ref[i] = jnp.concatenate(outs, axis=-1).astype(o_ref.dtype)


def _swa_step(proj, k_cache, v_cache, sinks, start):
    n = proj.shape[0]
    nseq = min(n, SUBLANES)
    out = pl.pallas_call(
        functools.partial(_swa_step_kernel, nseq=nseq, start=start),
        grid=(n // nseq,),
        in_specs=[pl.BlockSpec(memory_space=pltpu.SMEM),
                  pl.BlockSpec((nseq, 1, proj.shape[1]), lambda i: (i, 0, 0)),
                  pl.BlockSpec((nseq, WINDOW, SWA_KV_WIDTH), lambda i: (i, 0, 0)),
                  pl.BlockSpec((nseq, WINDOW, SWA_KV_WIDTH), lambda i: (i, 0, 0))],
        out_specs=pl.BlockSpec((nseq, 1, SWA_WIDTH), lambda i: (i, 0, 0)),
        out_shape=jax.ShapeDtypeStruct((n, 1, SWA_WIDTH), BF16),
        compiler_params=_cp("parallel"),
        name="swa_step",
    )(sinks, proj.reshape(n, 1, -1), k_cache, v_cache)
    return out.reshape(n, SWA_WIDTH)


def _head_expand(width):
    r = lax.broadcasted_iota(jnp.int32, (LANES, SSD_HEADS * width), 0)
    c = lax.broadcasted_iota(jnp.int32, (LANES, SSD_HEADS * width), 1)
    return jnp.where((c >= r * width) & (c < (r + 1) * width), 1.0, 0.0).astype(BF16)


def _ssd_kernel(x_ref, wz_ref, wx_ref, wdt_ref, cw_ref, cb_ref, dtb_ref, a_ref, d_ref, nw_ref,
                wo_ref, g_ref, b_ref, o_ref, st_ref, conv_ref, tail, state, ybuf):
    c = pl.program_id(1)
    q = SSD_CHUNK

    @pl.when(c == 0)
    def _():
        tail[...] = jnp.zeros_like(tail)
        state[...] = jnp.zeros_like(state)

    xin = x_ref[0]
    xb = xin.astype(BF16)
    cur = _dot(xb, wx_ref[...])
    tl = tail[...]
    row8 = lax.broadcasted_iota(jnp.int32, (SUBLANES, SSD_CONV_DIM), 0)
    acc = None
    for tap in range(SSD_CONV):
        sh = SSD_CONV - 1 - tap
        if sh == 0:
            term = cur
        else:
            rolled = pltpu.roll(cur, sh, 0)
            head = jnp.where(row8 < sh, pltpu.roll(tl, sh, 0), rolled[0:SUBLANES])
            term = jnp.concatenate([head, rolled[SUBLANES:]], axis=0)
        term = term * cw_ref[tap:tap + 1, :]
        acc = term if acc is None else acc + term
    tail[...] = cur[q - SUBLANES:q]
    xbc = _silu(acc + cb_ref[...])
    xs = xbc[:, 0:SSD_INNER]
    bm = xbc[:, SSD_INNER:SSD_INNER + SSD_GROUPS * SSD_STATE].astype(BF16)
    cm = xbc[:, SSD_INNER + SSD_GROUPS * SSD_STATE:].astype(BF16)

    dt = _softplus(_dot(xb, wdt_ref[...]) + dtb_ref[...])
    da = dt * a_ref[...]
    ri = lax.broadcasted_iota(jnp.int32, (q, q), 0)
    ci = lax.broadcasted_iota(jnp.int32, (q, q), 1)
    causal = ri >= ci
    tri = jnp.where(causal, 1.0, 0.0).astype(BF16)
    da1, da2, da3 = _split3(da)
    acs = _dot(tri, da1) + _dot(tri, da2) + _dot(tri, da3)
    acs_t = acs.T
    e64 = _head_expand(SSD_HEAD_DIM)
    dt_x = _dot_exact01(dt, e64)
    acs_x = _dot_exact01(acs, e64)
    last_x = acs_x[q - 1:q, :]
    dx = dt_x * xs
    dsx = (jnp.exp(last_x - acs_x) * dx).astype(BF16)
    eacs_x = jnp.exp(acs_x)
    dxb = dx.astype(BF16)

    for g in range(SSD_GROUPS):
        ns = slice(g * SSD_STATE, (g + 1) * SSD_STATE)
        gs = slice(g * SSD_GW, (g + 1) * SSD_GW)
        cb = _dot_nt(cm[:, ns], bm[:, ns])
        for r in range(SSD_HPG):
            h = g * SSD_HPG + r
            seg = acs[:, h:h + 1] - acs_t[h:h + 1, :]
            decay = jnp.where(causal, jnp.exp(jnp.where(causal, seg, 0.0)), 0.0)
            hs = slice(h * SSD_HEAD_DIM, (h + 1) * SSD_HEAD_DIM)
            ybuf[:, hs] = _dot((cb * decay).astype(BF16), dxb[:, hs])
        s_in = state[g]
        y_off = _dot(cm[:, ns], s_in.astype(BF16)) * eacs_x[:, gs]
        ybuf[:, gs] = ybuf[:, gs] + y_off
        state[g] = jnp.exp(last_x[:, gs]) * s_in + _dot_tn(bm[:, ns], dsx[:, gs])

    y = ybuf[...] + d_ref[...] * xs
    y = y * _silu(_dot(xb, wz_ref[...]))
    y = y * lax.rsqrt(jnp.mean(y * y, axis=-1, keepdims=True) + RMS_EPS) * nw_ref[...]
    mix = _dot(y.astype(BF16), wo_ref[...])
    o_ref[0] = _ln(DN_ALPHA * xin + mix, g_ref[...], b_ref[...])

    @pl.when(c == pl.num_programs(1) - 1)
    def _():
        conv_ref[0] = cur[q - SUBLANES:q]
        for g in range(SSD_GROUPS):
            st_ref[0, g * SSD_GW:(g + 1) * SSD_GW, :] = state[g].T


def _ssd_prompt(x, w3, prm, w_out, g, b):
    bsz, seq, _ = x.shape
    w_z, w_x, w_dt = w3
    cw, cb, dtb, a, d_x, nw = prm
    const = lambda shape: pl.BlockSpec(shape, lambda b, c: (0, 0))
    seqblk = lambda w: pl.BlockSpec((1, SSD_CHUNK, w), lambda b, c: (b, c, 0))
    return pl.pallas_call(
        _ssd_kernel,
        grid=(bsz, seq // SSD_CHUNK),
        in_specs=[seqblk(D_MODEL),
                  const((D_MODEL, SSD_INNER)), const((D_MODEL, SSD_CONV_DIM)), const((D_MODEL, LANES)),
                  const((SSD_CONV, SSD_CONV_DIM)), const((1, SSD_CONV_DIM)),
                  const((1, LANES)), const((1, LANES)),
                  const((1, SSD_INNER)), const((1, SSD_INNER)),
                  const((SSD_INNER, D_MODEL)), const((1, D_MODEL)), const((1, D_MODEL))],
        out_specs=[seqblk(D_MODEL),
                   pl.BlockSpec((1, SSD_INNER, SSD_STATE), lambda b, c: (b, 0, 0)),
                   pl.BlockSpec((1, SUBLANES, SSD_CONV_DIM), lambda b, c: (b, 0, 0))],
        out_shape=[jax.ShapeDtypeStruct((bsz, seq, D_MODEL), F32),
                   jax.ShapeDtypeStruct((bsz, SSD_INNER, SSD_STATE), F32),
                   jax.ShapeDtypeStruct((bsz, SUBLANES, SSD_CONV_DIM), F32)],
        scratch_shapes=[pltpu.VMEM((SUBLANES, SSD_CONV_DIM), F32),
                        pltpu.VMEM((SSD_GROUPS, SSD_STATE, SSD_GW), F32),
                        pltpu.VMEM((SSD_CHUNK, SSD_INNER), F32)],
        compiler_params=_cp("parallel", "arbitrary"),
        name="ssd_prompt",
    )(x, w_z, w_x, w_dt, cw, cb, dtb, a, d_x, nw, w_out, g.reshape(1, -1), b.reshape(1, -1))


def _pad_rows(row):
    return jnp.concatenate([row, jnp.zeros((SUBLANES - 1, row.shape[1]), row.dtype)], axis=0)


def _ssd_step_kernel(z_ref, x_ref, dt_ref, buf_ref, s0_ref, cw_ref, cb_ref, dtb_ref, a_ref, d_ref, nw_ref,
                     y_ref, s1_ref, *, nseq):
    e64 = _head_expand(SSD_HEAD_DIM)
    ones3 = jnp.where(lax.broadcasted_iota(jnp.int32, (SUBLANES, SSD_STATE), 0) < 3, 1.0, 0.0).astype(BF16)
    for i in range(nseq):
        buf = buf_ref[i]
        acc = buf[0:1] * cw_ref[0:1, :]
        acc = acc + buf[1:2] * cw_ref[1:2, :]
        acc = acc + buf[2:3] * cw_ref[2:3, :]
        acc = acc + x_ref[i] * cw_ref[3:4, :]
        xbc = _silu(acc + cb_ref[...])
        xs = xbc[:, 0:SSD_INNER]
        bm = xbc[:, SSD_INNER:SSD_INNER + SSD_GROUPS * SSD_STATE].astype(BF16)
        cm = xbc[:, SSD_INNER + SSD_GROUPS * SSD_STATE:].astype(BF16)
        dt = _softplus(dt_ref[i] + dtb_ref[...])
        ed = jnp.exp(dt * a_ref[...])
        dt_x = _dot_exact01(_pad_rows(dt), e64)[0:1]
        ed_x = _dot_exact01(_pad_rows(ed), e64)[0:1]
        dx = dt_x * xs
        dxb = dx.astype(BF16)
        ys = []
        for g in range(SSD_GROUPS):
            ns = slice(g * SSD_STATE, (g + 1) * SSD_STATE)
            gs = slice(g * SSD_GW, (g + 1) * SSD_GW)
            s0 = s0_ref[i, gs, :]
            cmg = _pad_rows(cm[:, ns])
            cbv = jnp.sum(cm[:, ns].astype(F32) * bm[:, ns].astype(F32), axis=-1, keepdims=True)
            y_diag = cbv.astype(BF16).astype(F32) * dxb[:, gs].astype(F32)
            y_off = _dot_nt(cmg, s0.astype(BF16))[0:1] * ed_x[:, gs]
            ys.append(y_diag + y_off)
            e1, e2, e3 = _split3(ed_x[:, gs])
            ed_rows = jnp.concatenate([e1, e2, e3, jnp.zeros((SUBLANES - 3, SSD_GW), BF16)], axis=0)
            ed_full = _dot_tn(ed_rows, ones3)
            outer = _dot_tn(_pad_rows(dxb[:, gs]), _pad_rows(bm[:, ns]))
            s1_ref[i, gs, :] = ed_full * s0 + outer
        y = jnp.concatenate(ys, axis=-1) + d_ref[...] * xs
        y = y * _silu(z_ref[i])
        y = y * lax.rsqrt(jnp.mean(y * y, axis=-1, keepdims=True) + RMS_EPS) * nw_ref[...]
        y_ref[i] = y.astype(y_ref.dtype)


def _ssd_step(z, xbc, dtraw, conv_buf, s0_all, layer, prm):
    n = z.shape[0]
    nseq = min(n, 4)
    cw, cb, dtb, a, d_x, nw = prm
    const = lambda shape: pl.BlockSpec(shape, lambda i: (0, 0))
    tok = lambda w: pl.BlockSpec((nseq, 1, w), lambda i: (i, 0, 0))
    y, s1 = pl.pallas_call(
        functools.partial(_ssd_step_kernel, nseq=nseq),
        grid=(n // nseq,),
        in_specs=[tok(SSD_INNER), tok(SSD_CONV_DIM), tok(LANES),
                  pl.BlockSpec((nseq, SSD_CONV - 1, SSD_CONV_DIM), lambda i: (i, 0, 0)),
                  pl.BlockSpec((None, nseq, SSD_INNER, SSD_STATE), lambda i: (layer, i, 0, 0)),
                  const((SSD_CONV, SSD_CONV_DIM)), const((1, SSD_CONV_DIM)),
                  const((1, LANES)), const((1, LANES)),
                  const((1, SSD_INNER)), const((1, SSD_INNER))],
        out_specs=[tok(SSD_INNER),
                   pl.BlockSpec((nseq, SSD_INNER, SSD_STATE), lambda i: (i, 0, 0))],
        out_shape=[jax.ShapeDtypeStruct((n, 1, SSD_INNER), BF16),
                   jax.ShapeDtypeStruct((n, SSD_INNER, SSD_STATE), F32)],
        compiler_params=_cp("parallel"),
        name="ssd_step",
    )(z.reshape(n, 1, -1), xbc.reshape(n, 1, -1), dtraw.reshape(n, 1, -1), conv_buf, s0_all,
      cw, cb, dtb, a, d_x, nw)
    return y.reshape(n, SSD_INNER), s1


def _ssd_params(conv_w, conv_b, dt_bias, a_log, d_skip, norm_w):
    pad = lambda v: jnp.pad(v, (0, LANES - SSD_HEADS)).reshape(1, LANES)
    return (conv_w, conv_b.reshape(1, -1), pad(dt_bias), pad(-jnp.exp(a_log)),
            jnp.repeat(d_skip, SSD_HEAD_DIM).reshape(1, SSD_INNER), norm_w.reshape(1, SSD_INNER))


PEER_TE = 2048
PEER_ROWS = PEER_TE // PEER_KEYS


PEER_CHUNK_ROWS = 4
PEER_TOK_CHUNK = 256
PACK = 2 * SUBLANES


def _batcher_sort_pairs(n):
    def merge(lo, hi, r):
        step = r * 2
        if step < hi - lo:
            yield from merge(lo, hi, step)
            yield from merge(lo + r, hi, step)
            for i in range(lo + r, hi - r, step):
                yield (i, i + r)
        else:
            yield (lo, lo + r)

    def sort(lo, hi):
        if hi - lo >= 1:
            mid = lo + (hi - lo) // 2
            yield from sort(lo, mid)
            yield from sort(mid + 1, hi)
            yield from merge(lo, hi, 1)

    return tuple(sort(0, n - 1))


_SORT16 = _batcher_sort_pairs(PEER_TOPK)
_BITONIC16 = tuple((i, i + d) for d in (8, 4, 2, 1) for i in range(PEER_TOPK) if not i & d)


def _mx(a, b):
    return b if a is None else a if b is None else jnp.maximum(a, b)


def _mn(a, b):
    return None if a is None or b is None else jnp.minimum(a, b)


def _top16(rows):
    v = list(rows) + [None] * (PEER_TOPK - len(rows))
    for i, j in _SORT16:
        v[i], v[j] = _mx(v[i], v[j]), _mn(v[i], v[j])
    for shift in (4, 2, 1):
        w = [None if x is None else pltpu.roll(x, shift, 0) for x in v]
        v = [_mx(v[k], w[PEER_TOPK - 1 - k]) for k in range(PEER_TOPK)]
        for i, j in _BITONIC16:
            v[i], v[j] = _mx(v[i], v[j]), _mn(v[i], v[j])
    return v


def _prefix_count(test, thr):
    m8 = test(thr[7])
    m4 = test(jnp.where(m8, thr[11], thr[3]))
    m2 = test(jnp.where(m8, jnp.where(m4, thr[13], thr[9]), jnp.where(m4, thr[5], thr[1])))
    hi = jnp.where(m4, jnp.where(m2, thr[14], thr[12]), jnp.where(m2, thr[10], thr[8]))
    lo = jnp.where(m4, jnp.where(m2, thr[6], thr[4]), jnp.where(m2, thr[2], thr[0]))
    m1 = test(jnp.where(m8, hi, lo))
    return (jnp.where(m8, 8.0, 0.0) + jnp.where(m4, 4.0, 0.0)) + (jnp.where(m2, 2.0, 0.0) + jnp.where(m1, 1.0, 0.0))


def _peer_route(xb, wq_ref, k1_ref, k2_ref, c_ref, e1_ref, rk_ref, e2_ref):
    tm = xb.shape[0]
    q = _dot(xb, wq_ref[...]).astype(BF16)
    half = PEER_DK // 2
    nblk = PEER_KEYS // SUBLANES
    sub = lax.broadcasted_iota(jnp.int32, (SUBLANES, tm), 0)

    def by_sublane(vals):
        out = vals[SUBLANES - 1]
        for s in range(SUBLANES - 2, -1, -1):
            out = jnp.where(sub == s, vals[s], out)
        return out

    for h in range(PEER_HEADS):
        s1 = _dot_nt(k1_ref[h], q[:, h * PEER_DK:h * PEER_DK + half])
        s2 = _dot_nt(k2_ref[h], q[:, h * PEER_DK + half:(h + 1) * PEER_DK])
        r1 = [s1[k * SUBLANES:(k + 1) * SUBLANES] for k in range(nblk)]
        r2 = [s2[k * SUBLANES:(k + 1) * SUBLANES] for k in range(nblk)]
        a = _top16(r1)
        b = _top16(r2)
        r1 = [x - a[0] for x in r1]
        r2 = [x - b[0] for x in r2]
        a = [x - a[0] for x in a]
        b = [x - b[0] for x in b]
        b_lo, b_hi, a_hi = by_sublane(b[0:8]), by_sublane(b[8:16]), by_sublane(a[8:16])
        cand = [a[0] + b_lo, a[0] + b_hi] + [a[i] + b_lo for i in range(1, 8)] + [a_hi + b[0]]
        top = _top16(cand)
        tau = top[PEER_TOPK - 1]
        zsum = jnp.exp(top[0])
        for k in range(1, PEER_TOPK):
            zsum = zsum + jnp.exp(top[k])
        inv_z = 1.0 / zsum
        for k in range(nblk):
            x1 = r1[k]
            cnt = _prefix_count(lambda t: (x1 + t) >= tau, b[0:15])
            cnt = cnt + jnp.where((x1 + b[15]) >= tau, 1.0, 0.0)
            c_ref[h, k, :, 0:tm] = cnt
            e1_ref[h, k, :, 0:tm] = jnp.exp(x1) * inv_z
        ranks, e2s = [], []
        for k in range(nblk):
            x2 = r2[k]
            rnk = _prefix_count(lambda t: t > x2, b[0:15])
            ranks.append(jnp.where(x2 >= b[15], rnk, float(PEER_TOPK)))
            e2s.append(jnp.exp(x2))
        rk_ref[h, :, 0:tm] = jnp.concatenate(ranks, axis=0).astype(BF16)
        e2_ref[h, :, 0:tm] = jnp.concatenate(e2s, axis=0).astype(BF16)


def _gelu_tanh(x):
    c0 = math.sqrt(2.0 / math.pi)
    return (0.5 * x) * (1.0 + jnp.tanh(x * (c0 + (c0 * 0.044715) * (x * x))))


def _peer_kernel(x_ref, xp_ref, wq_ref, k1_ref, k2_ref, u_ref, vt_ref, g_ref, b_ref, o_ref,
                 acc, c_ref, e1_ref, rk_ref, e2_ref, act_ref):
    i = pl.program_id(0)
    j = pl.program_id(1)
    nblk = pl.num_programs(0) - 1
    tm = x_ref.shape[0]
    xb = x_ref[...].astype(BF16)

    @pl.when((i == 0) & (j == 0))
    def _():
        acc[...] = jnp.zeros_like(acc)
        act_ref[1] = jnp.zeros((PEER_TE, tm), BF16)

    @pl.when((j == 0) & (i < nblk))
    def _():
        _peer_route(xb, wq_ref, k1_ref, k2_ref, c_ref, e1_ref, rk_ref, e2_ref)

    nb2 = PEER_KEYS // PACK
    slot = j % 2
    crows = PEER_CHUNK_ROWS
    nch = PEER_ROWS // crows
    tc = min(tm, PEER_TOK_CHUNK)
    apply_after = {nch // 2 - 1 + k: k * tc for k in range(tm // tc)}

    @pl.when((i < nblk) | (j == 0))
    def _():
        for ch in range(nch):
            e0 = ch * crows * PEER_KEYS
            h_t = _dot_nt(u_ref[e0:e0 + crows * PEER_KEYS, :], xb)
            if ch in apply_after:
                t0 = apply_after[ch]
                acc[:, t0:t0 + tc] += _dot(vt_ref[...], act_ref[1 - slot, :, t0:t0 + tc])
            rows = [ch * crows + ri for ri in range(crows)]
            for tl in range(tm // LANES):
                ts = slice(tl * LANES, (tl + 1) * LANES)
                gate = [[None] * nb2 for _ in range(crows)]
                for h in range(PEER_HEADS):
                    blk = [j * (PEER_ROWS // SUBLANES) + r // SUBLANES for r in rows]
                    sub = [r % SUBLANES for r in rows]
                    cbv = [jnp.broadcast_to(c_ref[h, k, s:s + 1, ts], (PACK, LANES)).astype(BF16)
                           for k, s in zip(blk, sub)]
                    ebv = [jnp.broadcast_to(e1_ref[h, k, s:s + 1, ts], (PACK, LANES)).astype(BF16)
                           for k, s in zip(blk, sub)]
                    for b2 in range(nb2):
                        ks = slice(b2 * PACK, (b2 + 1) * PACK)
                        rk = rk_ref[h, ks, ts]
                        e2 = e2_ref[h, ks, ts]
                        for ri in range(crows):
                            w = jnp.where(rk < cbv[ri], ebv[ri] * e2, 0.0)
                            gate[ri][b2] = w if gate[ri][b2] is None else gate[ri][b2] + w
                for ri in range(crows):
                    for b2 in range(nb2):
                        lo = ri * PEER_KEYS + b2 * PACK
                        hv = _gelu_tanh(h_t[lo:lo + PACK, ts].astype(BF16))
                        act_ref[slot, e0 + lo:e0 + lo + PACK, ts] = hv * gate[ri][b2]

    @pl.when(j == 0)
    def _():
        o_ref[...] = _ln(DN_ALPHA * xp_ref[...] + acc[...].T, g_ref[...], b_ref[...])
        acc[...] = jnp.zeros_like(acc)


def _peer_ln(x, wq, k1, k2, u, vt, g, b):
    t = x.shape[0]
    tm = min(t, 512)
    nblk = t // tm
    ntile = PEER_EXPERTS // PEER_TE
    const2 = lambda shape: pl.BlockSpec(shape, lambda i, j: (0, 0))
    const3 = lambda shape: pl.BlockSpec(shape, lambda i, j: (0, 0, 0))
    cur = lambda i, j: (jnp.minimum(i, nblk - 1), 0)
    prev = lambda i, j: (jnp.maximum(i - 1, 0), 0)
    return pl.pallas_call(
        _peer_kernel,
        grid=(nblk + 1, ntile),
        in_specs=[pl.BlockSpec((tm, D_MODEL), cur),
                  pl.BlockSpec((tm, D_MODEL), prev),
                  const2((D_MODEL, PEER_HEADS * PEER_DK)),
                  const3((PEER_HEADS, PEER_KEYS, PEER_DK // 2)),
                  const3((PEER_HEADS, PEER_KEYS, PEER_DK // 2)),
                  pl.BlockSpec((PEER_TE, D_MODEL), lambda i, j: (jnp.where(i < nblk, j, 0), 0)),
                  pl.BlockSpec((D_MODEL, PEER_TE),
                               lambda i, j: (0, jnp.where(i < nblk, (j + ntile - 1) % ntile, ntile - 1))),
                  const2((1, D_MODEL)), const2((1, D_MODEL))],
        out_specs=pl.BlockSpec((tm, D_MODEL), lambda i, j: (jnp.where(j == 0, prev(i, j)[0], cur(i, j)[0]), 0)),
        out_shape=jax.ShapeDtypeStruct((t, D_MODEL), F32),
        scratch_shapes=[pltpu.VMEM((D_MODEL, tm), F32),
                        pltpu.VMEM((PEER_HEADS, PEER_KEYS // SUBLANES, SUBLANES, tm), F32),
                        pltpu.VMEM((PEER_HEADS, PEER_KEYS // SUBLANES, SUBLANES, tm), F32),
                        pltpu.VMEM((PEER_HEADS, PEER_KEYS, tm), BF16),
                        pltpu.VMEM((PEER_HEADS, PEER_KEYS, tm), BF16),
                        pltpu.VMEM((2, PEER_TE, tm), BF16)],
        compiler_params=_cp("arbitrary", "arbitrary"),
        name="peer",
    )(x, x, wq, k1, k2, u, vt, g.reshape(1, -1), b.reshape(1, -1))


def _even_prompt(x, bsz, seq, w_in, s5p, sinks, w_out, g, b):
    proj = _mm(x, w_in, tn=640)
    p3 = proj.reshape(bsz, seq, -1)
    y_a, h_t = _s5_prompt(p3, jnp.zeros((bsz, 2 * S5_FLAT), F32), s5p, min(seq, 64))
    y_a = y_a.reshape(bsz * seq, S5_WIDTH)
    y_b = _swa_prompt(p3, sinks).reshape(bsz * seq, SWA_WIDTH)
    x = _mm2_ln(y_a, y_b, w_out, x, g, b)
    kv = p3[:, seq - WINDOW:, S5_WIDTH + SWA_WIDTH:]
    new_k = kv[..., :SWA_KV_WIDTH].reshape(bsz, WINDOW, SWA_KV_HEADS, SWA_HEAD_DIM)
    new_v = kv[..., SWA_KV_WIDTH:].reshape(bsz, WINDOW, SWA_KV_HEADS, SWA_HEAD_DIM)
    h_re = h_t[:, :S5_FLAT].reshape(bsz, S5_GROUPS, S5_STATE)
    h_im = h_t[:, S5_FLAT:].reshape(bsz, S5_GROUPS, S5_STATE)
    return x, h_re, h_im, new_k, new_v


def _even_sample(x, h_re, h_im, k_buf, v_buf, start, w_in, s5p, sinks, w_out, g, b):
    n = x.shape[0]
    proj = _mm(x, w_in, tn=640)
    h0 = jnp.concatenate([h_re.reshape(n, S5_FLAT), h_im.reshape(n, S5_FLAT)], axis=1)
    y_a, h_t = _s5(proj[:, :S5_WIDTH], h0, s5p, n, 1)
    kc = k_buf.reshape(n, WINDOW, SWA_KV_WIDTH)
    vc = v_buf.reshape(n, WINDOW, SWA_KV_WIDTH)
    y_b = _swa_step(proj, kc, vc, sinks, start)
    x = _mm2_ln(y_a, y_b, w_out, x, g, b)
    kn = proj[:, None, S5_WIDTH + SWA_WIDTH:S5_WIDTH + SWA_WIDTH + SWA_KV_WIDTH]
    vn = proj[:, None, S5_WIDTH + SWA_WIDTH + SWA_KV_WIDTH:]
    new_k = jnp.concatenate([kc[:, 1:], kn], axis=1).reshape(k_buf.shape)
    new_v = jnp.concatenate([vc[:, 1:], vn], axis=1).reshape(v_buf.shape)
    return (x, h_t[:, :S5_FLAT].reshape(h_re.shape), h_t[:, S5_FLAT:].reshape(h_im.shape), new_k, new_v)


def _odd_weights(w_in):
    w_z = w_in[:, :SSD_INNER].astype(BF16)
    w_x = w_in[:, SSD_INNER:SSD_INNER + SSD_CONV_DIM].astype(BF16)
    w_dt = jnp.pad(w_in[:, SSD_INNER + SSD_CONV_DIM:], ((0, 0), (0, LANES - SSD_HEADS))).astype(BF16)
    return w_z, w_x, w_dt


def _odd_prompt(x, bsz, seq, w3, ssdp, w_out, g, b):
    x, st, tail = _ssd_prompt(x.reshape(bsz, seq, -1), w3, ssdp, w_out, g, b)
    return (x.reshape(bsz * seq, -1), st.reshape(bsz, SSD_HEADS, SSD_HEAD_DIM, SSD_STATE),
            tail[:, SUBLANES - (SSD_CONV - 1):])


def _odd_sample(x, ssm_all, layer, conv, w3, ssdp, w_out, g, b):
    n = x.shape[0]
    w_z, w_x, w_dt = w3
    z = _mm(x, w_z)
    xbc = _mm(x, w_x)
    dtraw = _mm(x, w_dt)
    s0_all = ssm_all.reshape(ssm_all.shape[0], n, SSD_INNER, SSD_STATE)
    y, st = _ssd_step(z, xbc, dtraw, conv, s0_all, layer, ssdp)
    x = _mm_ln(y, w_out, x, g, b)
    new_conv = jnp.concatenate([conv[:, 1:], xbc[:, None, :]], axis=1)
    return x, st.reshape(ssm_all.shape[1:]), new_conv


def kernel(x_prompt, x_sample, state_s5_re, state_s5_im, cache_swa_k, cache_swa_v, state_ssd, state_conv,
           w_in_even, s5_lambda_re, s5_lambda_im, s5_log_dt, s5_b_re, s5_b_im, s5_c_re, s5_c_im, s5_d,
           s5_w_glu, swa_sinks, w_out_even,
           w_in_odd, ssd_conv_w, ssd_conv_b, ssd_dt_bias, ssd_a_log, ssd_d, ssd_norm_w, w_out_odd,
           ln1_g, ln1_b, ln2_g, ln2_b, peer_w_q, peer_k1, peer_k2, peer_u, peer_v):
    bsz, seq, _ = x_prompt.shape
    nsm = x_sample.shape[0]
    xp = x_prompt.reshape(bsz * seq, D_MODEL)
    xs = x_sample.reshape(nsm, D_MODEL)
    p_out = [[] for _ in range(6)]
    s_out = [[] for _ in range(6)]
    for layer in range(DEPTH):
        i = layer // 2
        if layer % 2 == 0:
            w_in = w_in_even[i].astype(BF16)
            w_out = w_out_even[i].astype(BF16)
            s5p = _s5_params(s5_lambda_re[i], s5_lambda_im[i], s5_log_dt[i], s5_b_re[i], s5_b_im[i],
                             s5_c_re[i], s5_c_im[i], s5_d[i], s5_w_glu[i])
            xp, hr, hi, nk, nv = _even_prompt(xp, bsz, seq, w_in, s5p, swa_sinks[i], w_out,
                                              ln1_g[layer], ln1_b[layer])
            for lst, val in zip(p_out[:4], (hr, hi, nk, nv)):
                lst.append(val)
            xs, hr, hi, nk, nv = _even_sample(xs, state_s5_re[i], state_s5_im[i], cache_swa_k[i], cache_swa_v[i],
                                              PAST_LEN, w_in, s5p, swa_sinks[i], w_out,
                                              ln1_g[layer], ln1_b[layer])
            for lst, val in zip(s_out[:4], (hr, hi, nk, nv)):
                lst.append(val)
        else:
            w3 = _odd_weights(w_in_odd[i])
            w_out = w_out_odd[i].astype(BF16)
            ssdp = _ssd_params(ssd_conv_w[i], ssd_conv_b[i], ssd_dt_bias[i], ssd_a_log[i], ssd_d[i],
                               ssd_norm_w[i])
            xp, st, cb = _odd_prompt(xp, bsz, seq, w3, ssdp, w_out, ln1_g[layer], ln1_b[layer])
            p_out[4].append(st)
            p_out[5].append(cb)
            xs, st, cb = _odd_sample(xs, state_ssd, i, state_conv[i], w3, ssdp, w_out,
                                     ln1_g[layer], ln1_b[layer])
            s_out[4].append(st)
            s_out[5].append(cb)
        wq = peer_w_q[layer].astype(BF16)
        k1 = peer_k1[layer].astype(BF16)
        k2 = peer_k2[layer].astype(BF16)
        u = peer_u[layer].astype(BF16)
        vt = peer_v[layer].astype(BF16).T
        xp = _peer_ln(xp, wq, k1, k2, u, vt, ln2_g[layer], ln2_b[layer])
        xs = _peer_ln(xs, wq, k1, k2, u, vt, ln2_g[layer], ln2_b[layer])
    stack = lambda lists: tuple(jnp.stack(l) for l in lists)
    return ((xp.reshape(bsz, seq, D_MODEL), xs.reshape(nsm, 1, D_MODEL)) + stack(p_out) + stack(s_out))
```

```python
import functools
import math

import jax
import jax.numpy as jnp
from jax import lax
from jax.experimental import pallas as pl
from jax.experimental.pallas import tpu as pltpu

F32 = jnp.float32
BF16 = jnp.bfloat16

D_MODEL = 1024
DEPTH = 4
PAST_LEN = 8192
S5_WIDTH = 512
S5_GROUP = 16
S5_GROUPS = 32
S5_STATE = 64
S5_FLAT = S5_GROUPS * S5_STATE
SWA_HEADS = 8
SWA_KV_HEADS = 2
SWA_REP = SWA_HEADS // SWA_KV_HEADS
SWA_HEAD_DIM = 64
SWA_WIDTH = SWA_HEADS * SWA_HEAD_DIM
SWA_KV_WIDTH = SWA_KV_HEADS * SWA_HEAD_DIM
WINDOW = 128
SSD_INNER = 2048
SSD_HEAD_DIM = 64
SSD_HEADS = 32
SSD_GROUPS = 4
SSD_HPG = SSD_HEADS // SSD_GROUPS
SSD_GW = SSD_HPG * SSD_HEAD_DIM
SSD_STATE = 128
SSD_CONV = 4
SSD_CHUNK = 128
SSD_CONV_DIM = SSD_INNER + 2 * SSD_GROUPS * SSD_STATE
PEER_HEADS = 8
PEER_KEYS = 128
PEER_EXPERTS = PEER_KEYS * PEER_KEYS
PEER_DK = 256
PEER_TOPK = 16
DN_ALPHA = (2 * DEPTH) ** 0.25
LN_EPS = 1e-5
RMS_EPS = 1e-5
NEG_INF = -1e30

LANES = 128
SUBLANES = 8
MXU_TILE = 256
VMEM_LIMIT = 56 * 1024 * 1024


def _cp(*sem):
    return pltpu.CompilerParams(dimension_semantics=sem, vmem_limit_bytes=VMEM_LIMIT)


def _ln(x, g, b):
    mu = jnp.mean(x, axis=-1, keepdims=True)
    xc = x - mu
    var = jnp.mean(xc * xc, axis=-1, keepdims=True)
    return xc * lax.rsqrt(var + LN_EPS) * g + b


def _softplus(x):
    return jnp.maximum(x, 0.0) + jnp.log(1.0 + jnp.exp(-jnp.abs(x)))


def _silu(x):
    return x * jax.nn.sigmoid(x)


def _dot(a, b):
    return jnp.dot(a, b, preferred_element_type=F32)


def _dot_nt(a, b):
    return lax.dot_general(a, b, (((1,), (1,)), ((), ())), preferred_element_type=F32)


def _dot_tn(a, b):
    return lax.dot_general(a, b, (((0,), (0,)), ((), ())), preferred_element_type=F32)


def _split3(x):
    x1 = x.astype(BF16)
    r1 = x - x1.astype(F32)
    x2 = r1.astype(BF16)
    r2 = r1 - x2.astype(F32)
    return x1, x2, r2.astype(BF16)


def _dot_exact01(x, m01):
    x1, x2, x3 = _split3(x)
    return _dot(x1, m01) + _dot(x2, m01) + _dot(x3, m01)


def _mm_kernel(x_ref, w_ref, o_ref):
    o_ref[...] = _dot(x_ref[...].astype(BF16), w_ref[...]).astype(o_ref.dtype)


def _mm(x, w, tn=512, out_dtype=F32):
    m, k = x.shape
    n = w.shape[1]
    tm = min(m, 512)
    tn = min(n, tn)
    return pl.pallas_call(
        _mm_kernel,
        grid=(m // tm, n // tn),
        in_specs=[pl.BlockSpec((tm, k), lambda i, j: (i, 0)),
                  pl.BlockSpec((k, tn), lambda i, j: (0, j))],
        out_specs=pl.BlockSpec((tm, tn), lambda i, j: (i, j)),
        out_shape=jax.ShapeDtypeStruct((m, n), out_dtype),
        compiler_params=_cp("parallel", "arbitrary"),
        name="mm",
    )(x, w)


def _mm2_ln_kernel(xa_ref, xb_ref, w_ref, r_ref, g_ref, b_ref, o_ref):
    ka = xa_ref.shape[1]
    y = _dot(xa_ref[...], w_ref[0:ka, :]) + _dot(xb_ref[...], w_ref[ka:, :])
    o_ref[...] = _ln(DN_ALPHA * r_ref[...] + y, g_ref[...], b_ref[...])


def _mm2_ln(xa, xb, w, res, g, b):
    m, ka = xa.shape
    kb = xb.shape[1]
    n = w.shape[1]
    tm = min(m, 512)
    return pl.pallas_call(
        _mm2_ln_kernel,
        grid=(m // tm,),
        in_specs=[pl.BlockSpec((tm, ka), lambda i: (i, 0)),
                  pl.BlockSpec((tm, kb), lambda i: (i, 0)),
                  pl.BlockSpec((ka + kb, n), lambda i: (0, 0)),
                  pl.BlockSpec((tm, n), lambda i: (i, 0)),
                  pl.BlockSpec((1, n), lambda i: (0, 0)),
                  pl.BlockSpec((1, n), lambda i: (0, 0))],
        out_specs=pl.BlockSpec((tm, n), lambda i: (i, 0)),
        out_shape=jax.ShapeDtypeStruct((m, n), F32),
        compiler_params=_cp("parallel"),
        name="mm2_ln",
    )(xa, xb, w, res, g.reshape(1, n), b.reshape(1, n))


def _mm_ln_kernel(x_ref, w_ref, r_ref, g_ref, b_ref, o_ref):
    y = _dot(x_ref[...].astype(BF16), w_ref[...])
    o_ref[...] = _ln(DN_ALPHA * r_ref[...] + y, g_ref[...], b_ref[...])


def _mm_ln(x, w, res, g, b):
    m, k = x.shape
    n = w.shape[1]
    tm = min(m, 512)
    return pl.pallas_call(
        _mm_ln_kernel,
        grid=(m // tm,),
        in_specs=[pl.BlockSpec((tm, k), lambda i: (i, 0)),
                  pl.BlockSpec((k, n), lambda i: (0, 0)),
                  pl.BlockSpec((tm, n), lambda i: (i, 0)),
                  pl.BlockSpec((1, n), lambda i: (0, 0)),
                  pl.BlockSpec((1, n), lambda i: (0, 0))],
        out_specs=pl.BlockSpec((tm, n), lambda i: (i, 0)),
        out_shape=jax.ShapeDtypeStruct((m, n), F32),
        compiler_params=_cp("parallel"),
        name="mm_ln",
    )(x, w, res, g.reshape(1, n), b.reshape(1, n))


def _s5_kernel(u_ref, h0_ref, bd_ref, ar_ref, ai_ref, cbd_ref, d_ref, wg_ref,
               y_ref, hT_ref, hbuf, state, *tm_bufs, rows, steps):
    c = pl.program_id(0)

    @pl.when(c == 0)
    def _():
        state[...] = h0_ref[...]

    if tm_bufs:
        ubm, utm, ytm = tm_bufs
        nlb = S5_WIDTH // LANES
        for s in range(rows):
            us = u_ref[s]
            for kb in range(nlb):
                ubm[kb, s * steps:(s + 1) * steps, :] = us[:, kb * LANES:(kb + 1) * LANES]
        for t in range(steps):
            for kb in range(nlb):
                utm[t * rows:(t + 1) * rows, kb * LANES:(kb + 1) * LANES] = ubm[kb, pl.ds(t, rows, stride=steps), :]
        u = utm[...]
    else:
        u = u_ref[...]
    ub = u.astype(BF16)
    for part in range(2):
        for jt in range(S5_FLAT // MXU_TILE):
            c0 = part * S5_FLAT + jt * MXU_TILE
            k0 = (jt * MXU_TILE // S5_STATE * S5_GROUP) // LANES * LANES
            hbuf[:, c0:c0 + MXU_TILE] = _dot(ub[:, k0:k0 + LANES], bd_ref[k0:k0 + LANES, c0:c0 + MXU_TILE])
    ar = ar_ref[...]
    ai = ai_ref[...]

    def step(t, carry):
        hr, hi = carry
        r0 = pl.multiple_of(t * rows, rows)
        br = hbuf[pl.ds(r0, rows), 0:S5_FLAT]
        bi = hbuf[pl.ds(r0, rows), S5_FLAT:2 * S5_FLAT]
        nr = ar * hr - ai * hi + br
        ni = ar * hi + ai * hr + bi
        hbuf[pl.ds(r0, rows), 0:S5_FLAT] = nr
        hbuf[pl.ds(r0, rows), S5_FLAT:2 * S5_FLAT] = ni
        return nr, ni

    hr, hi = lax.fori_loop(0, steps, step, (state[:, 0:S5_FLAT], state[:, S5_FLAT:2 * S5_FLAT]))
    state[:, 0:S5_FLAT] = hr
    state[:, S5_FLAT:2 * S5_FLAT] = hi

    ys = []
    for nt in range(S5_WIDTH // MXU_TILE):
        n0 = nt * MXU_TILE
        kw = MXU_TILE // S5_GROUP * S5_STATE
        acc = None
        for part in range(2):
            k0 = part * S5_FLAT + nt * kw
            term = _dot(hbuf[:, k0:k0 + kw].astype(BF16), cbd_ref[k0:k0 + kw, n0:n0 + MXU_TILE])
            acc = term if acc is None else acc + term
        ys.append(acc)
    y = jnp.concatenate(ys, axis=-1) + d_ref[...] * u
    g = jax.nn.gelu(y)
    out = g * jax.nn.sigmoid(_dot(g.astype(BF16), wg_ref[...]))
    if tm_bufs:
        for kb in range(nlb):
            ytm[kb] = out[:, kb * LANES:(kb + 1) * LANES]
        for s in range(rows):
            y_ref[s] = jnp.concatenate([ytm[kb, pl.ds(s, steps, stride=rows), :] for kb in range(nlb)],
                                       axis=-1).astype(y_ref.dtype)
    else:
        y_ref[...] = out.astype(y_ref.dtype)

    @pl.when(c == pl.num_programs(0) - 1)
    def _():
        hT_ref[...] = state[...]


def _s5(u_tm, h0, prm, rows, steps):
    n = u_tm.shape[0]
    blk = rows * steps
    bd, ar, ai, cbd, d, wg = prm
    const = lambda shape: pl.BlockSpec(shape, lambda c: (0, 0))
    return pl.pallas_call(
        functools.partial(_s5_kernel, rows=rows, steps=steps),
        grid=(n // blk,),
        in_specs=[pl.BlockSpec((blk, S5_WIDTH), lambda c: (c, 0)),
                  const((rows, 2 * S5_FLAT)),
                  const((S5_WIDTH, 2 * S5_FLAT)),
                  const((1, S5_FLAT)), const((1, S5_FLAT)),
                  const((2 * S5_FLAT, S5_WIDTH)),
                  const((1, S5_WIDTH)),
                  const((S5_WIDTH, S5_WIDTH))],
        out_specs=[pl.BlockSpec((blk, S5_WIDTH), lambda c: (c, 0)),
                   const((rows, 2 * S5_FLAT))],
        out_shape=[jax.ShapeDtypeStruct((n, S5_WIDTH), BF16),
                   jax.ShapeDtypeStruct((rows, 2 * S5_FLAT), F32)],
        scratch_shapes=[pltpu.VMEM((blk, 2 * S5_FLAT), F32),
                        pltpu.VMEM((rows, 2 * S5_FLAT), F32)],
        compiler_params=_cp("arbitrary"),
        name="s5",
    )(u_tm, h0, bd, ar, ai, cbd, d, wg)


def _s5_prompt(proj, h0, prm, steps):
    bsz, seq, _ = proj.shape
    blk = bsz * steps
    bd, ar, ai, cbd, d, wg = prm
    const = lambda shape: pl.BlockSpec(shape, lambda c: (0, 0))
    seqblk = pl.BlockSpec((bsz, steps, S5_WIDTH), lambda c: (0, c, 0))
    return pl.pallas_call(
        functools.partial(_s5_kernel, rows=bsz, steps=steps),
        grid=(seq // steps,),
        in_specs=[seqblk,
                  const((bsz, 2 * S5_FLAT)),
                  const((S5_WIDTH, 2 * S5_FLAT)),
                  const((1, S5_FLAT)), const((1, S5_FLAT)),
                  const((2 * S5_FLAT, S5_WIDTH)),
                  const((1, S5_WIDTH)),
                  const((S5_WIDTH, S5_WIDTH))],
        out_specs=[seqblk, const((bsz, 2 * S5_FLAT))],
        out_shape=[jax.ShapeDtypeStruct((bsz, seq, S5_WIDTH), BF16),
                   jax.ShapeDtypeStruct((bsz, 2 * S5_FLAT), F32)],
        scratch_shapes=[pltpu.VMEM((blk, 2 * S5_FLAT), F32),
                        pltpu.VMEM((bsz, 2 * S5_FLAT), F32),
                        pltpu.VMEM((S5_WIDTH // LANES, blk, LANES), F32),
                        pltpu.VMEM((blk, S5_WIDTH), F32),
                        pltpu.VMEM((S5_WIDTH // LANES, blk, LANES), F32)],
        compiler_params=_cp("arbitrary"),
        name="s5_prompt",
    )(proj, h0, bd, ar, ai, cbd, d, wg)


def _s5_params(lam_re, lam_im, log_dt, b_re, b_im, c_re, c_im, d_skip, w_glu):
    dt = jnp.exp(log_dt)[:, None]
    mag = jnp.exp(lam_re * dt)
    ab_re, ab_im = mag * jnp.cos(lam_im * dt), mag * jnp.sin(lam_im * dt)
    den = lam_re * lam_re + lam_im * lam_im
    nr, ni = ab_re - 1.0, ab_im
    coef_re = ((nr * lam_re + ni * lam_im) / den)[..., None]
    coef_im = ((ni * lam_re - nr * lam_im) / den)[..., None]
    bb_re = coef_re * b_re - coef_im * b_im
    bb_im = coef_re * b_im + coef_im * b_re
    eye = jnp.eye(S5_GROUPS, dtype=F32)
    pack_in = lambda bb: jnp.einsum('gpc,gh->gchp', bb, eye).reshape(S5_WIDTH, S5_FLAT)
    bd = jnp.concatenate([pack_in(bb_re), pack_in(bb_im)], axis=1).astype(BF16)
    pack_out = lambda cc: jnp.einsum('gcp,gh->gphc', cc, eye).reshape(S5_FLAT, S5_WIDTH)
    cbd = jnp.concatenate([pack_out(c_re), -pack_out(c_im)], axis=0).astype(BF16)
    return (bd, ab_re.reshape(1, S5_FLAT), ab_im.reshape(1, S5_FLAT), cbd,
            d_skip.reshape(1, S5_WIDTH), w_glu.astype(BF16))


def _swa_kernel(sink_ref, q_ref, kp_ref, kc_ref, vp_ref, vc_ref, o_ref):
    n = pl.program_id(1)
    q = q_ref[0].astype(BF16)
    k = jnp.concatenate([kp_ref[0], kc_ref[0]], axis=0).astype(BF16)
    v = jnp.concatenate([vp_ref[0], vc_ref[0]], axis=0).astype(BF16)
    qi = lax.broadcasted_iota(jnp.int32, (WINDOW, 2 * WINDOW), 0)
    kj = lax.broadcasted_iota(jnp.int32, (WINDOW, 2 * WINDOW), 1)
    dist = WINDOW + qi - kj
    valid = (dist >= 0) & (dist < WINDOW) & ((n > 0) | (kj >= WINDOW))
    distf = dist.astype(F32)
    outs = []
    for h in range(SWA_HEADS):
        g = h // SWA_REP
        qh = q[:, h * SWA_HEAD_DIM:(h + 1) * SWA_HEAD_DIM]
        kg = k[:, g * SWA_HEAD_DIM:(g + 1) * SWA_HEAD_DIM]
        vg = v[:, g * SWA_HEAD_DIM:(g + 1) * SWA_HEAD_DIM]
        s = _dot_nt(qh, kg) * (SWA_HEAD_DIM ** -0.5)
        s = s - (2.0 ** -(h + 1)) * distf
        s = jnp.where(valid, s, NEG_INF)
        sink = sink_ref[h]
        m = jnp.maximum(jnp.max(s, axis=-1, keepdims=True), sink)
        p = jnp.exp(s - m)
        p = p / (jnp.sum(p, axis=-1, keepdims=True) + jnp.exp(sink - m))
        outs.append(_dot(p.astype(BF16), vg))
    o_ref[0] = jnp.concatenate(outs, axis=-1).astype(o_ref.dtype)


def _swa_prompt(proj, sinks):
    bsz, seq, _ = proj.shape
    kcol = (S5_WIDTH + SWA_WIDTH) // SWA_KV_WIDTH
    prev = lambda b, n: (b, jnp.maximum(n - 1, 0), kcol)
    prev_v = lambda b, n: (b, jnp.maximum(n - 1, 0), kcol + 1)
    blk = (1, WINDOW, SWA_KV_WIDTH)
    return pl.pallas_call(
        _swa_kernel,
        grid=(bsz, seq // WINDOW),
        in_specs=[pl.BlockSpec(memory_space=pltpu.SMEM),
                  pl.BlockSpec((1, WINDOW, SWA_WIDTH), lambda b, n: (b, n, 1)),
                  pl.BlockSpec(blk, prev),
                  pl.BlockSpec(blk, lambda b, n: (b, n, kcol)),
                  pl.BlockSpec(blk, prev_v),
                  pl.BlockSpec(blk, lambda b, n: (b, n, kcol + 1))],
        out_specs=pl.BlockSpec((1, WINDOW, SWA_WIDTH), lambda b, n: (b, n, 0)),
        out_shape=jax.ShapeDtypeStruct((bsz, seq, SWA_WIDTH), BF16),
        compiler_params=_cp("parallel", "arbitrary"),
        name="swa_prompt",
    )(sinks, proj, proj, proj, proj, proj)


def _per_head_column(values):
    r = lax.broadcasted_iota(jnp.int32, (SWA_REP, 1), 0)
    col = jnp.full((SWA_REP, 1), values[-1], F32)
    for i in range(SWA_REP - 2, -1, -1):
        col = jnp.where(r == i, values[i], col)
    return col


def _swa_step_kernel(sink_ref, p_ref, kc_ref, vc_ref, o_ref, *, nseq, start):
    kj = lax.broadcasted_iota(jnp.int32, (SWA_REP, WINDOW), 1)
    dist = WINDOW - kj
    valid = (dist < WINDOW) & (start - WINDOW + kj >= 0)
    distf = dist.astype(F32)
    for i in range(nseq):
        row = p_ref[i]
        kn = row[:, S5_WIDTH + SWA_WIDTH:S5_WIDTH + SWA_WIDTH + SWA_KV_WIDTH].astype(BF16)
        vn = row[:, S5_WIDTH + SWA_WIDTH + SWA_KV_WIDTH:].astype(BF16)
        kc = kc_ref[i].astype(BF16)
        vc = vc_ref[i].astype(BF16)
        outs = []
        for g in range(SWA_KV_HEADS):
            q4 = jnp.concatenate(
                [row[:, S5_WIDTH + (g * SWA_REP + r) * SWA_HEAD_DIM:S5_WIDTH + (g * SWA_REP + r + 1) * SWA_HEAD_DIM]
                 for r in range(SWA_REP)], axis=0).astype(BF16)
            sl = slice(g * SWA_HEAD_DIM, (g + 1) * SWA_HEAD_DIM)
            slope = _per_head_column([2.0 ** -(g * SWA_REP + r + 1) for r in range(SWA_REP)])
            sink = _per_head_column([sink_ref[g * SWA_REP + r] for r in range(SWA_REP)])
            scale = SWA_HEAD_DIM ** -0.5
            sc = _dot_nt(q4, kc[:, sl]) * scale - slope * distf
            sc = jnp.where(valid, sc, NEG_INF)
            sn = jnp.sum(q4.astype(F32) * kn[:, sl].astype(F32), axis=-1, keepdims=True) * scale
            m = jnp.maximum(jnp.maximum(jnp.max(sc, axis=-1, keepdims=True), sn), sink)
            pc = jnp.exp(sc - m)
            pn = jnp.exp(sn - m)
            den = jnp.sum(pc, axis=-1, keepdims=True) + pn + jnp.exp(sink - m)
            o4 = _dot((pc / den).astype(BF16), vc[:, sl])
            o4 = o4 + (pn / den).astype(BF16).astype(F32) * vn[:, sl].astype(F32)
            outs.extend([o4[r:r + 1, :] for r in range(SWA_REP)])
        o_ref[i] = jnp.concatenate(outs, axis=-1).astype(o_ref.dtype)


def _swa_step(proj, k_cache, v_cache, sinks, start):
    n = proj.shape[0]
    nseq = min(n, SUBLANES)
    out = pl.pallas_call(
        functools.partial(_swa_step_kernel, nseq=nseq, start=start),
        grid=(n // nseq,),
        in_specs=[pl.BlockSpec(memory_space=pltpu.SMEM),
                  pl.BlockSpec((nseq, 1, proj.shape[1]), lambda i: (i, 0, 0)),
                  pl.BlockSpec((nseq, WINDOW, SWA_KV_WIDTH), lambda i: (i, 0, 0)),
                  pl.BlockSpec((nseq, WINDOW, SWA_KV_WIDTH), lambda i: (i, 0, 0))],
        out_specs=pl.BlockSpec((nseq, 1, SWA_WIDTH), lambda i: (i, 0, 0)),
        out_shape=jax.ShapeDtypeStruct((n, 1, SWA_WIDTH), BF16),
        compiler_params=_cp("parallel"),
        name="swa_step",
    )(sinks, proj.reshape(n, 1, -1), k_cache, v_cache)
    return out.reshape(n, SWA_WIDTH)


def _head_expand(width):
    r = lax.broadcasted_iota(jnp.int32, (LANES, SSD_HEADS * width), 0)
    c = lax.broadcasted_iota(jnp.int32, (LANES, SSD_HEADS * width), 1)
    return jnp.where((c >= r * width) & (c < (r + 1) * width), 1.0, 0.0).astype(BF16)


def _ssd_kernel(x_ref, wz_ref, wx_ref, wdt_ref, cw_ref, cb_ref, dtb_ref, a_ref, d_ref, nw_ref,
                wo_ref, g_ref, b_ref, o_ref, st_ref, conv_ref, tail, state, ybuf):
    c = pl.program_id(1)
    q = SSD_CHUNK

    @pl.when(c == 0)
    def _():
        tail[...] = jnp.zeros_like(tail)
        state[...] = jnp.zeros_like(state)

    xin = x_ref[0]
    xb = xin.astype(BF16)
    cur = _dot(xb, wx_ref[...])
    tl = tail[...]
    row8 = lax.broadcasted_iota(jnp.int32, (SUBLANES, SSD_CONV_DIM), 0)
    acc = None
    for tap in range(SSD_CONV):
        sh = SSD_CONV - 1 - tap
        if sh == 0:
            term = cur
        else:
            rolled = pltpu.roll(cur, sh, 0)
            head = jnp.where(row8 < sh, pltpu.roll(tl, sh, 0), rolled[0:SUBLANES])
            term = jnp.concatenate([head, rolled[SUBLANES:]], axis=0)
        term = term * cw_ref[tap:tap + 1, :]
        acc = term if acc is None else acc + term
    tail[...] = cur[q - SUBLANES:q]
    xbc = _silu(acc + cb_ref[...])
    xs = xbc[:, 0:SSD_INNER]
    bm = xbc[:, SSD_INNER:SSD_INNER + SSD_GROUPS * SSD_STATE].astype(BF16)
    cm = xbc[:, SSD_INNER + SSD_GROUPS * SSD_STATE:].astype(BF16)

    dt = _softplus(_dot(xb, wdt_ref[...]) + dtb_ref[...])
    da = dt * a_ref[...]
    ri = lax.broadcasted_iota(jnp.int32, (q, q), 0)
    ci = lax.broadcasted_iota(jnp.int32, (q, q), 1)
    causal = ri >= ci
    tri = jnp.where(causal, 1.0, 0.0).astype(BF16)
    da1, da2, da3 = _split3(da)
    acs = _dot(tri, da1) + _dot(tri, da2) + _dot(tri, da3)
    acs_t = acs.T
    e64 = _head_expand(SSD_HEAD_DIM)
    dt_x = _dot_exact01(dt, e64)
    acs_x = _dot_exact01(acs, e64)
    last_x = acs_x[q - 1:q, :]
    dx = dt_x * xs
    dsx = (jnp.exp(last_x - acs_x) * dx).astype(BF16)
    eacs_x = jnp.exp(acs_x)
    dxb = dx.astype(BF16)

    for g in range(SSD_GROUPS):
        ns = slice(g * SSD_STATE, (g + 1) * SSD_STATE)
        gs = slice(g * SSD_GW, (g + 1) * SSD_GW)
        cb = _dot_nt(cm[:, ns], bm[:, ns])
        for r in range(SSD_HPG):
            h = g * SSD_HPG + r
            seg = acs[:, h:h + 1] - acs_t[h:h + 1, :]
            decay = jnp.where(causal, jnp.exp(jnp.where(causal, seg, 0.0)), 0.0)
            hs = slice(h * SSD_HEAD_DIM, (h + 1) * SSD_HEAD_DIM)
            ybuf[:, hs] = _dot((cb * decay).astype(BF16), dxb[:, hs])
        s_in = state[g]
        y_off = _dot(cm[:, ns], s_in.astype(BF16)) * eacs_x[:, gs]
        ybuf[:, gs] = ybuf[:, gs] + y_off
        state[g] = jnp.exp(last_x[:, gs]) * s_in + _dot_tn(bm[:, ns], dsx[:, gs])

    y = ybuf[...] + d_ref[...] * xs
    y = y * _silu(_dot(xb, wz_ref[...]))
    y = y * lax.rsqrt(jnp.mean(y * y, axis=-1, keepdims=True) + RMS_EPS) * nw_ref[...]
    mix = _dot(y.astype(BF16), wo_ref[...])
    o_ref[0] = _ln(DN_ALPHA * xin + mix, g_ref[...], b_ref[...])

    @pl.when(c == pl.num_programs(1) - 1)
    def _():
        conv_ref[0] = cur[q - SUBLANES:q]
        for g in range(SSD_GROUPS):
            st_ref[0, g * SSD_GW:(g + 1) * SSD_GW, :] = state[g].T


def _ssd_prompt(x, w3, prm, w_out, g, b):
    bsz, seq, _ = x.shape
    w_z, w_x, w_dt = w3
    cw, cb, dtb, a, d_x, nw = prm
    const = lambda shape: pl.BlockSpec(shape, lambda b, c: (0, 0))
    seqblk = lambda w: pl.BlockSpec((1, SSD_CHUNK, w), lambda b, c: (b, c, 0))
    return pl.pallas_call(
        _ssd_kernel,
        grid=(bsz, seq // SSD_CHUNK),
        in_specs=[seqblk(D_MODEL),
                  const((D_MODEL, SSD_INNER)), const((D_MODEL, SSD_CONV_DIM)), const((D_MODEL, LANES)),
                  const((SSD_CONV, SSD_CONV_DIM)), const((1, SSD_CONV_DIM)),
                  const((1, LANES)), const((1, LANES)),
                  const((1, SSD_INNER)), const((1, SSD_INNER)),
                  const((SSD_INNER, D_MODEL)), const((1, D_MODEL)), const((1, D_MODEL))],
        out_specs=[seqblk(D_MODEL),
                   pl.BlockSpec((1, SSD_INNER, SSD_STATE), lambda b, c: (b, 0, 0)),
                   pl.BlockSpec((1, SUBLANES, SSD_CONV_DIM), lambda b, c: (b, 0, 0))],
        out_shape=[jax.ShapeDtypeStruct((bsz, seq, D_MODEL), F32),
                   jax.ShapeDtypeStruct((bsz, SSD_INNER, SSD_STATE), F32),
                   jax.ShapeDtypeStruct((bsz, SUBLANES, SSD_CONV_DIM), F32)],
        scratch_shapes=[pltpu.VMEM((SUBLANES, SSD_CONV_DIM), F32),
                        pltpu.VMEM((SSD_GROUPS, SSD_STATE, SSD_GW), F32),
                        pltpu.VMEM((SSD_CHUNK, SSD_INNER), F32)],
        compiler_params=_cp("parallel", "arbitrary"),
        name="ssd_prompt",
    )(x, w_z, w_x, w_dt, cw, cb, dtb, a, d_x, nw, w_out, g.reshape(1, -1), b.reshape(1, -1))


def _pad_rows(row):
    return jnp.concatenate([row, jnp.zeros((SUBLANES - 1, row.shape[1]), row.dtype)], axis=0)


def _ssd_step_kernel(z_ref, x_ref, dt_ref, buf_ref, s0_ref, cw_ref, cb_ref, dtb_ref, a_ref, d_ref, nw_ref,
                     y_ref, s1_ref, *, nseq):
    e64 = _head_expand(SSD_HEAD_DIM)
    ones3 = jnp.where(lax.broadcasted_iota(jnp.int32, (SUBLANES, SSD_STATE), 0) < 3, 1.0, 0.0).astype(BF16)
    for i in range(nseq):
        buf = buf_ref[i]
        acc = buf[0:1] * cw_ref[0:1, :]
        acc = acc + buf[1:2] * cw_ref[1:2, :]
        acc = acc + buf[2:3] * cw_ref[2:3, :]
        acc = acc + x_ref[i] * cw_ref[3:4, :]
        xbc = _silu(acc + cb_ref[...])
        xs = xbc[:, 0:SSD_INNER]
        bm = xbc[:, SSD_INNER:SSD_INNER + SSD_GROUPS * SSD_STATE].astype(BF16)
        cm = xbc[:, SSD_INNER + SSD_GROUPS * SSD_STATE:].astype(BF16)
        dt = _softplus(dt_ref[i] + dtb_ref[...])
        ed = jnp.exp(dt * a_ref[...])
        dt_x = _dot_exact01(_pad_rows(dt), e64)[0:1]
        ed_x = _dot_exact01(_pad_rows(ed), e64)[0:1]
        dx = dt_x * xs
        dxb = dx.astype(BF16)
        ys = []
        for g in range(SSD_GROUPS):
            ns = slice(g * SSD_STATE, (g + 1) * SSD_STATE)
            gs = slice(g * SSD_GW, (g + 1) * SSD_GW)
            s0 = s0_ref[i, gs, :]
            cmg = _pad_rows(cm[:, ns])
            cbv = jnp.sum(cm[:, ns].astype(F32) * bm[:, ns].astype(F32), axis=-1, keepdims=True)
            y_diag = cbv.astype(BF16).astype(F32) * dxb[:, gs].astype(F32)
            y_off = _dot_nt(cmg, s0.astype(BF16))[0:1] * ed_x[:, gs]
            ys.append(y_diag + y_off)
            e1, e2, e3 = _split3(ed_x[:, gs])
            ed_rows = jnp.concatenate([e1, e2, e3, jnp.zeros((SUBLANES - 3, SSD_GW), BF16)], axis=0)
            ed_full = _dot_tn(ed_rows, ones3)
            outer = _dot_tn(_pad_rows(dxb[:, gs]), _pad_rows(bm[:, ns]))
            s1_ref[i, gs, :] = ed_full * s0 + outer
        y = jnp.concatenate(ys, axis=-1) + d_ref[...] * xs
        y = y * _silu(z_ref[i])
        y = y * lax.rsqrt(jnp.mean(y * y, axis=-1, keepdims=True) + RMS_EPS) * nw_ref[...]
        y_ref[i] = y.astype(y_ref.dtype)


def _ssd_step(z, xbc, dtraw, conv_buf, s0_all, layer, prm):
    n = z.shape[0]
    nseq = min(n, 4)
    cw, cb, dtb, a, d_x, nw = prm
    const = lambda shape: pl.BlockSpec(shape, lambda i: (0, 0))
    tok = lambda w: pl.BlockSpec((nseq, 1, w), lambda i: (i, 0, 0))
    y, s1 = pl.pallas_call(
        functools.partial(_ssd_step_kernel, nseq=nseq),
        grid=(n // nseq,),
        in_specs=[tok(SSD_INNER), tok(SSD_CONV_DIM), tok(LANES),
                  pl.BlockSpec((nseq, SSD_CONV - 1, SSD_CONV_DIM), lambda i: (i, 0, 0)),
                  pl.BlockSpec((None, nseq, SSD_INNER, SSD_STATE), lambda i: (layer, i, 0, 0)),
                  const((SSD_CONV, SSD_CONV_DIM)), const((1, SSD_CONV_DIM)),
                  const((1, LANES)), const((1, LANES)),
                  const((1, SSD_INNER)), const((1, SSD_INNER))],
        out_specs=[tok(SSD_INNER),
                   pl.BlockSpec((nseq, SSD_INNER, SSD_STATE), lambda i: (i, 0, 0))],
        out_shape=[jax.ShapeDtypeStruct((n, 1, SSD_INNER), BF16),
                   jax.ShapeDtypeStruct((n, SSD_INNER, SSD_STATE), F32)],
        compiler_params=_cp("parallel"),
        name="ssd_step",
    )(z.reshape(n, 1, -1), xbc.reshape(n, 1, -1), dtraw.reshape(n, 1, -1), conv_buf, s0_all,
      cw, cb, dtb, a, d_x, nw)
    return y.reshape(n, SSD_INNER), s1


def _ssd_params(conv_w, conv_b, dt_bias, a_log, d_skip, norm_w):
    pad = lambda v: jnp.pad(v, (0, LANES - SSD_HEADS)).reshape(1, LANES)
    return (conv_w, conv_b.reshape(1, -1), pad(dt_bias), pad(-jnp.exp(a_log)),
            jnp.repeat(d_skip, SSD_HEAD_DIM).reshape(1, SSD_INNER), norm_w.reshape(1, SSD_INNER))


PEER_TE = 2048
PEER_ROWS = PEER_TE // PEER_KEYS


PEER_CHUNK_ROWS = 4
PEER_TOK_CHUNK = 256
PACK = 2 * SUBLANES


def _batcher_sort_pairs(n):
    def merge(lo, hi, r):
        step = r * 2
        if step < hi - lo:
            yield from merge(lo, hi, step)
            yield from merge(lo + r, hi, step)
            for i in range(lo + r, hi - r, step):
                yield (i, i + r)
        else:
            yield (lo, lo + r)

    def sort(lo, hi):
        if hi - lo >= 1:
            mid = lo + (hi - lo) // 2
            yield from sort(lo, mid)
            yield from sort(mid + 1, hi)
            yield from merge(lo, hi, 1)

    return tuple(sort(0, n - 1))


_SORT16 = _batcher_sort_pairs(PEER_TOPK)
_BITONIC16 = tuple((i, i + d) for d in (8, 4, 2, 1) for i in range(PEER_TOPK) if not i & d)


def _mx(a, b):
    return b if a is None else a if b is None else jnp.maximum(a, b)


def _mn(a, b):
    return None if a is None or b is None else jnp.minimum(a, b)


def _top16(rows):
    v = list(rows) + [None] * (PEER_TOPK - len(rows))
    for i, j in _SORT16:
        v[i], v[j] = _mx(v[i], v[j]), _mn(v[i], v[j])
    for shift in (4, 2, 1):
        w = [None if x is None else pltpu.roll(x, shift, 0) for x in v]
        v = [_mx(v[k], w[PEER_TOPK - 1 - k]) for k in range(PEER_TOPK)]
        for i, j in _BITONIC16:
            v[i], v[j] = _mx(v[i], v[j]), _mn(v[i], v[j])
    return v


def _prefix_count(test, thr):
    m8 = test(thr[7])
    m4 = test(jnp.where(m8, thr[11], thr[3]))
    m2 = test(jnp.where(m8, jnp.where(m4, thr[13], thr[9]), jnp.where(m4, thr[5], thr[1])))
    hi = jnp.where(m4, jnp.where(m2, thr[14], thr[12]), jnp.where(m2, thr[10], thr[8]))
    lo = jnp.where(m4, jnp.where(m2, thr[6], thr[4]), jnp.where(m2, thr[2], thr[0]))
    m1 = test(jnp.where(m8, hi, lo))
    return (jnp.where(m8, 8.0, 0.0) + jnp.where(m4, 4.0, 0.0)) + (jnp.where(m2, 2.0, 0.0) + jnp.where(m1, 1.0, 0.0))


def _peer_route(xb, wq_ref, k1_ref, k2_ref, c_ref, e1_ref, rk_ref, e2_ref):
    tm = xb.shape[0]
    q = _dot(xb, wq_ref[...]).astype(BF16)
    half = PEER_DK // 2
    nblk = PEER_KEYS // SUBLANES
    sub = lax.broadcasted_iota(jnp.int32, (SUBLANES, tm), 0)

    def by_sublane(vals):
        out = vals[SUBLANES - 1]
        for s in range(SUBLANES - 2, -1, -1):
            out = jnp.where(sub == s, vals[s], out)
        return out

    for h in range(PEER_HEADS):
        s1 = _dot_nt(k1_ref[h], q[:, h * PEER_DK:h * PEER_DK + half])
        s2 = _dot_nt(k2_ref[h], q[:, h * PEER_DK + half:(h + 1) * PEER_DK])
        r1 = [s1[k * SUBLANES:(k + 1) * SUBLANES] for k in range(nblk)]
        r2 = [s2[k * SUBLANES:(k + 1) * SUBLANES] for k in range(nblk)]
        a = _top16(r1)
        b = _top16(r2)
        r1 = [x - a[0] for x in r1]
        r2 = [x - b[0] for x in r2]
        a = [x - a[0] for x in a]
        b = [x - b[0] for x in b]
        b_lo, b_hi, a_hi = by_sublane(b[0:8]), by_sublane(b[8:16]), by_sublane(a[8:16])
        cand = [a[0] + b_lo, a[0] + b_hi] + [a[i] + b_lo for i in range(1, 8)] + [a_hi + b[0]]
        top = _top16(cand)
        tau = top[PEER_TOPK - 1]
        zsum = jnp.exp(top[0])
        for k in range(1, PEER_TOPK):
            zsum = zsum + jnp.exp(top[k])
        inv_z = 1.0 / zsum
        for k in range(nblk):
            x1 = r1[k]
            cnt = _prefix_count(lambda t: (x1 + t) >= tau, b[0:15])
            cnt = cnt + jnp.where((x1 + b[15]) >= tau, 1.0, 0.0)
            c_ref[h, k, :, 0:tm] = cnt
            e1_ref[h, k, :, 0:tm] = jnp.exp(x1) * inv_z
        ranks, e2s = [], []
        for k in range(nblk):
            x2 = r2[k]
            rnk = _prefix_count(lambda t: t > x2, b[0:15])
            ranks.append(jnp.where(x2 >= b[15], rnk, float(PEER_TOPK)))
            e2s.append(jnp.exp(x2))
        rk_ref[h, :, 0:tm] = jnp.concatenate(ranks, axis=0).astype(BF16)
        e2_ref[h, :, 0:tm] = jnp.concatenate(e2s, axis=0).astype(BF16)


def _gelu_tanh(x):
    c0 = math.sqrt(2.0 / math.pi)
    return (0.5 * x) * (1.0 + jnp.tanh(x * (c0 + (c0 * 0.044715) * (x * x))))


def _peer_kernel(x_ref, xp_ref, wq_ref, k1_ref, k2_ref, u_ref, vt_ref, g_ref, b_ref, o_ref,
                 acc, c_ref, e1_ref, rk_ref, e2_ref, act_ref):
    i = pl.program_id(0)
    j = pl.program_id(1)
    nblk = pl.num_programs(0) - 1
    tm = x_ref.shape[0]
    xb = x_ref[...].astype(BF16)

    @pl.when((i == 0) & (j == 0))
    def _():
        acc[...] = jnp.zeros_like(acc)
        act_ref[1] = jnp.zeros((PEER_TE, tm), BF16)

    @pl.when((j == 0) & (i < nblk))
    def _():
        _peer_route(xb, wq_ref, k1_ref, k2_ref, c_ref, e1_ref, rk_ref, e2_ref)

    nb2 = PEER_KEYS // PACK
    slot = j % 2
    crows = PEER_CHUNK_ROWS
    nch = PEER_ROWS // crows
    tc = min(tm, PEER_TOK_CHUNK)
    apply_after = {nch // 2 - 1 + k: k * tc for k in range(tm // tc)}

    @pl.when((i < nblk) | (j == 0))
    def _():
        for ch in range(nch):
            e0 = ch * crows * PEER_KEYS
            h_t = _dot_nt(u_ref[e0:e0 + crows * PEER_KEYS, :], xb)
            if ch in apply_after:
                t0 = apply_after[ch]
                acc[:, t0:t0 + tc] += _dot(vt_ref[...], act_ref[1 - slot, :, t0:t0 + tc])
            rows = [ch * crows + ri for ri in range(crows)]
            for tl in range(tm // LANES):
                ts = slice(tl * LANES, (tl + 1) * LANES)
                gate = [[None] * nb2 for _ in range(crows)]
                for h in range(PEER_HEADS):
                    blk = [j * (PEER_ROWS // SUBLANES) + r // SUBLANES for r in rows]
                    sub = [r % SUBLANES for r in rows]
                    cbv = [jnp.broadcast_to(c_ref[h, k, s:s + 1, ts], (PACK, LANES)).astype(BF16)
                           for k, s in zip(blk, sub)]
                    ebv = [jnp.broadcast_to(e1_ref[h, k, s:s + 1, ts], (PACK, LANES)).astype(BF16)
                           for k, s in zip(blk, sub)]
                    for b2 in range(nb2):
                        ks = slice(b2 * PACK, (b2 + 1) * PACK)
                        rk = rk_ref[h, ks, ts]
                        e2 = e2_ref[h, ks, ts]
                        for ri in range(crows):
                            w = jnp.where(rk < cbv[ri], ebv[ri] * e2, 0.0)
                            gate[ri][b2] = w if gate[ri][b2] is None else gate[ri][b2] + w
                for ri in range(crows):
                    for b2 in range(nb2):
                        lo = ri * PEER_KEYS + b2 * PACK
                        hv = _gelu_tanh(h_t[lo:lo + PACK, ts].astype(BF16))
                        act_ref[slot, e0 + lo:e0 + lo + PACK, ts] = hv * gate[ri][b2]

    @pl.when(j == 0)
    def _():
        o_ref[...] = _ln(DN_ALPHA * xp_ref[...] + acc[...].T, g_ref[...], b_ref[...])
        acc[...] = jnp.zeros_like(acc)


def _peer_ln(x, wq, k1, k2, u_all, vt_all, layer, g, b):
    t = x.shape[0]
    tm = min(t, 512)
    nblk = t // tm
    ntile = PEER_EXPERTS // PEER_TE
    const2 = lambda shape: pl.BlockSpec(shape, lambda i, j: (0, 0))
    const3 = lambda shape: pl.BlockSpec(shape, lambda i, j: (0, 0, 0))
    cur = lambda i, j: (jnp.minimum(i, nblk - 1), 0)
    prev = lambda i, j: (jnp.maximum(i - 1, 0), 0)
    return pl.pallas_call(
        _peer_kernel,
        grid=(nblk + 1, ntile),
        in_specs=[pl.BlockSpec((tm, D_MODEL), cur),
                  pl.BlockSpec((tm, D_MODEL), prev),
                  const2((D_MODEL, PEER_HEADS * PEER_DK)),
                  const3((PEER_HEADS, PEER_KEYS, PEER_DK // 2)),
                  const3((PEER_HEADS, PEER_KEYS, PEER_DK // 2)),
                  pl.BlockSpec((None, PEER_TE, D_MODEL), lambda i, j: (layer, jnp.where(i < nblk, j, 0), 0)),
                  pl.BlockSpec((None, D_MODEL, PEER_TE),
                               lambda i, j: (layer, 0, jnp.where(i < nblk, (j + ntile - 1) % ntile, ntile - 1))),
                  const2((1, D_MODEL)), const2((1, D_MODEL))],
        out_specs=pl.BlockSpec((tm, D_MODEL), lambda i, j: (jnp.where(j == 0, prev(i, j)[0], cur(i, j)[0]), 0)),
        out_shape=jax.ShapeDtypeStruct((t, D_MODEL), F32),
        scratch_shapes=[pltpu.VMEM((D_MODEL, tm), F32),
                        pltpu.VMEM((PEER_HEADS, PEER_KEYS // SUBLANES, SUBLANES, tm), F32),
                        pltpu.VMEM((PEER_HEADS, PEER_KEYS // SUBLANES, SUBLANES, tm), F32),
                        pltpu.VMEM((PEER_HEADS, PEER_KEYS, tm), BF16),
                        pltpu.VMEM((PEER_HEADS, PEER_KEYS, tm), BF16),
                        pltpu.VMEM((2, PEER_TE, tm), BF16)],
        compiler_params=_cp("arbitrary", "arbitrary"),
        name="peer",
    )(x, x, wq, k1, k2, u_all, vt_all, g.reshape(1, -1), b.reshape(1, -1))


def _peer_tables_kernel(u_ref, v_ref, ub_ref, vt_ref):
    ub_ref[...] = u_ref[...].astype(BF16)
    vt_ref[...] = v_ref[...].T.astype(BF16)


def _peer_tables(peer_u, peer_v):
    nl, ne, nd = peer_u.shape
    tile = pl.BlockSpec((None, D_MODEL, nd), lambda l, e: (l, e, 0))
    return pl.pallas_call(
        _peer_tables_kernel,
        grid=(nl, ne // D_MODEL),
        in_specs=[tile, tile],
        out_specs=[tile, pl.BlockSpec((None, nd, D_MODEL), lambda l, e: (l, 0, e))],
        out_shape=[jax.ShapeDtypeStruct((nl, ne, nd), BF16),
                   jax.ShapeDtypeStruct((nl, nd, ne), BF16)],
        compiler_params=_cp("parallel", "parallel"),
        name="peer_tables",
    )(peer_u, peer_v)


def _even_prompt(x, bsz, seq, w_in, s5p, sinks, w_out, g, b):
    proj = _mm(x, w_in, tn=640)
    p3 = proj.reshape(bsz, seq, -1)
    y_a, h_t = _s5_prompt(p3, jnp.zeros((bsz, 2 * S5_FLAT), F32), s5p, min(seq, 64))
    y_a = y_a.reshape(bsz * seq, S5_WIDTH)
    y_b = _swa_prompt(p3, sinks).reshape(bsz * seq, SWA_WIDTH)
    x = _mm2_ln(y_a, y_b, w_out, x, g, b)
    kv = p3[:, seq - WINDOW:, S5_WIDTH + SWA_WIDTH:]
    new_k = kv[..., :SWA_KV_WIDTH].reshape(bsz, WINDOW, SWA_KV_HEADS, SWA_HEAD_DIM)
    new_v = kv[..., SWA_KV_WIDTH:].reshape(bsz, WINDOW, SWA_KV_HEADS, SWA_HEAD_DIM)
    h_re = h_t[:, :S5_FLAT].reshape(bsz, S5_GROUPS, S5_STATE)
    h_im = h_t[:, S5_FLAT:].reshape(bsz, S5_GROUPS, S5_STATE)
    return x, h_re, h_im, new_k, new_v


def _even_sample(x, h_re, h_im, k_buf, v_buf, start, w_in, s5p, sinks, w_out, g, b):
    n = x.shape[0]
    proj = _mm(x, w_in, tn=640)
    h0 = jnp.concatenate([h_re.reshape(n, S5_FLAT), h_im.reshape(n, S5_FLAT)], axis=1)
    y_a, h_t = _s5(proj[:, :S5_WIDTH], h0, s5p, n, 1)
    kc = k_buf.reshape(n, WINDOW, SWA_KV_WIDTH)
    vc = v_buf.reshape(n, WINDOW, SWA_KV_WIDTH)
    y_b = _swa_step(proj, kc, vc, sinks, start)
    x = _mm2_ln(y_a, y_b, w_out, x, g, b)
    kn = proj[:, None, S5_WIDTH + SWA_WIDTH:S5_WIDTH + SWA_WIDTH + SWA_KV_WIDTH]
    vn = proj[:, None, S5_WIDTH + SWA_WIDTH + SWA_KV_WIDTH:]
    new_k = jnp.concatenate([kc[:, 1:], kn], axis=1).reshape(k_buf.shape)
    new_v = jnp.concatenate([vc[:, 1:], vn], axis=1).reshape(v_buf.shape)
    return (x, h_t[:, :S5_FLAT].reshape(h_re.shape), h_t[:, S5_FLAT:].reshape(h_im.shape), new_k, new_v)


def _odd_weights(w_in):
    w_z = w_in[:, :SSD_INNER].astype(BF16)
    w_x = w_in[:, SSD_INNER:SSD_INNER + SSD_CONV_DIM].astype(BF16)
    w_dt = jnp.pad(w_in[:, SSD_INNER + SSD_CONV_DIM:], ((0, 0), (0, LANES - SSD_HEADS))).astype(BF16)
    return w_z, w_x, w_dt


def _odd_prompt(x, bsz, seq, w3, ssdp, w_out, g, b):
    x, st, tail = _ssd_prompt(x.reshape(bsz, seq, -1), w3, ssdp, w_out, g, b)
    return (x.reshape(bsz * seq, -1), st.reshape(bsz, SSD_HEADS, SSD_HEAD_DIM, SSD_STATE),
            tail[:, SUBLANES - (SSD_CONV - 1):])


def _odd_sample(x, ssm_all, layer, conv, w3, ssdp, w_out, g, b):
    n = x.shape[0]
    w_z, w_x, w_dt = w3
    z = _mm(x, w_z)
    xbc = _mm(x, w_x)
    dtraw = _mm(x, w_dt)
    s0_all = ssm_all.reshape(ssm_all.shape[0], n, SSD_INNER, SSD_STATE)
    y, st = _ssd_step(z, xbc, dtraw, conv, s0_all, layer, ssdp)
    x = _mm_ln(y, w_out, x, g, b)
    new_conv = jnp.concatenate([conv[:, 1:], xbc[:, None, :]], axis=1)
    return x, st.reshape(ssm_all.shape[1:]), new_conv


def kernel(x_prompt, x_sample, state_s5_re, state_s5_im, cache_swa_k, cache_swa_v, state_ssd, state_conv,
           w_in_even, s5_lambda_re, s5_lambda_im, s5_log_dt, s5_b_re, s5_b_im, s5_c_re, s5_c_im, s5_d,
           s5_w_glu, swa_sinks, w_out_even,
           w_in_odd, ssd_conv_w, ssd_conv_b, ssd_dt_bias, ssd_a_log, ssd_d, ssd_norm_w, w_out_odd,
           ln1_g, ln1_b, ln2_g, ln2_b, peer_w_q, peer_k1, peer_k2, peer_u, peer_v):
    bsz, seq, _ = x_prompt.shape
    nsm = x_sample.shape[0]
    xp = x_prompt.reshape(bsz * seq, D_MODEL)
    xs = x_sample.reshape(nsm, D_MODEL)
    u_all, vt_all = _peer_tables(peer_u, peer_v)
    p_out = [[] for _ in range(6)]
    s_out = [[] for _ in range(6)]
    for layer in range(DEPTH):
        i = layer // 2
        if layer % 2 == 0:
            w_in = w_in_even[i].astype(BF16)
            w_out = w_out_even[i].astype(BF16)
            s5p = _s5_params(s5_lambda_re[i], s5_lambda_im[i], s5_log_dt[i], s5_b_re[i], s5_b_im[i],
                             s5_c_re[i], s5_c_im[i], s5_d[i], s5_w_glu[i])
            xp, hr, hi, nk, nv = _even_prompt(xp, bsz, seq, w_in, s5p, swa_sinks[i], w_out,
                                              ln1_g[layer], ln1_b[layer])
            for lst, val in zip(p_out[:4], (hr, hi, nk, nv)):
                lst.append(val)
            xs, hr, hi, nk, nv = _even_sample(xs, state_s5_re[i], state_s5_im[i], cache_swa_k[i], cache_swa_v[i],
                                              PAST_LEN, w_in, s5p, swa_sinks[i], w_out,
                                              ln1_g[layer], ln1_b[layer])
            for lst, val in zip(s_out[:4], (hr, hi, nk, nv)):
                lst.append(val)
        else:
            w3 = _odd_weights(w_in_odd[i])
            w_out = w_out_odd[i].astype(BF16)
            ssdp = _ssd_params(ssd_conv_w[i], ssd_conv_b[i], ssd_dt_bias[i], ssd_a_log[i], ssd_d[i],
                               ssd_norm_w[i])
            xp, st, cb = _odd_prompt(xp, bsz, seq, w3, ssdp, w_out, ln1_g[layer], ln1_b[layer])
            p_out[4].append(st)
            p_out[5].append(cb)
            xs, st, cb = _odd_sample(xs, state_ssd, i, state_conv[i], w3, ssdp, w_out,
                                     ln1_g[layer], ln1_b[layer])
            s_out[4].append(st)
            s_out[5].append(cb)
        wq = peer_w_q[layer].astype(BF16)
        k1 = peer_k1[layer].astype(BF16)
        k2 = peer_k2[layer].astype(BF16)
        xp = _peer_ln(xp, wq, k1, k2, u_all, vt_all, layer, ln2_g[layer], ln2_b[layer])
        xs = _peer_ln(xs, wq, k1, k2, u_all, vt_all, layer, ln2_g[layer], ln2_b[layer])
    stack = lambda lists: tuple(jnp.stack(l) for l in lists)
    return ((xp.reshape(bsz, seq, D_MODEL), xs.reshape(nsm, 1, D_MODEL)) + stack(p_out) + stack(s_out))
```

```python
import functools
import math

import jax
import jax.numpy as jnp
from jax import lax
from jax.experimental import pallas as pl
from jax.experimental.pallas import tpu as pltpu

F32 = jnp.float32
BF16 = jnp.bfloat16

D_MODEL = 1024
DEPTH = 4
PAST_LEN = 8192
S5_WIDTH = 512
S5_GROUP = 16
S5_GROUPS = 32
S5_STATE = 64
S5_FLAT = S5_GROUPS * S5_STATE
SWA_HEADS = 8
SWA_KV_HEADS = 2
SWA_REP = SWA_HEADS // SWA_KV_HEADS
SWA_HEAD_DIM = 64
SWA_WIDTH = SWA_HEADS * SWA_HEAD_DIM
SWA_KV_WIDTH = SWA_KV_HEADS * SWA_HEAD_DIM
D_IN_EVEN = S5_WIDTH + SWA_WIDTH + 2 * SWA_KV_WIDTH
WINDOW = 128
SSD_INNER = 2048
SSD_HEAD_DIM = 64
SSD_HEADS = 32
SSD_GROUPS = 4
SSD_HPG = SSD_HEADS // SSD_GROUPS
SSD_GW = SSD_HPG * SSD_HEAD_DIM
SSD_STATE = 128
SSD_CONV = 4
SSD_CHUNK = 128
SSD_CONV_DIM = SSD_INNER + 2 * SSD_GROUPS * SSD_STATE
PEER_HEADS = 8
PEER_KEYS = 128
PEER_EXPERTS = PEER_KEYS * PEER_KEYS
PEER_DK = 256
PEER_TOPK = 16
DN_ALPHA = (2 * DEPTH) ** 0.25
LN_EPS = 1e-5
RMS_EPS = 1e-5
NEG_INF = -1e30

LANES = 128
SUBLANES = 8
MXU_TILE = 256
VMEM_LIMIT = 56 * 1024 * 1024


def _cp(*sem):
    return pltpu.CompilerParams(dimension_semantics=sem, vmem_limit_bytes=VMEM_LIMIT)


def _ln(x, g, b):
    mu = jnp.mean(x, axis=-1, keepdims=True)
    xc = x - mu
    var = jnp.mean(xc * xc, axis=-1, keepdims=True)
    return xc * lax.rsqrt(var + LN_EPS) * g + b


def _softplus(x):
    return jnp.maximum(x, 0.0) + jnp.log(1.0 + jnp.exp(-jnp.abs(x)))


def _silu(x):
    return x * jax.nn.sigmoid(x)


def _dot(a, b):
    return jnp.dot(a, b, preferred_element_type=F32)


def _dot_nt(a, b):
    return lax.dot_general(a, b, (((1,), (1,)), ((), ())), preferred_element_type=F32)


def _dot_tn(a, b):
    return lax.dot_general(a, b, (((0,), (0,)), ((), ())), preferred_element_type=F32)


def _split3(x):
    x1 = x.astype(BF16)
    r1 = x - x1.astype(F32)
    x2 = r1.astype(BF16)
    r2 = r1 - x2.astype(F32)
    return x1, x2, r2.astype(BF16)


def _dot_exact01(x, m01):
    x1, x2, x3 = _split3(x)
    return _dot(x1, m01) + _dot(x2, m01) + _dot(x3, m01)


def _mm_kernel(x_ref, w_ref, o_ref):
    o_ref[...] = _dot(x_ref[...].astype(BF16), w_ref[...]).astype(o_ref.dtype)


def _mm(x, w, tm=512, tn=512, out_dtype=F32):
    m, k = x.shape
    n = w.shape[1]
    tm = min(m, tm)
    tn = min(n, tn)
    return pl.pallas_call(
        _mm_kernel,
        grid=(m // tm, n // tn),
        in_specs=[pl.BlockSpec((tm, k), lambda i, j: (i, 0)),
                  pl.BlockSpec((k, tn), lambda i, j: (0, j))],
        out_specs=pl.BlockSpec((tm, tn), lambda i, j: (i, j)),
        out_shape=jax.ShapeDtypeStruct((m, n), out_dtype),
        compiler_params=_cp("parallel", "arbitrary"),
        name="mm",
    )(x, w)


def _mm2_ln_kernel(xa_ref, xb_ref, w_ref, r_ref, g_ref, b_ref, o_ref):
    ka = xa_ref.shape[1]
    y = _dot(xa_ref[...], w_ref[0:ka, :]) + _dot(xb_ref[...], w_ref[ka:, :])
    o_ref[...] = _ln(DN_ALPHA * r_ref[...] + y, g_ref[...], b_ref[...])


def _mm2_ln(xa, xb, w, res, g, b):
    m, ka = xa.shape
    kb = xb.shape[1]
    n = w.shape[1]
    tm = min(m, 512)
    return pl.pallas_call(
        _mm2_ln_kernel,
        grid=(m // tm,),
        in_specs=[pl.BlockSpec((tm, ka), lambda i: (i, 0)),
                  pl.BlockSpec((tm, kb), lambda i: (i, 0)),
                  pl.BlockSpec((ka + kb, n), lambda i: (0, 0)),
                  pl.BlockSpec((tm, n), lambda i: (i, 0)),
                  pl.BlockSpec((1, n), lambda i: (0, 0)),
                  pl.BlockSpec((1, n), lambda i: (0, 0))],
        out_specs=pl.BlockSpec((tm, n), lambda i: (i, 0)),
        out_shape=jax.ShapeDtypeStruct((m, n), F32),
        compiler_params=_cp("parallel"),
        name="mm2_ln",
    )(xa, xb, w, res, g.reshape(1, n), b.reshape(1, n))


def _mm_ln_kernel(x_ref, w_ref, r_ref, g_ref, b_ref, o_ref):
    y = _dot(x_ref[...].astype(BF16), w_ref[...])
    o_ref[...] = _ln(DN_ALPHA * r_ref[...] + y, g_ref[...], b_ref[...])


def _mm_ln(x, w, res, g, b):
    m, k = x.shape
    n = w.shape[1]
    tm = min(m, 512)
    return pl.pallas_call(
        _mm_ln_kernel,
        grid=(m // tm,),
        in_specs=[pl.BlockSpec((tm, k), lambda i: (i, 0)),
                  pl.BlockSpec((k, n), lambda i: (0, 0)),
                  pl.BlockSpec((tm, n), lambda i: (i, 0)),
                  pl.BlockSpec((1, n), lambda i: (0, 0)),
                  pl.BlockSpec((1, n), lambda i: (0, 0))],
        out_specs=pl.BlockSpec((tm, n), lambda i: (i, 0)),
        out_shape=jax.ShapeDtypeStruct((m, n), F32),
        compiler_params=_cp("parallel"),
        name="mm_ln",
    )(x, w, res, g.reshape(1, n), b.reshape(1, n))


def _s5_kernel(u_ref, h0_ref, bd_ref, ar_ref, ai_ref, cbd_ref, d_ref, wg_ref,
               y_ref, hT_ref, hbuf, state, *tm_bufs, rows, steps):
    c = pl.program_id(0)

    @pl.when(c == 0)
    def _():
        state[...] = h0_ref[...]

    if tm_bufs:
        ubm, utm, ytm = tm_bufs
        nlb = S5_WIDTH // LANES
        for s in range(rows):
            us = u_ref[s]
            for kb in range(nlb):
                ubm[kb, s * steps:(s + 1) * steps, :] = us[:, kb * LANES:(kb + 1) * LANES]
        for t in range(steps):
            for kb in range(nlb):
                utm[t * rows:(t + 1) * rows, kb * LANES:(kb + 1) * LANES] = ubm[kb, pl.ds(t, rows, stride=steps), :]
        u = utm[...]
    else:
        u = u_ref[...]
    ub = u.astype(BF16)
    for part in range(2):
        for jt in range(S5_FLAT // MXU_TILE):
            c0 = part * S5_FLAT + jt * MXU_TILE
            k0 = (jt * MXU_TILE // S5_STATE * S5_GROUP) // LANES * LANES
            hbuf[:, c0:c0 + MXU_TILE] = _dot(ub[:, k0:k0 + LANES], bd_ref[k0:k0 + LANES, c0:c0 + MXU_TILE])
    ar = ar_ref[...]
    ai = ai_ref[...]

    def step(t, carry):
        hr, hi = carry
        r0 = pl.multiple_of(t * rows, rows)
        br = hbuf[pl.ds(r0, rows), 0:S5_FLAT]
        bi = hbuf[pl.ds(r0, rows), S5_FLAT:2 * S5_FLAT]
        nr = ar * hr - ai * hi + br
        ni = ar * hi + ai * hr + bi
        hbuf[pl.ds(r0, rows), 0:S5_FLAT] = nr
        hbuf[pl.ds(r0, rows), S5_FLAT:2 * S5_FLAT] = ni
        return nr, ni

    hr, hi = lax.fori_loop(0, steps, step, (state[:, 0:S5_FLAT], state[:, S5_FLAT:2 * S5_FLAT]))
    state[:, 0:S5_FLAT] = hr
    state[:, S5_FLAT:2 * S5_FLAT] = hi

    ys = []
    for nt in range(S5_WIDTH // MXU_TILE):
        n0 = nt * MXU_TILE
        kw = MXU_TILE // S5_GROUP * S5_STATE
        acc = None
        for part in range(2):
            k0 = part * S5_FLAT + nt * kw
            term = _dot(hbuf[:, k0:k0 + kw].astype(BF16), cbd_ref[k0:k0 + kw, n0:n0 + MXU_TILE])
            acc = term if acc is None else acc + term
        ys.append(acc)
    y = jnp.concatenate(ys, axis=-1) + d_ref[...] * u
    g = jax.nn.gelu(y)
    out = g * jax.nn.sigmoid(_dot(g.astype(BF16), wg_ref[...]))
    if tm_bufs:
        for kb in range(nlb):
            ytm[kb] = out[:, kb * LANES:(kb + 1) * LANES]
        for s in range(rows):
            y_ref[s] = jnp.concatenate([ytm[kb, pl.ds(s, steps, stride=rows), :] for kb in range(nlb)],
                                       axis=-1).astype(y_ref.dtype)
    else:
        y_ref[...] = out.astype(y_ref.dtype)

    @pl.when(c == pl.num_programs(0) - 1)
    def _():
        hT_ref[...] = state[...]


def _s5(u_tm, h0, prm, rows, steps):
    n = u_tm.shape[0]
    blk = rows * steps
    bd, ar, ai, cbd, d, wg = prm
    const = lambda shape: pl.BlockSpec(shape, lambda c: (0, 0))
    return pl.pallas_call(
        functools.partial(_s5_kernel, rows=rows, steps=steps),
        grid=(n // blk,),
        in_specs=[pl.BlockSpec((blk, S5_WIDTH), lambda c: (c, 0)),
                  const((rows, 2 * S5_FLAT)),
                  const((S5_WIDTH, 2 * S5_FLAT)),
                  const((1, S5_FLAT)), const((1, S5_FLAT)),
                  const((2 * S5_FLAT, S5_WIDTH)),
                  const((1, S5_WIDTH)),
                  const((S5_WIDTH, S5_WIDTH))],
        out_specs=[pl.BlockSpec((blk, S5_WIDTH), lambda c: (c, 0)),
                   const((rows, 2 * S5_FLAT))],
        out_shape=[jax.ShapeDtypeStruct((n, S5_WIDTH), BF16),
                   jax.ShapeDtypeStruct((rows, 2 * S5_FLAT), F32)],
        scratch_shapes=[pltpu.VMEM((blk, 2 * S5_FLAT), F32),
                        pltpu.VMEM((rows, 2 * S5_FLAT), F32)],
        compiler_params=_cp("arbitrary"),
        name="s5",
    )(u_tm, h0, bd, ar, ai, cbd, d, wg)


def _s5_prompt(proj, h0, prm, steps):
    bsz, seq, _ = proj.shape
    blk = bsz * steps
    bd, ar, ai, cbd, d, wg = prm
    const = lambda shape: pl.BlockSpec(shape, lambda c: (0, 0))
    seqblk = pl.BlockSpec((bsz, steps, S5_WIDTH), lambda c: (0, c, 0))
    return pl.pallas_call(
        functools.partial(_s5_kernel, rows=bsz, steps=steps),
        grid=(seq // steps,),
        in_specs=[seqblk,
                  const((bsz, 2 * S5_FLAT)),
                  const((S5_WIDTH, 2 * S5_FLAT)),
                  const((1, S5_FLAT)), const((1, S5_FLAT)),
                  const((2 * S5_FLAT, S5_WIDTH)),
                  const((1, S5_WIDTH)),
                  const((S5_WIDTH, S5_WIDTH))],
        out_specs=[seqblk, const((bsz, 2 * S5_FLAT))],
        out_shape=[jax.ShapeDtypeStruct((bsz, seq, S5_WIDTH), BF16),
                   jax.ShapeDtypeStruct((bsz, 2 * S5_FLAT), F32)],
        scratch_shapes=[pltpu.VMEM((blk, 2 * S5_FLAT), F32),
                        pltpu.VMEM((bsz, 2 * S5_FLAT), F32),
                        pltpu.VMEM((S5_WIDTH // LANES, blk, LANES), F32),
                        pltpu.VMEM((blk, S5_WIDTH), F32),
                        pltpu.VMEM((S5_WIDTH // LANES, blk, LANES), F32)],
        compiler_params=_cp("arbitrary"),
        name="s5_prompt",
    )(proj, h0, bd, ar, ai, cbd, d, wg)


def _s5_params(lam_re, lam_im, log_dt, b_re, b_im, c_re, c_im, d_skip, w_glu):
    dt = jnp.exp(log_dt)[:, None]
    mag = jnp.exp(lam_re * dt)
    ab_re, ab_im = mag * jnp.cos(lam_im * dt), mag * jnp.sin(lam_im * dt)
    den = lam_re * lam_re + lam_im * lam_im
    nr, ni = ab_re - 1.0, ab_im
    coef_re = ((nr * lam_re + ni * lam_im) / den)[..., None]
    coef_im = ((ni * lam_re - nr * lam_im) / den)[..., None]
    bb_re = coef_re * b_re - coef_im * b_im
    bb_im = coef_re * b_im + coef_im * b_re
    eye = jnp.eye(S5_GROUPS, dtype=F32)
    pack_in = lambda bb: jnp.einsum('gpc,gh->gchp', bb, eye).reshape(S5_WIDTH, S5_FLAT)
    bd = jnp.concatenate([pack_in(bb_re), pack_in(bb_im)], axis=1).astype(BF16)
    pack_out = lambda cc: jnp.einsum('gcp,gh->gphc', cc, eye).reshape(S5_FLAT, S5_WIDTH)
    cbd = jnp.concatenate([pack_out(c_re), -pack_out(c_im)], axis=0).astype(BF16)
    return (bd, ab_re.reshape(1, S5_FLAT), ab_im.reshape(1, S5_FLAT), cbd,
            d_skip.reshape(1, S5_WIDTH), w_glu.astype(BF16))


def _swa_kernel(sink_ref, q_ref, kp_ref, kc_ref, vp_ref, vc_ref, o_ref):
    n = pl.program_id(1)
    q = q_ref[0].astype(BF16)
    k = jnp.concatenate([kp_ref[0], kc_ref[0]], axis=0).astype(BF16)
    v = jnp.concatenate([vp_ref[0], vc_ref[0]], axis=0).astype(BF16)
    qi = lax.broadcasted_iota(jnp.int32, (WINDOW, 2 * WINDOW), 0)
    kj = lax.broadcasted_iota(jnp.int32, (WINDOW, 2 * WINDOW), 1)
    dist = WINDOW + qi - kj
    valid = (dist >= 0) & (dist < WINDOW) & ((n > 0) | (kj >= WINDOW))
    distf = dist.astype(F32)
    outs = []
    for h in range(SWA_HEADS):
        g = h // SWA_REP
        qh = q[:, h * SWA_HEAD_DIM:(h + 1) * SWA_HEAD_DIM]
        kg = k[:, g * SWA_HEAD_DIM:(g + 1) * SWA_HEAD_DIM]
        vg = v[:, g * SWA_HEAD_DIM:(g + 1) * SWA_HEAD_DIM]
        s = _dot_nt(qh, kg) * (SWA_HEAD_DIM ** -0.5)
        s = s - (2.0 ** -(h + 1)) * distf
        s = jnp.where(valid, s, NEG_INF)
        sink = sink_ref[h]
        m = jnp.maximum(jnp.max(s, axis=-1, keepdims=True), sink)
        p = jnp.exp(s - m)
        p = p / (jnp.sum(p, axis=-1, keepdims=True) + jnp.exp(sink - m))
        outs.append(_dot(p.astype(BF16), vg))
    o_ref[0] = jnp.concatenate(outs, axis=-1).astype(o_ref.dtype)


def _swa_prompt(proj, sinks):
    bsz, seq, _ = proj.shape
    kcol = (S5_WIDTH + SWA_WIDTH) // SWA_KV_WIDTH
    prev = lambda b, n: (b, jnp.maximum(n - 1, 0), kcol)
    prev_v = lambda b, n: (b, jnp.maximum(n - 1, 0), kcol + 1)
    blk = (1, WINDOW, SWA_KV_WIDTH)
    return pl.pallas_call(
        _swa_kernel,
        grid=(bsz, seq // WINDOW),
        in_specs=[pl.BlockSpec(memory_space=pltpu.SMEM),
                  pl.BlockSpec((1, WINDOW, SWA_WIDTH), lambda b, n: (b, n, 1)),
                  pl.BlockSpec(blk, prev),
                  pl.BlockSpec(blk, lambda b, n: (b, n, kcol)),
                  pl.BlockSpec(blk, prev_v),
                  pl.BlockSpec(blk, lambda b, n: (b, n, kcol + 1))],
        out_specs=pl.BlockSpec((1, WINDOW, SWA_WIDTH), lambda b, n: (b, n, 0)),
        out_shape=jax.ShapeDtypeStruct((bsz, seq, SWA_WIDTH), BF16),
        compiler_params=_cp("parallel", "arbitrary"),
        name="swa_prompt",
    )(sinks, proj, proj, proj, proj, proj)


def _per_head_column(values):
    r = lax.broadcasted_iota(jnp.int32, (SWA_REP, 1), 0)
    col = jnp.full((SWA_REP, 1), values[-1], F32)
    for i in range(SWA_REP - 2, -1, -1):
        col = jnp.where(r == i, values[i], col)
    return col


def _swa_step_kernel(sink_ref, p_ref, kc_ref, vc_ref, o_ref, *, nseq, start):
    kj = lax.broadcasted_iota(jnp.int32, (SWA_REP, WINDOW), 1)
    dist = WINDOW - kj
    valid = (dist < WINDOW) & (start - WINDOW + kj >= 0)
    distf = dist.astype(F32)
    for i in range(nseq):
        row = p_ref[i]
        kn = row[:, S5_WIDTH + SWA_WIDTH:S5_WIDTH + SWA_WIDTH + SWA_KV_WIDTH].astype(BF16)
        vn = row[:, S5_WIDTH + SWA_WIDTH + SWA_KV_WIDTH:].astype(BF16)
        kc = kc_ref[i].astype(BF16)
        vc = vc_ref[i].astype(BF16)
        outs = []
        for g in range(SWA_KV_HEADS):
            q4 = jnp.concatenate(
                [row[:, S5_WIDTH + (g * SWA_REP + r) * SWA_HEAD_DIM:S5_WIDTH + (g * SWA_REP + r + 1) * SWA_HEAD_DIM]
                 for r in range(SWA_REP)], axis=0).astype(BF16)
            sl = slice(g * SWA_HEAD_DIM, (g + 1) * SWA_HEAD_DIM)
            slope = _per_head_column([2.0 ** -(g * SWA_REP + r + 1) for r in range(SWA_REP)])
            sink = _per_head_column([sink_ref[g * SWA_REP + r] for r in range(SWA_REP)])
            scale = SWA_HEAD_DIM ** -0.5
            sc = _dot_nt(q4, kc[:, sl]) * scale - slope * distf
            sc = jnp.where(valid, sc, NEG_INF)
            sn = jnp.sum(q4.astype(F32) * kn[:, sl].astype(F32), axis=-1, keepdims=True) * scale
            m = jnp.maximum(jnp.maximum(jnp.max(sc, axis=-1, keepdims=True), sn), sink)
            pc = jnp.exp(sc - m)
            pn = jnp.exp(sn - m)
            den = jnp.sum(pc, axis=-1, keepdims=True) + pn + jnp.exp(sink - m)
            o4 = _dot((pc / den).astype(BF16), vc[:, sl])
            o4 = o4 + (pn / den).astype(BF16).astype(F32) * vn[:, sl].astype(F32)
            outs.extend([o4[r:r + 1, :] for r in range(SWA_REP)])
        o_ref[i] = jnp.concatenate(outs, axis=-1).astype(o_ref.dtype)


def _swa_step(proj, k_cache, v_cache, sinks, start):
    n = proj.shape[0]
    nseq = min(n, SUBLANES)
    out = pl.pallas_call(
        functools.partial(_swa_step_kernel, nseq=nseq, start=start),
        grid=(n // nseq,),
        in_specs=[pl.BlockSpec(memory_space=pltpu.SMEM),
                  pl.BlockSpec((nseq, 1, proj.shape[1]), lambda i: (i, 0, 0)),
                  pl.BlockSpec((nseq, WINDOW, SWA_KV_WIDTH), lambda i: (i, 0, 0)),
                  pl.BlockSpec((nseq, WINDOW, SWA_KV_WIDTH), lambda i: (i, 0, 0))],
        out_specs=pl.BlockSpec((nseq, 1, SWA_WIDTH), lambda i: (i, 0, 0)),
        out_shape=jax.ShapeDtypeStruct((n, 1, SWA_WIDTH), BF16),
        compiler_params=_cp("parallel"),
        name="swa_step",
    )(sinks, proj.reshape(n, 1, -1), k_cache, v_cache)
    return out.reshape(n, SWA_WIDTH)


def _head_expand(width):
    r = lax.broadcasted_iota(jnp.int32, (LANES, SSD_HEADS * width), 0)
    c = lax.broadcasted_iota(jnp.int32, (LANES, SSD_HEADS * width), 1)
    return jnp.where((c >= r * width) & (c < (r + 1) * width), 1.0, 0.0).astype(BF16)


def _ssd_kernel(x_ref, wz_ref, wx_ref, wdt_ref, cw_ref, cb_ref, dtb_ref, a_ref, d_ref, nw_ref,
                wo_ref, g_ref, b_ref, o_ref, st_ref, conv_ref, tail, state, ybuf):
    c = pl.program_id(1)
    q = SSD_CHUNK

    @pl.when(c == 0)
    def _():
        tail[...] = jnp.zeros_like(tail)
        state[...] = jnp.zeros_like(state)

    xin = x_ref[0]
    xb = xin.astype(BF16)
    cur = _dot(xb, wx_ref[...])
    tl = tail[...]
    row8 = lax.broadcasted_iota(jnp.int32, (SUBLANES, SSD_CONV_DIM), 0)
    acc = None
    for tap in range(SSD_CONV):
        sh = SSD_CONV - 1 - tap
        if sh == 0:
            term = cur
        else:
            rolled = pltpu.roll(cur, sh, 0)
            head = jnp.where(row8 < sh, pltpu.roll(tl, sh, 0), rolled[0:SUBLANES])
            term = jnp.concatenate([head, rolled[SUBLANES:]], axis=0)
        term = term * cw_ref[tap:tap + 1, :]
        acc = term if acc is None else acc + term
    tail[...] = cur[q - SUBLANES:q]
    xbc = _silu(acc + cb_ref[...])
    xs = xbc[:, 0:SSD_INNER]
    bm = xbc[:, SSD_INNER:SSD_INNER + SSD_GROUPS * SSD_STATE].astype(BF16)
    cm = xbc[:, SSD_INNER + SSD_GROUPS * SSD_STATE:].astype(BF16)

    dt = _softplus(_dot(xb, wdt_ref[...]) + dtb_ref[...])
    da = dt * a_ref[...]
    ri = lax.broadcasted_iota(jnp.int32, (q, q), 0)
    ci = lax.broadcasted_iota(jnp.int32, (q, q), 1)
    causal = ri >= ci
    tri = jnp.where(causal, 1.0, 0.0).astype(BF16)
    da1, da2, da3 = _split3(da)
    acs = _dot(tri, da1) + _dot(tri, da2) + _dot(tri, da3)
    acs_t = acs.T
    e64 = _head_expand(SSD_HEAD_DIM)
    dt_x = _dot_exact01(dt, e64)
    acs_x = _dot_exact01(acs, e64)
    last_x = acs_x[q - 1:q, :]
    dx = dt_x * xs
    dsx = (jnp.exp(last_x - acs_x) * dx).astype(BF16)
    eacs_x = jnp.exp(acs_x)
    dxb = dx.astype(BF16)

    for g in range(SSD_GROUPS):
        ns = slice(g * SSD_STATE, (g + 1) * SSD_STATE)
        gs = slice(g * SSD_GW, (g + 1) * SSD_GW)
        cb = _dot_nt(cm[:, ns], bm[:, ns])
        for r in range(SSD_HPG):
            h = g * SSD_HPG + r
            seg = acs[:, h:h + 1] - acs_t[h:h + 1, :]
            decay = jnp.where(causal, jnp.exp(jnp.where(causal, seg, 0.0)), 0.0)
            hs = slice(h * SSD_HEAD_DIM, (h + 1) * SSD_HEAD_DIM)
            ybuf[:, hs] = _dot((cb * decay).astype(BF16), dxb[:, hs])
        s_in = state[g]
        y_off = _dot(cm[:, ns], s_in.astype(BF16)) * eacs_x[:, gs]
        ybuf[:, gs] = ybuf[:, gs] + y_off
        state[g] = jnp.exp(last_x[:, gs]) * s_in + _dot_tn(bm[:, ns], dsx[:, gs])

    y = ybuf[...] + d_ref[...] * xs
    y = y * _silu(_dot(xb, wz_ref[...]))
    y = y * lax.rsqrt(jnp.mean(y * y, axis=-1, keepdims=True) + RMS_EPS) * nw_ref[...]
    mix = _dot(y.astype(BF16), wo_ref[...])
    o_ref[0] = _ln(DN_ALPHA * xin + mix, g_ref[...], b_ref[...])

    @pl.when(c == pl.num_programs(1) - 1)
    def _():
        conv_ref[0] = cur[q - SUBLANES:q]
        for g in range(SSD_GROUPS):
            st_ref[0, g * SSD_GW:(g + 1) * SSD_GW, :] = state[g].T


def _ssd_prompt(x, w3, prm, w_out, g, b):
    bsz, seq, _ = x.shape
    w_z, w_x, w_dt = w3
    cw, cb, dtb, a, d_x, nw = prm
    const = lambda shape: pl.BlockSpec(shape, lambda b, c: (0, 0))
    seqblk = lambda w: pl.BlockSpec((1, SSD_CHUNK, w), lambda b, c: (b, c, 0))
    return pl.pallas_call(
        _ssd_kernel,
        grid=(bsz, seq // SSD_CHUNK),
        in_specs=[seqblk(D_MODEL),
                  const((D_MODEL, SSD_INNER)), const((D_MODEL, SSD_CONV_DIM)), const((D_MODEL, LANES)),
                  const((SSD_CONV, SSD_CONV_DIM)), const((1, SSD_CONV_DIM)),
                  const((1, LANES)), const((1, LANES)),
                  const((1, SSD_INNER)), const((1, SSD_INNER)),
                  const((SSD_INNER, D_MODEL)), const((1, D_MODEL)), const((1, D_MODEL))],
        out_specs=[seqblk(D_MODEL),
                   pl.BlockSpec((1, SSD_INNER, SSD_STATE), lambda b, c: (b, 0, 0)),
                   pl.BlockSpec((1, SUBLANES, SSD_CONV_DIM), lambda b, c: (b, 0, 0))],
        out_shape=[jax.ShapeDtypeStruct((bsz, seq, D_MODEL), F32),
                   jax.ShapeDtypeStruct((bsz, SSD_INNER, SSD_STATE), F32),
                   jax.ShapeDtypeStruct((bsz, SUBLANES, SSD_CONV_DIM), F32)],
        scratch_shapes=[pltpu.VMEM((SUBLANES, SSD_CONV_DIM), F32),
                        pltpu.VMEM((SSD_GROUPS, SSD_STATE, SSD_GW), F32),
                        pltpu.VMEM((SSD_CHUNK, SSD_INNER), F32)],
        compiler_params=_cp("parallel", "arbitrary"),
        name="ssd_prompt",
    )(x, w_z, w_x, w_dt, cw, cb, dtb, a, d_x, nw, w_out, g.reshape(1, -1), b.reshape(1, -1))


def _pad_rows(row):
    return jnp.concatenate([row, jnp.zeros((SUBLANES - 1, row.shape[1]), row.dtype)], axis=0)


def _ssd_step_kernel(z_ref, x_ref, dt_ref, buf_ref, s0_ref, cw_ref, cb_ref, dtb_ref, a_ref, d_ref, nw_ref,
                     y_ref, s1_ref, *, nseq):
    e64 = _head_expand(SSD_HEAD_DIM)
    ones3 = jnp.where(lax.broadcasted_iota(jnp.int32, (SUBLANES, SSD_STATE), 0) < 3, 1.0, 0.0).astype(BF16)
    for i in range(nseq):
        buf = buf_ref[i]
        acc = buf[0:1] * cw_ref[0:1, :]
        acc = acc + buf[1:2] * cw_ref[1:2, :]
        acc = acc + buf[2:3] * cw_ref[2:3, :]
        acc = acc + x_ref[i] * cw_ref[3:4, :]
        xbc = _silu(acc + cb_ref[...])
        xs = xbc[:, 0:SSD_INNER]
        bm = xbc[:, SSD_INNER:SSD_INNER + SSD_GROUPS * SSD_STATE].astype(BF16)
        cm = xbc[:, SSD_INNER + SSD_GROUPS * SSD_STATE:].astype(BF16)
        dt = _softplus(dt_ref[i] + dtb_ref[...])
        ed = jnp.exp(dt * a_ref[...])
        dt_x = _dot_exact01(_pad_rows(dt), e64)[0:1]
        ed_x = _dot_exact01(_pad_rows(ed), e64)[0:1]
        dx = dt_x * xs
        dxb = dx.astype(BF16)
        ys = []
        for g in range(SSD_GROUPS):
            ns = slice(g * SSD_STATE, (g + 1) * SSD_STATE)
            gs = slice(g * SSD_GW, (g + 1) * SSD_GW)
            s0 = s0_ref[i, gs, :]
            cmg = _pad_rows(cm[:, ns])
            cbv = jnp.sum(cm[:, ns].astype(F32) * bm[:, ns].astype(F32), axis=-1, keepdims=True)
            y_diag = cbv.astype(BF16).astype(F32) * dxb[:, gs].astype(F32)
            y_off = _dot_nt(cmg, s0.astype(BF16))[0:1] * ed_x[:, gs]
            ys.append(y_diag + y_off)
            e1, e2, e3 = _split3(ed_x[:, gs])
            ed_rows = jnp.concatenate([e1, e2, e3, jnp.zeros((SUBLANES - 3, SSD_GW), BF16)], axis=0)
            ed_full = _dot_tn(ed_rows, ones3)
            outer = _dot_tn(_pad_rows(dxb[:, gs]), _pad_rows(bm[:, ns]))
            s1_ref[i, gs, :] = ed_full * s0 + outer
        y = jnp.concatenate(ys, axis=-1) + d_ref[...] * xs
        y = y * _silu(z_ref[i])
        y = y * lax.rsqrt(jnp.mean(y * y, axis=-1, keepdims=True) + RMS_EPS) * nw_ref[...]
        y_ref[i] = y.astype(y_ref.dtype)


def _ssd_step(z, xbc, dtraw, conv_buf, s0_all, layer, prm):
    n = z.shape[0]
    nseq = min(n, 4)
    cw, cb, dtb, a, d_x, nw = prm
    const = lambda shape: pl.BlockSpec(shape, lambda i: (0, 0))
    tok = lambda w: pl.BlockSpec((nseq, 1, w), lambda i: (i, 0, 0))
    y, s1 = pl.pallas_call(
        functools.partial(_ssd_step_kernel, nseq=nseq),
        grid=(n // nseq,),
        in_specs=[tok(SSD_INNER), tok(SSD_CONV_DIM), tok(LANES),
                  pl.BlockSpec((nseq, SSD_CONV - 1, SSD_CONV_DIM), lambda i: (i, 0, 0)),
                  pl.BlockSpec((None, nseq, SSD_INNER, SSD_STATE), lambda i: (layer, i, 0, 0)),
                  const((SSD_CONV, SSD_CONV_DIM)), const((1, SSD_CONV_DIM)),
                  const((1, LANES)), const((1, LANES)),
                  const((1, SSD_INNER)), const((1, SSD_INNER))],
        out_specs=[tok(SSD_INNER),
                   pl.BlockSpec((nseq, SSD_INNER, SSD_STATE), lambda i: (i, 0, 0))],
        out_shape=[jax.ShapeDtypeStruct((n, 1, SSD_INNER), BF16),
                   jax.ShapeDtypeStruct((n, SSD_INNER, SSD_STATE), F32)],
        compiler_params=_cp("parallel"),
        name="ssd_step",
    )(z.reshape(n, 1, -1), xbc.reshape(n, 1, -1), dtraw.reshape(n, 1, -1), conv_buf, s0_all,
      cw, cb, dtb, a, d_x, nw)
    return y.reshape(n, SSD_INNER), s1


def _ssd_params(conv_w, conv_b, dt_bias, a_log, d_skip, norm_w):
    pad = lambda v: jnp.pad(v, (0, LANES - SSD_HEADS)).reshape(1, LANES)
    return (conv_w, conv_b.reshape(1, -1), pad(dt_bias), pad(-jnp.exp(a_log)),
            jnp.repeat(d_skip, SSD_HEAD_DIM).reshape(1, SSD_INNER), norm_w.reshape(1, SSD_INNER))


PEER_TE = 2048
PEER_ROWS = PEER_TE // PEER_KEYS


PEER_CHUNK_ROWS = 4
PEER_TOK_CHUNK = 256
PACK = 2 * SUBLANES


def _batcher_sort_pairs(n):
    def merge(lo, hi, r):
        step = r * 2
        if step < hi - lo:
            yield from merge(lo, hi, step)
            yield from merge(lo + r, hi, step)
            for i in range(lo + r, hi - r, step):
                yield (i, i + r)
        else:
            yield (lo, lo + r)

    def sort(lo, hi):
        if hi - lo >= 1:
            mid = lo + (hi - lo) // 2
            yield from sort(lo, mid)
            yield from sort(mid + 1, hi)
            yield from merge(lo, hi, 1)

    return tuple(sort(0, n - 1))


_SORT16 = _batcher_sort_pairs(PEER_TOPK)
_BITONIC16 = tuple((i, i + d) for d in (8, 4, 2, 1) for i in range(PEER_TOPK) if not i & d)


def _mx(a, b):
    return b if a is None else a if b is None else jnp.maximum(a, b)


def _mn(a, b):
    return None if a is None or b is None else jnp.minimum(a, b)


def _top16(rows):
    v = list(rows) + [None] * (PEER_TOPK - len(rows))
    for i, j in _SORT16:
        v[i], v[j] = _mx(v[i], v[j]), _mn(v[i], v[j])
    for shift in (4, 2, 1):
        w = [None if x is None else pltpu.roll(x, shift, 0) for x in v]
        v = [_mx(v[k], w[PEER_TOPK - 1 - k]) for k in range(PEER_TOPK)]
        for i, j in _BITONIC16:
            v[i], v[j] = _mx(v[i], v[j]), _mn(v[i], v[j])
    return v


def _prefix_count(test, thr):
    m8 = test(thr[7])
    m4 = test(jnp.where(m8, thr[11], thr[3]))
    m2 = test(jnp.where(m8, jnp.where(m4, thr[13], thr[9]), jnp.where(m4, thr[5], thr[1])))
    hi = jnp.where(m4, jnp.where(m2, thr[14], thr[12]), jnp.where(m2, thr[10], thr[8]))
    lo = jnp.where(m4, jnp.where(m2, thr[6], thr[4]), jnp.where(m2, thr[2], thr[0]))
    m1 = test(jnp.where(m8, hi, lo))
    return (jnp.where(m8, 8.0, 0.0) + jnp.where(m4, 4.0, 0.0)) + (jnp.where(m2, 2.0, 0.0) + jnp.where(m1, 1.0, 0.0))


def _peer_route(xb, wq_ref, k1_ref, k2_ref, c_ref, e1_ref, rk_ref, e2_ref):
    tm = xb.shape[0]
    q = _dot(xb, wq_ref[...]).astype(BF16)
    half = PEER_DK // 2
    nblk = PEER_KEYS // SUBLANES
    sub = lax.broadcasted_iota(jnp.int32, (SUBLANES, tm), 0)

    def by_sublane(vals):
        out = vals[SUBLANES - 1]
        for s in range(SUBLANES - 2, -1, -1):
            out = jnp.where(sub == s, vals[s], out)
        return out

    for h in range(PEER_HEADS):
        s1 = _dot_nt(k1_ref[h], q[:, h * PEER_DK:h * PEER_DK + half])
        s2 = _dot_nt(k2_ref[h], q[:, h * PEER_DK + half:(h + 1) * PEER_DK])
        r1 = [s1[k * SUBLANES:(k + 1) * SUBLANES] for k in range(nblk)]
        r2 = [s2[k * SUBLANES:(k + 1) * SUBLANES] for k in range(nblk)]
        a = _top16(r1)
        b = _top16(r2)
        r1 = [x - a[0] for x in r1]
        r2 = [x - b[0] for x in r2]
        a = [x - a[0] for x in a]
        b = [x - b[0] for x in b]
        b_lo, b_hi, a_hi = by_sublane(b[0:8]), by_sublane(b[8:16]), by_sublane(a[8:16])
        cand = [a[0] + b_lo, a[0] + b_hi] + [a[i] + b_lo for i in range(1, 8)] + [a_hi + b[0]]
        top = _top16(cand)
        tau = top[PEER_TOPK - 1]
        zsum = jnp.exp(top[0])
        for k in range(1, PEER_TOPK):
            zsum = zsum + jnp.exp(top[k])
        inv_z = 1.0 / zsum
        for k in range(nblk):
            x1 = r1[k]
            cnt = _prefix_count(lambda t: (x1 + t) >= tau, b[0:15])
            cnt = cnt + jnp.where((x1 + b[15]) >= tau, 1.0, 0.0)
            c_ref[h, k, :, 0:tm] = cnt
            e1_ref[h, k, :, 0:tm] = jnp.exp(x1) * inv_z
        ranks, e2s = [], []
        for k in range(nblk):
            x2 = r2[k]
            rnk = _prefix_count(lambda t: t > x2, b[0:15])
            ranks.append(jnp.where(x2 >= b[15], rnk, float(PEER_TOPK)))
            e2s.append(jnp.exp(x2))
        rk_ref[h, :, 0:tm] = jnp.concatenate(ranks, axis=0).astype(BF16)
        e2_ref[h, :, 0:tm] = jnp.concatenate(e2s, axis=0).astype(BF16)


def _gelu_tanh(x):
    c0 = math.sqrt(2.0 / math.pi)
    return (0.5 * x) * (1.0 + jnp.tanh(x * (c0 + (c0 * 0.044715) * (x * x))))


def _peer_kernel(x_ref, xp_ref, wq_ref, k1_ref, k2_ref, u_ref, vt_ref, g_ref, b_ref, o_ref,
                 acc, c_ref, e1_ref, rk_ref, e2_ref, act_ref):
    i = pl.program_id(0)
    j = pl.program_id(1)
    nblk = pl.num_programs(0) - 1
    tm = x_ref.shape[0]
    xb = x_ref[...].astype(BF16)

    @pl.when((i == 0) & (j == 0))
    def _():
        acc[...] = jnp.zeros_like(acc)
        act_ref[1] = jnp.zeros((PEER_TE, tm), BF16)

    @pl.when((j == 0) & (i < nblk))
    def _():
        _peer_route(xb, wq_ref, k1_ref, k2_ref, c_ref, e1_ref, rk_ref, e2_ref)

    nb2 = PEER_KEYS // PACK
    slot = j % 2
    crows = PEER_CHUNK_ROWS
    nch = PEER_ROWS // crows
    tc = min(tm, PEER_TOK_CHUNK)
    apply_after = {nch // 2 - 1 + k: k * tc for k in range(tm // tc)}

    @pl.when((i < nblk) | (j == 0))
    def _():
        for ch in range(nch):
            e0 = ch * crows * PEER_KEYS
            h_t = _dot_nt(u_ref[e0:e0 + crows * PEER_KEYS, :], xb)
            if ch in apply_after:
                t0 = apply_after[ch]
                acc[:, t0:t0 + tc] += _dot(vt_ref[...], act_ref[1 - slot, :, t0:t0 + tc])
            rows = [ch * crows + ri for ri in range(crows)]
            for tl in range(tm // LANES):
                ts = slice(tl * LANES, (tl + 1) * LANES)
                gate = [[None] * nb2 for _ in range(crows)]
                for h in range(PEER_HEADS):
                    blk = [j * (PEER_ROWS // SUBLANES) + r // SUBLANES for r in rows]
                    sub = [r % SUBLANES for r in rows]
                    cbv = [jnp.broadcast_to(c_ref[h, k, s:s + 1, ts], (PACK, LANES)).astype(BF16)
                           for k, s in zip(blk, sub)]
                    ebv = [jnp.broadcast_to(e1_ref[h, k, s:s + 1, ts], (PACK, LANES)).astype(BF16)
                           for k, s in zip(blk, sub)]
                    for b2 in range(nb2):
                        ks = slice(b2 * PACK, (b2 + 1) * PACK)
                        rk = rk_ref[h, ks, ts]
                        e2 = e2_ref[h, ks, ts]
                        for ri in range(crows):
                            w = jnp.where(rk < cbv[ri], ebv[ri] * e2, 0.0)
                            gate[ri][b2] = w if gate[ri][b2] is None else gate[ri][b2] + w
                for ri in range(crows):
                    for b2 in range(nb2):
                        lo = ri * PEER_KEYS + b2 * PACK
                        hv = _gelu_tanh(h_t[lo:lo + PACK, ts].astype(BF16))
                        act_ref[slot, e0 + lo:e0 + lo + PACK, ts] = hv * gate[ri][b2]

    @pl.when(j == 0)
    def _():
        o_ref[...] = _ln(DN_ALPHA * xp_ref[...] + acc[...].T, g_ref[...], b_ref[...])
        acc[...] = jnp.zeros_like(acc)


def _peer_ln(x, wq, k1, k2, u_all, vt_all, layer, g, b):
    t = x.shape[0]
    tm = min(t, 512)
    nblk = t // tm
    ntile = PEER_EXPERTS // PEER_TE
    const2 = lambda shape: pl.BlockSpec(shape, lambda i, j: (0, 0))
    const3 = lambda shape: pl.BlockSpec(shape, lambda i, j: (0, 0, 0))
    cur = lambda i, j: (jnp.minimum(i, nblk - 1), 0)
    prev = lambda i, j: (jnp.maximum(i - 1, 0), 0)
    return pl.pallas_call(
        _peer_kernel,
        grid=(nblk + 1, ntile),
        in_specs=[pl.BlockSpec((tm, D_MODEL), cur),
                  pl.BlockSpec((tm, D_MODEL), prev),
                  const2((D_MODEL, PEER_HEADS * PEER_DK)),
                  const3((PEER_HEADS, PEER_KEYS, PEER_DK // 2)),
                  const3((PEER_HEADS, PEER_KEYS, PEER_DK // 2)),
                  pl.BlockSpec((None, PEER_TE, D_MODEL), lambda i, j: (layer, jnp.where(i < nblk, j, 0), 0)),
                  pl.BlockSpec((None, D_MODEL, PEER_TE),
                               lambda i, j: (layer, 0, jnp.where(i < nblk, (j + ntile - 1) % ntile, ntile - 1))),
                  const2((1, D_MODEL)), const2((1, D_MODEL))],
        out_specs=pl.BlockSpec((tm, D_MODEL), lambda i, j: (jnp.where(j == 0, prev(i, j)[0], cur(i, j)[0]), 0)),
        out_shape=jax.ShapeDtypeStruct((t, D_MODEL), F32),
        scratch_shapes=[pltpu.VMEM((D_MODEL, tm), F32),
                        pltpu.VMEM((PEER_HEADS, PEER_KEYS // SUBLANES, SUBLANES, tm), F32),
                        pltpu.VMEM((PEER_HEADS, PEER_KEYS // SUBLANES, SUBLANES, tm), F32),
                        pltpu.VMEM((PEER_HEADS, PEER_KEYS, tm), BF16),
                        pltpu.VMEM((PEER_HEADS, PEER_KEYS, tm), BF16),
                        pltpu.VMEM((2, PEER_TE, tm), BF16)],
        compiler_params=_cp("arbitrary", "arbitrary"),
        name="peer",
    )(x, x, wq, k1, k2, u_all, vt_all, g.reshape(1, -1), b.reshape(1, -1))


def _peer_tables_kernel(u_ref, v_ref, ub_ref, vt_ref):
    ub_ref[...] = u_ref[...].astype(BF16)
    vt_ref[...] = v_ref[...].T.astype(BF16)


def _peer_tables(peer_u, peer_v):
    nl, ne, nd = peer_u.shape
    tile = pl.BlockSpec((None, D_MODEL, nd), lambda l, e: (l, e, 0))
    return pl.pallas_call(
        _peer_tables_kernel,
        grid=(nl, ne // D_MODEL),
        in_specs=[tile, tile],
        out_specs=[tile, pl.BlockSpec((None, nd, D_MODEL), lambda l, e: (l, 0, e))],
        out_shape=[jax.ShapeDtypeStruct((nl, ne, nd), BF16),
                   jax.ShapeDtypeStruct((nl, nd, ne), BF16)],
        compiler_params=_cp("parallel", "parallel"),
        name="peer_tables",
    )(peer_u, peer_v)


def _even_prompt(x, bsz, seq, w_in, s5p, sinks, w_out, g, b):
    proj = _mm(x, w_in, tm=1024, tn=D_IN_EVEN)
    p3 = proj.reshape(bsz, seq, -1)
    y_a, h_t = _s5_prompt(p3, jnp.zeros((bsz, 2 * S5_FLAT), F32), s5p, min(seq, 64))
    y_a = y_a.reshape(bsz * seq, S5_WIDTH)
    y_b = _swa_prompt(p3, sinks).reshape(bsz * seq, SWA_WIDTH)
    x = _mm2_ln(y_a, y_b, w_out, x, g, b)
    kv = p3[:, seq - WINDOW:, S5_WIDTH + SWA_WIDTH:]
    new_k = kv[..., :SWA_KV_WIDTH].reshape(bsz, WINDOW, SWA_KV_HEADS, SWA_HEAD_DIM)
    new_v = kv[..., SWA_KV_WIDTH:].reshape(bsz, WINDOW, SWA_KV_HEADS, SWA_HEAD_DIM)
    h_re = h_t[:, :S5_FLAT].reshape(bsz, S5_GROUPS, S5_STATE)
    h_im = h_t[:, S5_FLAT:].reshape(bsz, S5_GROUPS, S5_STATE)
    return x, h_re, h_im, new_k, new_v


def _even_sample(x, h_re, h_im, k_buf, v_buf, start, w_in, s5p, sinks, w_out, g, b):
    n = x.shape[0]
    proj = _mm(x, w_in, tm=1024, tn=D_IN_EVEN)
    h0 = jnp.concatenate([h_re.reshape(n, S5_FLAT), h_im.reshape(n, S5_FLAT)], axis=1)
    y_a, h_t = _s5(proj[:, :S5_WIDTH], h0, s5p, n, 1)
    kc = k_buf.reshape(n, WINDOW, SWA_KV_WIDTH)
    vc = v_buf.reshape(n, WINDOW, SWA_KV_WIDTH)
    y_b = _swa_step(proj, kc, vc, sinks, start)
    x = _mm2_ln(y_a, y_b, w_out, x, g, b)
    kn = proj[:, None, S5_WIDTH + SWA_WIDTH:S5_WIDTH + SWA_WIDTH + SWA_KV_WIDTH]
    vn = proj[:, None, S5_WIDTH + SWA_WIDTH + SWA_KV_WIDTH:]
    new_k = jnp.concatenate([kc[:, 1:], kn], axis=1).reshape(k_buf.shape)
    new_v = jnp.concatenate([vc[:, 1:], vn], axis=1).reshape(v_buf.shape)
    return (x, h_t[:, :S5_FLAT].reshape(h_re.shape), h_t[:, S5_FLAT:].reshape(h_im.shape), new_k, new_v)


def _odd_weights(w_in):
    w_z = w_in[:, :SSD_INNER].astype(BF16)
    w_x = w_in[:, SSD_INNER:SSD_INNER + SSD_CONV_DIM].astype(BF16)
    w_dt = jnp.pad(w_in[:, SSD_INNER + SSD_CONV_DIM:], ((0, 0), (0, LANES - SSD_HEADS))).astype(BF16)
    return w_z, w_x, w_dt


def _odd_prompt(x, bsz, seq, w3, ssdp, w_out, g, b):
    x, st, tail = _ssd_prompt(x.reshape(bsz, seq, -1), w3, ssdp, w_out, g, b)
    return (x.reshape(bsz * seq, -1), st.reshape(bsz, SSD_HEADS, SSD_HEAD_DIM, SSD_STATE),
            tail[:, SUBLANES - (SSD_CONV - 1):])


def _odd_sample(x, ssm_all, layer, conv, w3, ssdp, w_out, g, b):
    n = x.shape[0]
    w_z, w_x, w_dt = w3
    z = _mm(x, w_z)
    xbc = _mm(x, w_x)
    dtraw = _mm(x, w_dt)
    s0_all = ssm_all.reshape(ssm_all.shape[0], n, SSD_INNER, SSD_STATE)
    y, st = _ssd_step(z, xbc, dtraw, conv, s0_all, layer, ssdp)
    x = _mm_ln(y, w_out, x, g, b)
    new_conv = jnp.concatenate([conv[:, 1:], xbc[:, None, :]], axis=1)
    return x, st.reshape(ssm_all.shape[1:]), new_conv


def kernel(x_prompt, x_sample, state_s5_re, state_s5_im, cache_swa_k, cache_swa_v, state_ssd, state_conv,
           w_in_even, s5_lambda_re, s5_lambda_im, s5_log_dt, s5_b_re, s5_b_im, s5_c_re, s5_c_im, s5_d,
           s5_w_glu, swa_sinks, w_out_even,
           w_in_odd, ssd_conv_w, ssd_conv_b, ssd_dt_bias, ssd_a_log, ssd_d, ssd_norm_w, w_out_odd,
           ln1_g, ln1_b, ln2_g, ln2_b, peer_w_q, peer_k1, peer_k2, peer_u, peer_v):
    bsz, seq, _ = x_prompt.shape
    nsm = x_sample.shape[0]
    xp = x_prompt.reshape(bsz * seq, D_MODEL)
    xs = x_sample.reshape(nsm, D_MODEL)
    u_all, vt_all = _peer_tables(peer_u, peer_v)
    p_out = [[] for _ in range(6)]
    s_out = [[] for _ in range(6)]
    for layer in range(DEPTH):
        i = layer // 2
        if layer % 2 == 0:
            w_in = w_in_even[i].astype(BF16)
            w_out = w_out_even[i].astype(BF16)
            s5p = _s5_params(s5_lambda_re[i], s5_lambda_im[i], s5_log_dt[i], s5_b_re[i], s5_b_im[i],
                             s5_c_re[i], s5_c_im[i], s5_d[i], s5_w_glu[i])
            xp, hr, hi, nk, nv = _even_prompt(xp, bsz, seq, w_in, s5p, swa_sinks[i], w_out,
                                              ln1_g[layer], ln1_b[layer])
            for lst, val in zip(p_out[:4], (hr, hi, nk, nv)):
                lst.append(val)
            xs, hr, hi, nk, nv = _even_sample(xs, state_s5_re[i], state_s5_im[i], cache_swa_k[i], cache_swa_v[i],
                                              PAST_LEN, w_in, s5p, swa_sinks[i], w_out,
                                              ln1_g[layer], ln1_b[layer])
            for lst, val in zip(s_out[:4], (hr, hi, nk, nv)):
                lst.append(val)
        else:
            w3 = _odd_weights(w_in_odd[i])
            w_out = w_out_odd[i].astype(BF16)
            ssdp = _ssd_params(ssd_conv_w[i], ssd_conv_b[i], ssd_dt_bias[i], ssd_a_log[i], ssd_d[i],
                               ssd_norm_w[i])
            xp, st, cb = _odd_prompt(xp, bsz, seq, w3, ssdp, w_out, ln1_g[layer], ln1_b[layer])
            p_out[4].append(st)
            p_out[5].append(cb)
            xs, st, cb = _odd_sample(xs, state_ssd, i, state_conv[i], w3, ssdp, w_out,
                                     ln1_g[layer], ln1_b[layer])
            s_out[4].append(st)
            s_out[5].append(cb)
        wq = peer_w_q[layer].astype(BF16)
        k1 = peer_k1[layer].astype(BF16)
        k2 = peer_k2[layer].astype(BF16)
        xp = _peer_ln(xp, wq, k1, k2, u_all, vt_all, layer, ln2_g[layer], ln2_b[layer])
        xs = _peer_ln(xs, wq, k1, k2, u_all, vt_all, layer, ln2_g[layer], ln2_b[layer])
    stack = lambda lists: tuple(jnp.stack(l) for l in lists)
    return ((xp.reshape(bsz, seq, D_MODEL), xs.reshape(nsm, 1, D_MODEL)) + stack(p_out) + stack(s_out))
```

```python
import functools
import math

import jax
import jax.numpy as jnp
from jax import lax
from jax.experimental import pallas as pl
from jax.experimental.pallas import tpu as pltpu

F32 = jnp.float32
BF16 = jnp.bfloat16

D_MODEL = 1024
DEPTH = 4
PAST_LEN = 8192
S5_WIDTH = 512
S5_GROUP = 16
S5_GROUPS = 32
S5_STATE = 64
S5_FLAT = S5_GROUPS * S5_STATE
SWA_HEADS = 8
SWA_KV_HEADS = 2
SWA_REP = SWA_HEADS // SWA_KV_HEADS
SWA_HEAD_DIM = 64
SWA_WIDTH = SWA_HEADS * SWA_HEAD_DIM
SWA_KV_WIDTH = SWA_KV_HEADS * SWA_HEAD_DIM
D_IN_EVEN = S5_WIDTH + SWA_WIDTH + 2 * SWA_KV_WIDTH
WINDOW = 128
SSD_INNER = 2048
SSD_HEAD_DIM = 64
SSD_HEADS = 32
SSD_GROUPS = 4
SSD_HPG = SSD_HEADS // SSD_GROUPS
SSD_GW = SSD_HPG * SSD_HEAD_DIM
SSD_STATE = 128
SSD_CONV = 4
SSD_CHUNK = 128
SSD_CONV_DIM = SSD_INNER + 2 * SSD_GROUPS * SSD_STATE
PEER_HEADS = 8
PEER_KEYS = 128
PEER_EXPERTS = PEER_KEYS * PEER_KEYS
PEER_DK = 256
PEER_TOPK = 16
DN_ALPHA = (2 * DEPTH) ** 0.25
LN_EPS = 1e-5
RMS_EPS = 1e-5
NEG_INF = -1e30

LANES = 128
SUBLANES = 8
MXU_TILE = 256
VMEM_LIMIT = 56 * 1024 * 1024


def _cp(*sem):
    return pltpu.CompilerParams(dimension_semantics=sem, vmem_limit_bytes=VMEM_LIMIT)


def _ln(x, g, b):
    mu = jnp.mean(x, axis=-1, keepdims=True)
    xc = x - mu
    var = jnp.mean(xc * xc, axis=-1, keepdims=True)
    return xc * lax.rsqrt(var + LN_EPS) * g + b


def _softplus(x):
    return jnp.maximum(x, 0.0) + jnp.log(1.0 + jnp.exp(-jnp.abs(x)))


def _silu(x):
    return x * jax.nn.sigmoid(x)


def _dot(a, b):
    return jnp.dot(a, b, preferred_element_type=F32)


def _dot_nt(a, b):
    return lax.dot_general(a, b, (((1,), (1,)), ((), ())), preferred_element_type=F32)


def _dot_tn(a, b):
    return lax.dot_general(a, b, (((0,), (0,)), ((), ())), preferred_element_type=F32)


def _split3(x):
    x1 = x.astype(BF16)
    r1 = x - x1.astype(F32)
    x2 = r1.astype(BF16)
    r2 = r1 - x2.astype(F32)
    return x1, x2, r2.astype(BF16)


def _dot_exact01(x, m01):
    x1, x2, x3 = _split3(x)
    return _dot(x1, m01) + _dot(x2, m01) + _dot(x3, m01)


def _mm_kernel(x_ref, w_ref, o_ref):
    o_ref[...] = _dot(x_ref[...].astype(BF16), w_ref[...]).astype(o_ref.dtype)


def _mm(x, w, tm=512, tn=512, out_dtype=F32):
    m, k = x.shape
    n = w.shape[1]
    tm = min(m, tm)
    tn = min(n, tn)
    return pl.pallas_call(
        _mm_kernel,
        grid=(m // tm, n // tn),
        in_specs=[pl.BlockSpec((tm, k), lambda i, j: (i, 0)),
                  pl.BlockSpec((k, tn), lambda i, j: (0, j))],
        out_specs=pl.BlockSpec((tm, tn), lambda i, j: (i, j)),
        out_shape=jax.ShapeDtypeStruct((m, n), out_dtype),
        compiler_params=_cp("parallel", "arbitrary"),
        name="mm",
    )(x, w)


def _mm2_ln_kernel(xa_ref, xb_ref, w_ref, r_ref, g_ref, b_ref, o_ref):
    ka = xa_ref.shape[1]
    y = _dot(xa_ref[...], w_ref[0:ka, :]) + _dot(xb_ref[...], w_ref[ka:, :])
    o_ref[...] = _ln(DN_ALPHA * r_ref[...] + y, g_ref[...], b_ref[...])


def _mm2_ln(xa, xb, w, res, g, b):
    m, ka = xa.shape
    kb = xb.shape[1]
    n = w.shape[1]
    tm = min(m, 512)
    return pl.pallas_call(
        _mm2_ln_kernel,
        grid=(m // tm,),
        in_specs=[pl.BlockSpec((tm, ka), lambda i: (i, 0)),
                  pl.BlockSpec((tm, kb), lambda i: (i, 0)),
                  pl.BlockSpec((ka + kb, n), lambda i: (0, 0)),
                  pl.BlockSpec((tm, n), lambda i: (i, 0)),
                  pl.BlockSpec((1, n), lambda i: (0, 0)),
                  pl.BlockSpec((1, n), lambda i: (0, 0))],
        out_specs=pl.BlockSpec((tm, n), lambda i: (i, 0)),
        out_shape=jax.ShapeDtypeStruct((m, n), F32),
        compiler_params=_cp("parallel"),
        name="mm2_ln",
    )(xa, xb, w, res, g.reshape(1, n), b.reshape(1, n))


def _mm_ln_kernel(x_ref, w_ref, r_ref, g_ref, b_ref, o_ref):
    y = _dot(x_ref[...].astype(BF16), w_ref[...])
    o_ref[...] = _ln(DN_ALPHA * r_ref[...] + y, g_ref[...], b_ref[...])


def _mm_ln(x, w, res, g, b):
    m, k = x.shape
    n = w.shape[1]
    tm = min(m, 512)
    return pl.pallas_call(
        _mm_ln_kernel,
        grid=(m // tm,),
        in_specs=[pl.BlockSpec((tm, k), lambda i: (i, 0)),
                  pl.BlockSpec((k, n), lambda i: (0, 0)),
                  pl.BlockSpec((tm, n), lambda i: (i, 0)),
                  pl.BlockSpec((1, n), lambda i: (0, 0)),
                  pl.BlockSpec((1, n), lambda i: (0, 0))],
        out_specs=pl.BlockSpec((tm, n), lambda i: (i, 0)),
        out_shape=jax.ShapeDtypeStruct((m, n), F32),
        compiler_params=_cp("parallel"),
        name="mm_ln",
    )(x, w, res, g.reshape(1, n), b.reshape(1, n))


def _s5_kernel(u_ref, h0_ref, bd_ref, ar_ref, ai_ref, cbd_ref, d_ref, wg_ref,
               y_ref, hT_ref, hbuf, state, *tm_bufs, rows, steps):
    c = pl.program_id(0)

    @pl.when(c == 0)
    def _():
        state[...] = h0_ref[...]

    if tm_bufs:
        ubm, utm, ytm = tm_bufs
        nlb = S5_WIDTH // LANES
        for s in range(rows):
            us = u_ref[s]
            for kb in range(nlb):
                ubm[kb, s * steps:(s + 1) * steps, :] = us[:, kb * LANES:(kb + 1) * LANES]
        for t in range(steps):
            for kb in range(nlb):
                utm[t * rows:(t + 1) * rows, kb * LANES:(kb + 1) * LANES] = ubm[kb, pl.ds(t, rows, stride=steps), :]
        u = utm[...]
    else:
        u = u_ref[...]
    ub = u.astype(BF16)
    for part in range(2):
        for jt in range(S5_FLAT // MXU_TILE):
            c0 = part * S5_FLAT + jt * MXU_TILE
            k0 = (jt * MXU_TILE // S5_STATE * S5_GROUP) // LANES * LANES
            hbuf[:, c0:c0 + MXU_TILE] = _dot(ub[:, k0:k0 + LANES], bd_ref[k0:k0 + LANES, c0:c0 + MXU_TILE])
    ar = ar_ref[...]
    ai = ai_ref[...]

    def step(t, carry):
        hr, hi = carry
        r0 = pl.multiple_of(t * rows, rows)
        br = hbuf[pl.ds(r0, rows), 0:S5_FLAT]
        bi = hbuf[pl.ds(r0, rows), S5_FLAT:2 * S5_FLAT]
        nr = ar * hr - ai * hi + br
        ni = ar * hi + ai * hr + bi
        hbuf[pl.ds(r0, rows), 0:S5_FLAT] = nr
        hbuf[pl.ds(r0, rows), S5_FLAT:2 * S5_FLAT] = ni
        return nr, ni

    hr, hi = lax.fori_loop(0, steps, step, (state[:, 0:S5_FLAT], state[:, S5_FLAT:2 * S5_FLAT]))
    state[:, 0:S5_FLAT] = hr
    state[:, S5_FLAT:2 * S5_FLAT] = hi

    ys = []
    for nt in range(S5_WIDTH // MXU_TILE):
        n0 = nt * MXU_TILE
        kw = MXU_TILE // S5_GROUP * S5_STATE
        acc = None
        for part in range(2):
            k0 = part * S5_FLAT + nt * kw
            term = _dot(hbuf[:, k0:k0 + kw].astype(BF16), cbd_ref[k0:k0 + kw, n0:n0 + MXU_TILE])
            acc = term if acc is None else acc + term
        ys.append(acc)
    y = jnp.concatenate(ys, axis=-1) + d_ref[...] * u
    g = jax.nn.gelu(y)
    out = g * jax.nn.sigmoid(_dot(g.astype(BF16), wg_ref[...]))
    if tm_bufs:
        for kb in range(nlb):
            ytm[kb] = out[:, kb * LANES:(kb + 1) * LANES]
        for s in range(rows):
            y_ref[s] = jnp.concatenate([ytm[kb, pl.ds(s, steps, stride=rows), :] for kb in range(nlb)],
                                       axis=-1).astype(y_ref.dtype)
    else:
        y_ref[...] = out.astype(y_ref.dtype)

    @pl.when(c == pl.num_programs(0) - 1)
    def _():
        hT_ref[...] = state[...]


def _s5(u_tm, h0, prm, rows, steps):
    n = u_tm.shape[0]
    blk = rows * steps
    bd, ar, ai, cbd, d, wg = prm
    const = lambda shape: pl.BlockSpec(shape, lambda c: (0, 0))
    return pl.pallas_call(
        functools.partial(_s5_kernel, rows=rows, steps=steps),
        grid=(n // blk,),
        in_specs=[pl.BlockSpec((blk, S5_WIDTH), lambda c: (c, 0)),
                  const((rows, 2 * S5_FLAT)),
                  const((S5_WIDTH, 2 * S5_FLAT)),
                  const((1, S5_FLAT)), const((1, S5_FLAT)),
                  const((2 * S5_FLAT, S5_WIDTH)),
                  const((1, S5_WIDTH)),
                  const((S5_WIDTH, S5_WIDTH))],
        out_specs=[pl.BlockSpec((blk, S5_WIDTH), lambda c: (c, 0)),
                   const((rows, 2 * S5_FLAT))],
        out_shape=[jax.ShapeDtypeStruct((n, S5_WIDTH), BF16),
                   jax.ShapeDtypeStruct((rows, 2 * S5_FLAT), F32)],
        scratch_shapes=[pltpu.VMEM((blk, 2 * S5_FLAT), F32),
                        pltpu.VMEM((rows, 2 * S5_FLAT), F32)],
        compiler_params=_cp("arbitrary"),
        name="s5",
    )(u_tm, h0, bd, ar, ai, cbd, d, wg)


def _s5_prompt(proj, h0, prm, steps):
    bsz, seq, _ = proj.shape
    blk = bsz * steps
    bd, ar, ai, cbd, d, wg = prm
    const = lambda shape: pl.BlockSpec(shape, lambda c: (0, 0))
    seqblk = pl.BlockSpec((bsz, steps, S5_WIDTH), lambda c: (0, c, 0))
    return pl.pallas_call(
        functools.partial(_s5_kernel, rows=bsz, steps=steps),
        grid=(seq // steps,),
        in_specs=[seqblk,
                  const((bsz, 2 * S5_FLAT)),
                  const((S5_WIDTH, 2 * S5_FLAT)),
                  const((1, S5_FLAT)), const((1, S5_FLAT)),
                  const((2 * S5_FLAT, S5_WIDTH)),
                  const((1, S5_WIDTH)),
                  const((S5_WIDTH, S5_WIDTH))],
        out_specs=[seqblk, const((bsz, 2 * S5_FLAT))],
        out_shape=[jax.ShapeDtypeStruct((bsz, seq, S5_WIDTH), BF16),
                   jax.ShapeDtypeStruct((bsz, 2 * S5_FLAT), F32)],
        scratch_shapes=[pltpu.VMEM((blk, 2 * S5_FLAT), F32),
                        pltpu.VMEM((bsz, 2 * S5_FLAT), F32),
                        pltpu.VMEM((S5_WIDTH // LANES, blk, LANES), F32),
                        pltpu.VMEM((blk, S5_WIDTH), F32),
                        pltpu.VMEM((S5_WIDTH // LANES, blk, LANES), F32)],
        compiler_params=_cp("arbitrary"),
        name="s5_prompt",
    )(proj, h0, bd, ar, ai, cbd, d, wg)


def _s5_params(lam_re, lam_im, log_dt, b_re, b_im, c_re, c_im, d_skip, w_glu):
    dt = jnp.exp(log_dt)[:, None]
    mag = jnp.exp(lam_re * dt)
    ab_re, ab_im = mag * jnp.cos(lam_im * dt), mag * jnp.sin(lam_im * dt)
    den = lam_re * lam_re + lam_im * lam_im
    nr, ni = ab_re - 1.0, ab_im
    coef_re = ((nr * lam_re + ni * lam_im) / den)[..., None]
    coef_im = ((ni * lam_re - nr * lam_im) / den)[..., None]
    bb_re = coef_re * b_re - coef_im * b_im
    bb_im = coef_re * b_im + coef_im * b_re
    eye = jnp.eye(S5_GROUPS, dtype=F32)
    pack_in = lambda bb: jnp.einsum('gpc,gh->gchp', bb, eye).reshape(S5_WIDTH, S5_FLAT)
    bd = jnp.concatenate([pack_in(bb_re), pack_in(bb_im)], axis=1).astype(BF16)
    pack_out = lambda cc: jnp.einsum('gcp,gh->gphc', cc, eye).reshape(S5_FLAT, S5_WIDTH)
    cbd = jnp.concatenate([pack_out(c_re), -pack_out(c_im)], axis=0).astype(BF16)
    return (bd, ab_re.reshape(1, S5_FLAT), ab_im.reshape(1, S5_FLAT), cbd,
            d_skip.reshape(1, S5_WIDTH), w_glu.astype(BF16))


def _swa_kernel(sink_ref, q_ref, kp_ref, kc_ref, vp_ref, vc_ref, o_ref):
    n = pl.program_id(1)
    q = q_ref[0].astype(BF16)
    k = jnp.concatenate([kp_ref[0], kc_ref[0]], axis=0).astype(BF16)
    v = jnp.concatenate([vp_ref[0], vc_ref[0]], axis=0).astype(BF16)
    qi = lax.broadcasted_iota(jnp.int32, (WINDOW, 2 * WINDOW), 0)
    kj = lax.broadcasted_iota(jnp.int32, (WINDOW, 2 * WINDOW), 1)
    dist = WINDOW + qi - kj
    valid = (dist >= 0) & (dist < WINDOW) & ((n > 0) | (kj >= WINDOW))
    distf = dist.astype(F32)
    outs = []
    for h in range(SWA_HEADS):
        g = h // SWA_REP
        qh = q[:, h * SWA_HEAD_DIM:(h + 1) * SWA_HEAD_DIM]
        kg = k[:, g * SWA_HEAD_DIM:(g + 1) * SWA_HEAD_DIM]
        vg = v[:, g * SWA_HEAD_DIM:(g + 1) * SWA_HEAD_DIM]
        s = _dot_nt(qh, kg) * (SWA_HEAD_DIM ** -0.5)
        s = s - (2.0 ** -(h + 1)) * distf
        s = jnp.where(valid, s, NEG_INF)
        sink = sink_ref[h]
        m = jnp.maximum(jnp.max(s, axis=-1, keepdims=True), sink)
        p = jnp.exp(s - m)
        p = p / (jnp.sum(p, axis=-1, keepdims=True) + jnp.exp(sink - m))
        outs.append(_dot(p.astype(BF16), vg))
    o_ref[0] = jnp.concatenate(outs, axis=-1).astype(o_ref.dtype)


def _swa_prompt(proj, sinks):
    bsz, seq, _ = proj.shape
    kcol = (S5_WIDTH + SWA_WIDTH) // SWA_KV_WIDTH
    prev = lambda b, n: (b, jnp.maximum(n - 1, 0), kcol)
    prev_v = lambda b, n: (b, jnp.maximum(n - 1, 0), kcol + 1)
    blk = (1, WINDOW, SWA_KV_WIDTH)
    return pl.pallas_call(
        _swa_kernel,
        grid=(bsz, seq // WINDOW),
        in_specs=[pl.BlockSpec(memory_space=pltpu.SMEM),
                  pl.BlockSpec((1, WINDOW, SWA_WIDTH), lambda b, n: (b, n, 1)),
                  pl.BlockSpec(blk, prev),
                  pl.BlockSpec(blk, lambda b, n: (b, n, kcol)),
                  pl.BlockSpec(blk, prev_v),
                  pl.BlockSpec(blk, lambda b, n: (b, n, kcol + 1))],
        out_specs=pl.BlockSpec((1, WINDOW, SWA_WIDTH), lambda b, n: (b, n, 0)),
        out_shape=jax.ShapeDtypeStruct((bsz, seq, SWA_WIDTH), BF16),
        compiler_params=_cp("parallel", "arbitrary"),
        name="swa_prompt",
    )(sinks, proj, proj, proj, proj, proj)


def _per_head_column(values):
    r = lax.broadcasted_iota(jnp.int32, (SWA_REP, 1), 0)
    col = jnp.full((SWA_REP, 1), values[-1], F32)
    for i in range(SWA_REP - 2, -1, -1):
        col = jnp.where(r == i, values[i], col)
    return col


def _swa_step_kernel(sink_ref, p_ref, kc_ref, vc_ref, o_ref, *, nseq, start):
    kj = lax.broadcasted_iota(jnp.int32, (SWA_REP, WINDOW), 1)
    dist = WINDOW - kj
    valid = (dist < WINDOW) & (start - WINDOW + kj >= 0)
    distf = dist.astype(F32)
    for i in range(nseq):
        row = p_ref[i]
        kn = row[:, S5_WIDTH + SWA_WIDTH:S5_WIDTH + SWA_WIDTH + SWA_KV_WIDTH].astype(BF16)
        vn = row[:, S5_WIDTH + SWA_WIDTH + SWA_KV_WIDTH:].astype(BF16)
        kc = kc_ref[i].astype(BF16)
        vc = vc_ref[i].astype(BF16)
        outs = []
        for g in range(SWA_KV_HEADS):
            q4 = jnp.concatenate(
                [row[:, S5_WIDTH + (g * SWA_REP + r) * SWA_HEAD_DIM:S5_WIDTH + (g * SWA_REP + r + 1) * SWA_HEAD_DIM]
                 for r in range(SWA_REP)], axis=0).astype(BF16)
            sl = slice(g * SWA_HEAD_DIM, (g + 1) * SWA_HEAD_DIM)
            slope = _per_head_column([2.0 ** -(g * SWA_REP + r + 1) for r in range(SWA_REP)])
            sink = _per_head_column([sink_ref[g * SWA_REP + r] for r in range(SWA_REP)])
            scale = SWA_HEAD_DIM ** -0.5
            sc = _dot_nt(q4, kc[:, sl]) * scale - slope * distf
            sc = jnp.where(valid, sc, NEG_INF)
            sn = jnp.sum(q4.astype(F32) * kn[:, sl].astype(F32), axis=-1, keepdims=True) * scale
            m = jnp.maximum(jnp.maximum(jnp.max(sc, axis=-1, keepdims=True), sn), sink)
            pc = jnp.exp(sc - m)
            pn = jnp.exp(sn - m)
            den = jnp.sum(pc, axis=-1, keepdims=True) + pn + jnp.exp(sink - m)
            o4 = _dot((pc / den).astype(BF16), vc[:, sl])
            o4 = o4 + (pn / den).astype(BF16).astype(F32) * vn[:, sl].astype(F32)
            outs.extend([o4[r:r + 1, :] for r in range(SWA_REP)])
        o_ref[i] = jnp.concatenate(outs, axis=-1).astype(o_ref.dtype)


def _swa_step(proj, k_cache, v_cache, sinks, start):
    n = proj.shape[0]
    nseq = min(n, SUBLANES)
    out = pl.pallas_call(
        functools.partial(_swa_step_kernel, nseq=nseq, start=start),
        grid=(n // nseq,),
        in_specs=[pl.BlockSpec(memory_space=pltpu.SMEM),
                  pl.BlockSpec((nseq, 1, proj.shape[1]), lambda i: (i, 0, 0)),
                  pl.BlockSpec((nseq, WINDOW, SWA_KV_WIDTH), lambda i: (i, 0, 0)),
                  pl.BlockSpec((nseq, WINDOW, SWA_KV_WIDTH), lambda i: (i, 0, 0))],
        out_specs=pl.BlockSpec((nseq, 1, SWA_WIDTH), lambda i: (i, 0, 0)),
        out_shape=jax.ShapeDtypeStruct((n, 1, SWA_WIDTH), BF16),
        compiler_params=_cp("parallel"),
        name="swa_step",
    )(sinks, proj.reshape(n, 1, -1), k_cache, v_cache)
    return out.reshape(n, SWA_WIDTH)


def _head_expand(width):
    r = lax.broadcasted_iota(jnp.int32, (LANES, SSD_HEADS * width), 0)
    c = lax.broadcasted_iota(jnp.int32, (LANES, SSD_HEADS * width), 1)
    return jnp.where((c >= r * width) & (c < (r + 1) * width), 1.0, 0.0).astype(BF16)


def _ssd_kernel(x_ref, wz_ref, wx_ref, wdt_ref, cw_ref, cb_ref, dtb_ref, a_ref, d_ref, nw_ref,
                wo_ref, g_ref, b_ref, o_ref, st_ref, conv_ref, tail, state, ybuf):
    c = pl.program_id(1)
    q = SSD_CHUNK

    @pl.when(c == 0)
    def _():
        tail[...] = jnp.zeros_like(tail)
        state[...] = jnp.zeros_like(state)

    xin = x_ref[0]
    xb = xin.astype(BF16)
    cur = _dot(xb, wx_ref[...])
    tl = tail[...]
    row8 = lax.broadcasted_iota(jnp.int32, (SUBLANES, SSD_CONV_DIM), 0)
    acc = None
    for tap in range(SSD_CONV):
        sh = SSD_CONV - 1 - tap
        if sh == 0:
            term = cur
        else:
            rolled = pltpu.roll(cur, sh, 0)
            head = jnp.where(row8 < sh, pltpu.roll(tl, sh, 0), rolled[0:SUBLANES])
            term = jnp.concatenate([head, rolled[SUBLANES:]], axis=0)
        term = term * cw_ref[tap:tap + 1, :]
        acc = term if acc is None else acc + term
    tail[...] = cur[q - SUBLANES:q]
    xbc = _silu(acc + cb_ref[...])
    xs = xbc[:, 0:SSD_INNER]
    bm = xbc[:, SSD_INNER:SSD_INNER + SSD_GROUPS * SSD_STATE].astype(BF16)
    cm = xbc[:, SSD_INNER + SSD_GROUPS * SSD_STATE:].astype(BF16)

    dt = _softplus(_dot(xb, wdt_ref[...]) + dtb_ref[...])
    da = dt * a_ref[...]
    ri = lax.broadcasted_iota(jnp.int32, (q, q), 0)
    ci = lax.broadcasted_iota(jnp.int32, (q, q), 1)
    causal = ri >= ci
    tri = jnp.where(causal, 1.0, 0.0).astype(BF16)
    da1, da2, da3 = _split3(da)
    acs = _dot(tri, da1) + _dot(tri, da2) + _dot(tri, da3)
    acs_t = acs.T
    e64 = _head_expand(SSD_HEAD_DIM)
    dt_x = _dot_exact01(dt, e64)
    acs_x = _dot_exact01(acs, e64)
    last_x = acs_x[q - 1:q, :]
    dx = dt_x * xs
    dsx = (jnp.exp(last_x - acs_x) * dx).astype(BF16)
    eacs_x = jnp.exp(acs_x)
    dxb = dx.astype(BF16)

    for g in range(SSD_GROUPS):
        ns = slice(g * SSD_STATE, (g + 1) * SSD_STATE)
        gs = slice(g * SSD_GW, (g + 1) * SSD_GW)
        cb = _dot_nt(cm[:, ns], bm[:, ns])
        for r in range(0, SSD_HPG, 2):
            lmats = []
            for h in (g * SSD_HPG + r, g * SSD_HPG + r + 1):
                seg = acs[:, h:h + 1] - acs_t[h:h + 1, :]
                decay = jnp.where(causal, jnp.exp(jnp.where(causal, seg, 0.0)), 0.0)
                lmats.append((cb * decay).astype(BF16))
            ps = slice((g * SSD_HPG + r) * SSD_HEAD_DIM, (g * SSD_HPG + r + 2) * SSD_HEAD_DIM)
            pair = dxb[:, ps]
            first = lax.broadcasted_iota(jnp.int32, pair.shape, 1) < SSD_HEAD_DIM
            zero = jnp.zeros_like(pair)
            rhs = jnp.concatenate([jnp.where(first, pair, zero), jnp.where(first, zero, pair)], axis=0)
            ybuf[:, ps] = _dot(jnp.concatenate(lmats, axis=1), rhs)
        s_in = state[g]
        y_off = _dot(cm[:, ns], s_in.astype(BF16)) * eacs_x[:, gs]
        ybuf[:, gs] = ybuf[:, gs] + y_off
        state[g] = jnp.exp(last_x[:, gs]) * s_in + _dot_tn(bm[:, ns], dsx[:, gs])

    y = ybuf[...] + d_ref[...] * xs
    y = y * _silu(_dot(xb, wz_ref[...]))
    y = y * lax.rsqrt(jnp.mean(y * y, axis=-1, keepdims=True) + RMS_EPS) * nw_ref[...]
    mix = _dot(y.astype(BF16), wo_ref[...])
    o_ref[0] = _ln(DN_ALPHA * xin + mix, g_ref[...], b_ref[...])

    @pl.when(c == pl.num_programs(1) - 1)
    def _():
        conv_ref[0] = cur[q - SUBLANES:q]
        for g in range(SSD_GROUPS):
            st_ref[0, g * SSD_GW:(g + 1) * SSD_GW, :] = state[g].T


def _ssd_prompt(x, w3, prm, w_out, g, b):
    bsz, seq, _ = x.shape
    w_z, w_x, w_dt = w3
    cw, cb, dtb, a, d_x, nw = prm
    const = lambda shape: pl.BlockSpec(shape, lambda b, c: (0, 0))
    seqblk = lambda w: pl.BlockSpec((1, SSD_CHUNK, w), lambda b, c: (b, c, 0))
    return pl.pallas_call(
        _ssd_kernel,
        grid=(bsz, seq // SSD_CHUNK),
        in_specs=[seqblk(D_MODEL),
                  const((D_MODEL, SSD_INNER)), const((D_MODEL, SSD_CONV_DIM)), const((D_MODEL, LANES)),
                  const((SSD_CONV, SSD_CONV_DIM)), const((1, SSD_CONV_DIM)),
                  const((1, LANES)), const((1, LANES)),
                  const((1, SSD_INNER)), const((1, SSD_INNER)),
                  const((SSD_INNER, D_MODEL)), const((1, D_MODEL)), const((1, D_MODEL))],
        out_specs=[seqblk(D_MODEL),
                   pl.BlockSpec((1, SSD_INNER, SSD_STATE), lambda b, c: (b, 0, 0)),
                   pl.BlockSpec((1, SUBLANES, SSD_CONV_DIM), lambda b, c: (b, 0, 0))],
        out_shape=[jax.ShapeDtypeStruct((bsz, seq, D_MODEL), F32),
                   jax.ShapeDtypeStruct((bsz, SSD_INNER, SSD_STATE), F32),
                   jax.ShapeDtypeStruct((bsz, SUBLANES, SSD_CONV_DIM), F32)],
        scratch_shapes=[pltpu.VMEM((SUBLANES, SSD_CONV_DIM), F32),
                        pltpu.VMEM((SSD_GROUPS, SSD_STATE, SSD_GW), F32),
                        pltpu.VMEM((SSD_CHUNK, SSD_INNER), F32)],
        compiler_params=_cp("parallel", "arbitrary"),
        name="ssd_prompt",
    )(x, w_z, w_x, w_dt, cw, cb, dtb, a, d_x, nw, w_out, g.reshape(1, -1), b.reshape(1, -1))


def _pad_rows(row):
    return jnp.concatenate([row, jnp.zeros((SUBLANES - 1, row.shape[1]), row.dtype)], axis=0)


def _ssd_step_kernel(z_ref, x_ref, dt_ref, buf_ref, s0_ref, cw_ref, cb_ref, dtb_ref, a_ref, d_ref, nw_ref,
                     y_ref, s1_ref, *, nseq):
    e64 = _head_expand(SSD_HEAD_DIM)
    ones3 = jnp.where(lax.broadcasted_iota(jnp.int32, (SUBLANES, SSD_STATE), 0) < 3, 1.0, 0.0).astype(BF16)
    for i in range(nseq):
        buf = buf_ref[i]
        acc = buf[0:1] * cw_ref[0:1, :]
        acc = acc + buf[1:2] * cw_ref[1:2, :]
        acc = acc + buf[2:3] * cw_ref[2:3, :]
        acc = acc + x_ref[i] * cw_ref[3:4, :]
        xbc = _silu(acc + cb_ref[...])
        xs = xbc[:, 0:SSD_INNER]
        bm = xbc[:, SSD_INNER:SSD_INNER + SSD_GROUPS * SSD_STATE].astype(BF16)
        cm = xbc[:, SSD_INNER + SSD_GROUPS * SSD_STATE:].astype(BF16)
        dt = _softplus(dt_ref[i] + dtb_ref[...])
        ed = jnp.exp(dt * a_ref[...])
        dt_x = _dot_exact01(_pad_rows(dt), e64)[0:1]
        ed_x = _dot_exact01(_pad_rows(ed), e64)[0:1]
        dx = dt_x * xs
        dxb = dx.astype(BF16)
        ys = []
        for g in range(SSD_GROUPS):
            ns = slice(g * SSD_STATE, (g + 1) * SSD_STATE)
            gs = slice(g * SSD_GW, (g + 1) * SSD_GW)
            s0 = s0_ref[i, gs, :]
            cmg = _pad_rows(cm[:, ns])
            cbv = jnp.sum(cm[:, ns].astype(F32) * bm[:, ns].astype(F32), axis=-1, keepdims=True)
            y_diag = cbv.astype(BF16).astype(F32) * dxb[:, gs].astype(F32)
            y_off = _dot_nt(cmg, s0.astype(BF16))[0:1] * ed_x[:, gs]
            ys.append(y_diag + y_off)
            e1, e2, e3 = _split3(ed_x[:, gs])
            ed_rows = jnp.concatenate([e1, e2, e3, jnp.zeros((SUBLANES - 3, SSD_GW), BF16)], axis=0)
            ed_full = _dot_tn(ed_rows, ones3)
            outer = _dot_tn(_pad_rows(dxb[:, gs]), _pad_rows(bm[:, ns]))
            s1_ref[i, gs, :] = ed_full * s0 + outer
        y = jnp.concatenate(ys, axis=-1) + d_ref[...] * xs
        y = y * _silu(z_ref[i])
        y = y * lax.rsqrt(jnp.mean(y * y, axis=-1, keepdims=True) + RMS_EPS) * nw_ref[...]
        y_ref[i] = y.astype(y_ref.dtype)


def _ssd_step(z, xbc, dtraw, conv_buf, s0_all, layer, prm):
    n = z.shape[0]
    nseq = min(n, 4)
    cw, cb, dtb, a, d_x, nw = prm
    const = lambda shape: pl.BlockSpec(shape, lambda i: (0, 0))
    tok = lambda w: pl.BlockSpec((nseq, 1, w), lambda i: (i, 0, 0))
    y, s1 = pl.pallas_call(
        functools.partial(_ssd_step_kernel, nseq=nseq),
        grid=(n // nseq,),
        in_specs=[tok(SSD_INNER), tok(SSD_CONV_DIM), tok(LANES),
                  pl.BlockSpec((nseq, SSD_CONV - 1, SSD_CONV_DIM), lambda i: (i, 0, 0)),
                  pl.BlockSpec((None, nseq, SSD_INNER, SSD_STATE), lambda i: (layer, i, 0, 0)),
                  const((SSD_CONV, SSD_CONV_DIM)), const((1, SSD_CONV_DIM)),
                  const((1, LANES)), const((1, LANES)),
                  const((1, SSD_INNER)), const((1, SSD_INNER))],
        out_specs=[tok(SSD_INNER),
                   pl.BlockSpec((nseq, SSD_INNER, SSD_STATE), lambda i: (i, 0, 0))],
        out_shape=[jax.ShapeDtypeStruct((n, 1, SSD_INNER), BF16),
                   jax.ShapeDtypeStruct((n, SSD_INNER, SSD_STATE), F32)],
        compiler_params=_cp("parallel"),
        name="ssd_step",
    )(z.reshape(n, 1, -1), xbc.reshape(n, 1, -1), dtraw.reshape(n, 1, -1), conv_buf, s0_all,
      cw, cb, dtb, a, d_x, nw)
    return y.reshape(n, SSD_INNER), s1


def _ssd_params(conv_w, conv_b, dt_bias, a_log, d_skip, norm_w):
    pad = lambda v: jnp.pad(v, (0, LANES - SSD_HEADS)).reshape(1, LANES)
    return (conv_w, conv_b.reshape(1, -1), pad(dt_bias), pad(-jnp.exp(a_log)),
            jnp.repeat(d_skip, SSD_HEAD_DIM).reshape(1, SSD_INNER), norm_w.reshape(1, SSD_INNER))


PEER_TE = 2048
PEER_ROWS = PEER_TE // PEER_KEYS


PEER_CHUNK_ROWS = 4
PEER_TOK_CHUNK = 256
PACK = 2 * SUBLANES


def _batcher_sort_pairs(n):
    def merge(lo, hi, r):
        step = r * 2
        if step < hi - lo:
            yield from merge(lo, hi, step)
            yield from merge(lo + r, hi, step)
            for i in range(lo + r, hi - r, step):
                yield (i, i + r)
        else:
            yield (lo, lo + r)

    def sort(lo, hi):
        if hi - lo >= 1:
            mid = lo + (hi - lo) // 2
            yield from sort(lo, mid)
            yield from sort(mid + 1, hi)
            yield from merge(lo, hi, 1)

    return tuple(sort(0, n - 1))


_SORT16 = _batcher_sort_pairs(PEER_TOPK)
_BITONIC16 = tuple((i, i + d) for d in (8, 4, 2, 1) for i in range(PEER_TOPK) if not i & d)


def _mx(a, b):
    return b if a is None else a if b is None else jnp.maximum(a, b)


def _mn(a, b):
    return None if a is None or b is None else jnp.minimum(a, b)


def _top16(rows):
    v = list(rows) + [None] * (PEER_TOPK - len(rows))
    for i, j in _SORT16:
        v[i], v[j] = _mx(v[i], v[j]), _mn(v[i], v[j])
    for shift in (4, 2, 1):
        w = [None if x is None else pltpu.roll(x, shift, 0) for x in v]
        v = [_mx(v[k], w[PEER_TOPK - 1 - k]) for k in range(PEER_TOPK)]
        for i, j in _BITONIC16:
            v[i], v[j] = _mx(v[i], v[j]), _mn(v[i], v[j])
    return v


def _prefix_count(test, thr):
    m8 = test(thr[7])
    m4 = test(jnp.where(m8, thr[11], thr[3]))
    m2 = test(jnp.where(m8, jnp.where(m4, thr[13], thr[9]), jnp.where(m4, thr[5], thr[1])))
    hi = jnp.where(m4, jnp.where(m2, thr[14], thr[12]), jnp.where(m2, thr[10], thr[8]))
    lo = jnp.where(m4, jnp.where(m2, thr[6], thr[4]), jnp.where(m2, thr[2], thr[0]))
    m1 = test(jnp.where(m8, hi, lo))
    return (jnp.where(m8, 8.0, 0.0) + jnp.where(m4, 4.0, 0.0)) + (jnp.where(m2, 2.0, 0.0) + jnp.where(m1, 1.0, 0.0))


def _peer_route(xb, wq_ref, k1_ref, k2_ref, c_ref, e1_ref, rk_ref, e2_ref):
    tm = xb.shape[0]
    q = _dot(xb, wq_ref[...]).astype(BF16)
    half = PEER_DK // 2
    nblk = PEER_KEYS // SUBLANES
    sub = lax.broadcasted_iota(jnp.int32, (SUBLANES, tm), 0)

    def by_sublane(vals):
        out = vals[SUBLANES - 1]
        for s in range(SUBLANES - 2, -1, -1):
            out = jnp.where(sub == s, vals[s], out)
        return out

    for h in range(PEER_HEADS):
        s1 = _dot_nt(k1_ref[h], q[:, h * PEER_DK:h * PEER_DK + half])
        s2 = _dot_nt(k2_ref[h], q[:, h * PEER_DK + half:(h + 1) * PEER_DK])
        r1 = [s1[k * SUBLANES:(k + 1) * SUBLANES] for k in range(nblk)]
        r2 = [s2[k * SUBLANES:(k + 1) * SUBLANES] for k in range(nblk)]
        a = _top16(r1)
        b = _top16(r2)
        r1 = [x - a[0] for x in r1]
        r2 = [x - b[0] for x in r2]
        a = [x - a[0] for x in a]
        b = [x - b[0] for x in b]
        b_lo, b_hi, a_hi = by_sublane(b[0:8]), by_sublane(b[8:16]), by_sublane(a[8:16])
        cand = [a[0] + b_lo, a[0] + b_hi] + [a[i] + b_lo for i in range(1, 8)] + [a_hi + b[0]]
        top = _top16(cand)
        tau = top[PEER_TOPK - 1]
        zsum = jnp.exp(top[0])
        for k in range(1, PEER_TOPK):
            zsum = zsum + jnp.exp(top[k])
        inv_z = 1.0 / zsum
        for k in range(nblk):
            x1 = r1[k]
            cnt = _prefix_count(lambda t: (x1 + t) >= tau, b[0:15])
            cnt = cnt + jnp.where((x1 + b[15]) >= tau, 1.0, 0.0)
            c_ref[h, k, :, 0:tm] = cnt
            e1_ref[h, k, :, 0:tm] = jnp.exp(x1) * inv_z
        ranks, e2s = [], []
        for k in range(nblk):
            x2 = r2[k]
            rnk = _prefix_count(lambda t: t > x2, b[0:15])
            ranks.append(jnp.where(x2 >= b[15], rnk, float(PEER_TOPK)))
            e2s.append(jnp.exp(x2))
        rk_ref[h, :, 0:tm] = jnp.concatenate(ranks, axis=0).astype(BF16)
        e2_ref[h, :, 0:tm] = jnp.concatenate(e2s, axis=0).astype(BF16)


def _gelu_tanh(x):
    c0 = math.sqrt(2.0 / math.pi)
    return (0.5 * x) * (1.0 + jnp.tanh(x * (c0 + (c0 * 0.044715) * (x * x))))


def _peer_kernel(x_ref, xp_ref, wq_ref, k1_ref, k2_ref, u_ref, vt_ref, g_ref, b_ref, o_ref,
                 acc, c_ref, e1_ref, rk_ref, e2_ref, act_ref):
    i = pl.program_id(0)
    j = pl.program_id(1)
    nblk = pl.num_programs(0) - 1
    tm = x_ref.shape[0]
    xb = x_ref[...].astype(BF16)

    @pl.when((i == 0) & (j == 0))
    def _():
        acc[...] = jnp.zeros_like(acc)
        act_ref[1] = jnp.zeros((PEER_TE, tm), BF16)

    @pl.when((j == 0) & (i < nblk))
    def _():
        _peer_route(xb, wq_ref, k1_ref, k2_ref, c_ref, e1_ref, rk_ref, e2_ref)

    nb2 = PEER_KEYS // PACK
    slot = j % 2
    crows = PEER_CHUNK_ROWS
    nch = PEER_ROWS // crows
    tc = min(tm, PEER_TOK_CHUNK)
    apply_after = {nch // 2 - 1 + k: k * tc for k in range(tm // tc)}

    @pl.when((i < nblk) | (j == 0))
    def _():
        for ch in range(nch):
            e0 = ch * crows * PEER_KEYS
            h_t = _dot_nt(u_ref[e0:e0 + crows * PEER_KEYS, :], xb)
            if ch in apply_after:
                t0 = apply_after[ch]
                acc[:, t0:t0 + tc] += _dot(vt_ref[...], act_ref[1 - slot, :, t0:t0 + tc])
            rows = [ch * crows + ri for ri in range(crows)]
            for tl in range(tm // LANES):
                ts = slice(tl * LANES, (tl + 1) * LANES)
                gate = [[None] * nb2 for _ in range(crows)]
                for h in range(PEER_HEADS):
                    blk = [j * (PEER_ROWS // SUBLANES) + r // SUBLANES for r in rows]
                    sub = [r % SUBLANES for r in rows]
                    cbv = [jnp.broadcast_to(c_ref[h, k, s:s + 1, ts], (PACK, LANES)).astype(BF16)
                           for k, s in zip(blk, sub)]
                    ebv = [jnp.broadcast_to(e1_ref[h, k, s:s + 1, ts], (PACK, LANES)).astype(BF16)
                           for k, s in zip(blk, sub)]
                    for b2 in range(nb2):
                        ks = slice(b2 * PACK, (b2 + 1) * PACK)
                        rk = rk_ref[h, ks, ts]
                        e2 = e2_ref[h, ks, ts]
                        for ri in range(crows):
                            w = jnp.where(rk < cbv[ri], ebv[ri] * e2, 0.0)
                            gate[ri][b2] = w if gate[ri][b2] is None else gate[ri][b2] + w
                for ri in range(crows):
                    for b2 in range(nb2):
                        lo = ri * PEER_KEYS + b2 * PACK
                        hv = _gelu_tanh(h_t[lo:lo + PACK, ts].astype(BF16))
                        act_ref[slot, e0 + lo:e0 + lo + PACK, ts] = hv * gate[ri][b2]

    @pl.when(j == 0)
    def _():
        o_ref[...] = _ln(DN_ALPHA * xp_ref[...] + acc[...].T, g_ref[...], b_ref[...])
        acc[...] = jnp.zeros_like(acc)


def _peer_ln(x, wq, k1, k2, u_all, vt_all, layer, g, b):
    t = x.shape[0]
    tm = min(t, 512)
    nblk = t // tm
    ntile = PEER_EXPERTS // PEER_TE
    const2 = lambda shape: pl.BlockSpec(shape, lambda i, j: (0, 0))
    const3 = lambda shape: pl.BlockSpec(shape, lambda i, j: (0, 0, 0))
    cur = lambda i, j: (jnp.minimum(i, nblk - 1), 0)
    prev = lambda i, j: (jnp.maximum(i - 1, 0), 0)
    return pl.pallas_call(
        _peer_kernel,
        grid=(nblk + 1, ntile),
        in_specs=[pl.BlockSpec((tm, D_MODEL), cur),
                  pl.BlockSpec((tm, D_MODEL), prev),
                  const2((D_MODEL, PEER_HEADS * PEER_DK)),
                  const3((PEER_HEADS, PEER_KEYS, PEER_DK // 2)),
                  const3((PEER_HEADS, PEER_KEYS, PEER_DK // 2)),
                  pl.BlockSpec((None, PEER_TE, D_MODEL), lambda i, j: (layer, jnp.where(i < nblk, j, 0), 0)),
                  pl.BlockSpec((None, D_MODEL, PEER_TE),
                               lambda i, j: (layer, 0, jnp.where(i < nblk, (j + ntile - 1) % ntile, ntile - 1))),
                  const2((1, D_MODEL)), const2((1, D_MODEL))],
        out_specs=pl.BlockSpec((tm, D_MODEL), lambda i, j: (jnp.where(j == 0, prev(i, j)[0], cur(i, j)[0]), 0)),
        out_shape=jax.ShapeDtypeStruct((t, D_MODEL), F32),
        scratch_shapes=[pltpu.VMEM((D_MODEL, tm), F32),
                        pltpu.VMEM((PEER_HEADS, PEER_KEYS // SUBLANES, SUBLANES, tm), F32),
                        pltpu.VMEM((PEER_HEADS, PEER_KEYS // SUBLANES, SUBLANES, tm), F32),
                        pltpu.VMEM((PEER_HEADS, PEER_KEYS, tm), BF16),
                        pltpu.VMEM((PEER_HEADS, PEER_KEYS, tm), BF16),
                        pltpu.VMEM((2, PEER_TE, tm), BF16)],
        compiler_params=_cp("arbitrary", "arbitrary"),
        name="peer",
    )(x, x, wq, k1, k2, u_all, vt_all, g.reshape(1, -1), b.reshape(1, -1))


def _peer_tables_kernel(u_ref, v_ref, ub_ref, vt_ref):
    ub_ref[...] = u_ref[...].astype(BF16)
    vt_ref[...] = v_ref[...].T.astype(BF16)


def _peer_tables(peer_u, peer_v):
    nl, ne, nd = peer_u.shape
    tile = pl.BlockSpec((None, D_MODEL, nd), lambda l, e: (l, e, 0))
    return pl.pallas_call(
        _peer_tables_kernel,
        grid=(nl, ne // D_MODEL),
        in_specs=[tile, tile],
        out_specs=[tile, pl.BlockSpec((None, nd, D_MODEL), lambda l, e: (l, 0, e))],
        out_shape=[jax.ShapeDtypeStruct((nl, ne, nd), BF16),
                   jax.ShapeDtypeStruct((nl, nd, ne), BF16)],
        compiler_params=_cp("parallel", "parallel"),
        name="peer_tables",
    )(peer_u, peer_v)


def _even_prompt(x, bsz, seq, w_in, s5p, sinks, w_out, g, b):
    proj = _mm(x, w_in, tm=1024, tn=D_IN_EVEN)
    p3 = proj.reshape(bsz, seq, -1)
    y_a, h_t = _s5_prompt(p3, jnp.zeros((bsz, 2 * S5_FLAT), F32), s5p, min(seq, 64))
    y_a = y_a.reshape(bsz * seq, S5_WIDTH)
    y_b = _swa_prompt(p3, sinks).reshape(bsz * seq, SWA_WIDTH)
    x = _mm2_ln(y_a, y_b, w_out, x, g, b)
    kv = p3[:, seq - WINDOW:, S5_WIDTH + SWA_WIDTH:]
    new_k = kv[..., :SWA_KV_WIDTH].reshape(bsz, WINDOW, SWA_KV_HEADS, SWA_HEAD_DIM)
    new_v = kv[..., SWA_KV_WIDTH:].reshape(bsz, WINDOW, SWA_KV_HEADS, SWA_HEAD_DIM)
    h_re = h_t[:, :S5_FLAT].reshape(bsz, S5_GROUPS, S5_STATE)
    h_im = h_t[:, S5_FLAT:].reshape(bsz, S5_GROUPS, S5_STATE)
    return x, h_re, h_im, new_k, new_v


def _even_sample(x, h_re, h_im, k_buf, v_buf, start, w_in, s5p, sinks, w_out, g, b):
    n = x.shape[0]
    proj = _mm(x, w_in, tm=1024, tn=D_IN_EVEN)
    h0 = jnp.concatenate([h_re.reshape(n, S5_FLAT), h_im.reshape(n, S5_FLAT)], axis=1)
    y_a, h_t = _s5(proj[:, :S5_WIDTH], h0, s5p, n, 1)
    kc = k_buf.reshape(n, WINDOW, SWA_KV_WIDTH)
    vc = v_buf.reshape(n, WINDOW, SWA_KV_WIDTH)
    y_b = _swa_step(proj, kc, vc, sinks, start)
    x = _mm2_ln(y_a, y_b, w_out, x, g, b)
    kn = proj[:, None, S5_WIDTH + SWA_WIDTH:S5_WIDTH + SWA_WIDTH + SWA_KV_WIDTH]
    vn = proj[:, None, S5_WIDTH + SWA_WIDTH + SWA_KV_WIDTH:]
    new_k = jnp.concatenate([kc[:, 1:], kn], axis=1).reshape(k_buf.shape)
    new_v = jnp.concatenate([vc[:, 1:], vn], axis=1).reshape(v_buf.shape)
    return (x, h_t[:, :S5_FLAT].reshape(h_re.shape), h_t[:, S5_FLAT:].reshape(h_im.shape), new_k, new_v)


def _odd_weights(w_in):
    w_z = w_in[:, :SSD_INNER].astype(BF16)
    w_x = w_in[:, SSD_INNER:SSD_INNER + SSD_CONV_DIM].astype(BF16)
    w_dt = jnp.pad(w_in[:, SSD_INNER + SSD_CONV_DIM:], ((0, 0), (0, LANES - SSD_HEADS))).astype(BF16)
    return w_z, w_x, w_dt


def _odd_prompt(x, bsz, seq, w3, ssdp, w_out, g, b):
    x, st, tail = _ssd_prompt(x.reshape(bsz, seq, -1), w3, ssdp, w_out, g, b)
    return (x.reshape(bsz * seq, -1), st.reshape(bsz, SSD_HEADS, SSD_HEAD_DIM, SSD_STATE),
            tail[:, SUBLANES - (SSD_CONV - 1):])


def _odd_sample(x, ssm_all, layer, conv, w3, ssdp, w_out, g, b):
    n = x.shape[0]
    w_z, w_x, w_dt = w3
    z = _mm(x, w_z)
    xbc = _mm(x, w_x)
    dtraw = _mm(x, w_dt)
    s0_all = ssm_all.reshape(ssm_all.shape[0], n, SSD_INNER, SSD_STATE)
    y, st = _ssd_step(z, xbc, dtraw, conv, s0_all, layer, ssdp)
    x = _mm_ln(y, w_out, x, g, b)
    new_conv = jnp.concatenate([conv[:, 1:], xbc[:, None, :]], axis=1)
    return x, st.reshape(ssm_all.shape[1:]), new_conv


def kernel(x_prompt, x_sample, state_s5_re, state_s5_im, cache_swa_k, cache_swa_v, state_ssd, state_conv,
           w_in_even, s5_lambda_re, s5_lambda_im, s5_log_dt, s5_b_re, s5_b_im, s5_c_re, s5_c_im, s5_d,
           s5_w_glu, swa_sinks, w_out_even,
           w_in_odd, ssd_conv_w, ssd_conv_b, ssd_dt_bias, ssd_a_log, ssd_d, ssd_norm_w, w_out_odd,
           ln1_g, ln1_b, ln2_g, ln2_b, peer_w_q, peer_k1, peer_k2, peer_u, peer_v):
    bsz, seq, _ = x_prompt.shape
    nsm = x_sample.shape[0]
    xp = x_prompt.reshape(bsz * seq, D_MODEL)
    xs = x_sample.reshape(nsm, D_MODEL)
    u_all, vt_all = _peer_tables(peer_u, peer_v)
    p_out = [[] for _ in range(6)]
    s_out = [[] for _ in range(6)]
    for layer in range(DEPTH):
        i = layer // 2
        if layer % 2 == 0:
            w_in = w_in_even[i].astype(BF16)
            w_out = w_out_even[i].astype(BF16)
            s5p = _s5_params(s5_lambda_re[i], s5_lambda_im[i], s5_log_dt[i], s5_b_re[i], s5_b_im[i],
                             s5_c_re[i], s5_c_im[i], s5_d[i], s5_w_glu[i])
            xp, hr, hi, nk, nv = _even_prompt(xp, bsz, seq, w_in, s5p, swa_sinks[i], w_out,
                                              ln1_g[layer], ln1_b[layer])
            for lst, val in zip(p_out[:4], (hr, hi, nk, nv)):
                lst.append(val)
            xs, hr, hi, nk, nv = _even_sample(xs, state_s5_re[i], state_s5_im[i], cache_swa_k[i], cache_swa_v[i],
                                              PAST_LEN, w_in, s5p, swa_sinks[i], w_out,
                                              ln1_g[layer], ln1_b[layer])
            for lst, val in zip(s_out[:4], (hr, hi, nk, nv)):
                lst.append(val)
        else:
            w3 = _odd_weights(w_in_odd[i])
            w_out = w_out_odd[i].astype(BF16)
            ssdp = _ssd_params(ssd_conv_w[i], ssd_conv_b[i], ssd_dt_bias[i], ssd_a_log[i], ssd_d[i],
                               ssd_norm_w[i])
            xp, st, cb = _odd_prompt(xp, bsz, seq, w3, ssdp, w_out, ln1_g[layer], ln1_b[layer])
            p_out[4].append(st)
            p_out[5].append(cb)
            xs, st, cb = _odd_sample(xs, state_ssd, i, state_conv[i], w3, ssdp, w_out,
                                     ln1_g[layer], ln1_b[layer])
            s_out[4].append(st)
            s_out[5].append(cb)
        wq = peer_w_q[layer].astype(BF16)
        k1 = peer_k1[layer].astype(BF16)
        k2 = peer_k2[layer].astype(BF16)
        xp = _peer_ln(xp, wq, k1, k2, u_all, vt_all, layer, ln2_g[layer], ln2_b[layer])
        xs = _peer_ln(xs, wq, k1, k2, u_all, vt_all, layer, ln2_g[layer], ln2_b[layer])
    stack = lambda lists: tuple(jnp.stack(l) for l in lists)
    return ((xp.reshape(bsz, seq, D_MODEL), xs.reshape(nsm, 1, D_MODEL)) + stack(p_out) + stack(s_out))
```
